```python
import math
import jax
import jax.numpy as jnp
from jax import lax
import numpy as np

D_MODEL = 2048
BATCH = 2
SEQ = 4096
DEPTH = 2
DEC_BATCH = 128
DEC_SEQ = 8
PAST_LEN = 8192
PAGE_SIZE = 128

D_MIX = D_MODEL
MLA_HEADS = 8
MLA_NOPE = 128
MLA_ROPE = 64
MLA_V = 128
MLA_Q_RANK = 512
MLA_KV_RANK = 256
MLA_WIDTH = MLA_HEADS * MLA_V
MLA_IN = MLA_Q_RANK + MLA_KV_RANK + MLA_ROPE
MLA_SCALE = (MLA_NOPE + MLA_ROPE) ** -0.5
ROPE_BASE = 10000.0
QUERY_BLOCK = 128
RWKV_HEADS = 8
RWKV_HEAD = 64
RWKV_WIDTH = RWKV_HEADS * RWKV_HEAD
RWKV_W_LORA = 64
RWKV_A_LORA = 64
RWKV_G_LORA = 128
RWKV_PROJ = 3 * RWKV_WIDTH + RWKV_W_LORA + RWKV_A_LORA + RWKV_G_LORA
RWKV_GN_EPS = 64e-5
GLA_HEADS = 4
GLA_DK = 64
GLA_DV = 128
GLA_WIDTH = GLA_HEADS * GLA_DV
GLA_GATE_LORA = 16
GLA_TAU = 16.0
GLA_CHUNK = 32
GLA_PROJ = 2 * GLA_HEADS * GLA_DK + 2 * GLA_WIDTH + GLA_GATE_LORA
D_IN = MLA_IN + RWKV_PROJ + GLA_PROJ
D_FF = 5632
N_EXPERTS = 8
TOP_K = 2
EXPERT_FF = 5632
N_DENSE = (DEPTH + 1) // 2
N_MOE = DEPTH // 2
ALPHA = (2 * DEPTH) ** 0.25
BETA = (8 * DEPTH) ** -0.25
LN_EPS = 1e-5
RMS_EPS = 1e-6

kernel_name = 'hybrid_mla_rwkv7_gla_decoder_step'


def _layernorm(x, g, b):
    xf = x.astype(jnp.float32)
    mu = jnp.mean(xf, -1, keepdims=True)
    var = jnp.mean(jnp.square(xf - mu), -1, keepdims=True)
    return ((xf - mu) * lax.rsqrt(var + LN_EPS) * g + b).astype(x.dtype)


def _rmsnorm(x, g):
    xf = x.astype(jnp.float32)
    return (xf * lax.rsqrt(jnp.mean(xf * xf, -1, keepdims=True) + RMS_EPS) * g).astype(x.dtype)


def _split(z, sizes):
    idx = [int(i) for i in np.cumsum(sizes)[:-1]]
    return jnp.split(z, idx, axis=-1)


def _rope_tables(pos):
    inv = 1.0 / (ROPE_BASE ** (jnp.arange(0, MLA_ROPE, 2, dtype=jnp.float32) / MLA_ROPE))
    ang = pos.astype(jnp.float32)[:, None] * inv[None, :]
    return jnp.cos(ang), jnp.sin(ang)


def _apply_rope(x, cos, sin):
    shape = cos.shape[:1] + (1,) * (x.ndim - 3) + cos.shape[1:]
    cos, sin = cos.reshape(shape), sin.reshape(shape)
    x1, x2 = jnp.split(x.astype(jnp.float32), 2, axis=-1)
    return jnp.concatenate([x1 * cos - x2 * sin, x1 * sin + x2 * cos], -1).astype(x.dtype)


def _mla_logits(q_lat, q_pe, lat, k_pe):
    s = jnp.einsum('bqhr,bsr->bhqs', q_lat, lat) + jnp.einsum('bqhp,bsp->bhqs', q_pe, k_pe)
    return s.astype(jnp.float32) * MLA_SCALE


def _mla_prompt_attend(q_lat, q_pe, lat, k_pe):
    B, T, H, R = q_lat.shape
    nb = T // QUERY_BLOCK
    qlb = q_lat.reshape(B, nb, QUERY_BLOCK, H, R).swapaxes(0, 1)
    qpb = q_pe.reshape(B, nb, QUERY_BLOCK, H, MLA_ROPE).swapaxes(0, 1)
    kpos = jnp.arange(T)

    def block(args):
        i, ql, qp = args
        qpos = i * QUERY_BLOCK + jnp.arange(QUERY_BLOCK)
        s = _mla_logits(ql, qp, lat, k_pe)
        s = jnp.where(qpos[:, None] >= kpos[None, :], s, -jnp.inf)
        pr = jax.nn.softmax(s, axis=-1).astype(lat.dtype)
        return jnp.einsum('bhqs,bsr->bqhr', pr, lat)

    o = lax.map(block, (jnp.arange(nb), qlb, qpb))
    return o.swapaxes(0, 1).reshape(B, T, H, R)


def _mla_sample_attend(q_lat, q_pe, lat, k_pe, past_lat, past_pe):
    T = q_lat.shape[1]
    P = past_lat.shape[1]
    s_past = _mla_logits(q_lat, q_pe, past_lat, past_pe)
    s_new = _mla_logits(q_lat, q_pe, lat, k_pe)
    s_new = jnp.where(jnp.tril(jnp.ones((T, T), bool)), s_new, -jnp.inf)
    pr = jax.nn.softmax(jnp.concatenate([s_past, s_new], -1), axis=-1).astype(lat.dtype)
    return (jnp.einsum('bhqs,bsr->bqhr', pr[..., :P], past_lat)
            + jnp.einsum('bhqs,bsr->bqhr', pr[..., P:], lat))


def _mla(z, pos, l, p, past_lat, past_pe):
    B, T, _ = z.shape
    zq, zkv, zpe = _split(z, [MLA_Q_RANK, MLA_KV_RANK, MLA_ROPE])
    cos, sin = _rope_tables(pos)
    q = jnp.einsum('btr,rhd->bthd', _rmsnorm(zq, p['mla_q_norm'][l]), p['mla_w_q_up'][l])
    q_nope = q[..., :MLA_NOPE]
    q_pe = _apply_rope(q[..., MLA_NOPE:], cos, sin)
    lat = _rmsnorm(zkv, p['mla_kv_norm'][l])
    k_pe = _apply_rope(zpe, cos, sin)
    q_lat = jnp.einsum('bthd,rhd->bthr', q_nope, p['mla_w_uk'][l])
    if past_lat is None:
        o_lat = _mla_prompt_attend(q_lat, q_pe, lat, k_pe)
    else:
        o_lat = _mla_sample_attend(q_lat, q_pe, lat, k_pe, past_lat, past_pe)
    o = jnp.einsum('bthr,rhv->bthv', o_lat, p['mla_w_uv'][l]).reshape(B, T, MLA_WIDTH)
    return o, lat, k_pe


def _rwkv7_scan(r, w, k, v, a, b, S0):
    xs = tuple(t.astype(jnp.float32).swapaxes(0, 1) for t in (r, w, k, v, a, b))

    def step(S, inp):
        rt, wt, kt, vt, at, bt = inp
        sa = jnp.einsum('bhij,bhj->bhi', S, at)
        S = S * wt[:, :, None, :] + sa[..., None] * bt[:, :, None, :] + vt[..., None] * kt[:, :, None, :]
        return S, jnp.einsum('bhij,bhj->bhi', S, rt)

    S, y = lax.scan(step, S0.astype(jnp.float32), xs)
    return y.swapaxes(0, 1), S


def _rwkv7(z, shift0, S0, l, p):
    B, T, _ = z.shape
    prev = jnp.concatenate([shift0[:, None, :].astype(z.dtype), z[:, :-1]], axis=1)
    zm = z + (prev - z) * p['rwkv_mu'][l]
    r, k, v, wl, al, gl = _split(zm, [RWKV_WIDTH, RWKV_WIDTH, RWKV_WIDTH, RWKV_W_LORA, RWKV_A_LORA, RWKV_G_LORA])
    w = -jax.nn.softplus(-(p['rwkv_w0'][l] + jnp.tanh(wl) @ p['rwkv_w2'][l]).astype(jnp.float32)) - 0.5
    decay = jnp.exp(-jnp.exp(w))
    a = jax.nn.sigmoid(p['rwkv_a0'][l] + al @ p['rwkv_a2'][l])
    g = jax.nn.sigmoid(gl) @ p['rwkv_g2'][l]

    def heads(t):
        return t.reshape(B, T, RWKV_HEADS, RWKV_HEAD)

    kk = heads((k * p['rwkv_k_k'][l]).astype(jnp.float32))
    kk = kk / jnp.maximum(jnp.linalg.norm(kk, axis=-1, keepdims=True), 1e-12)
    k = k * (1.0 + (a - 1.0) * p['rwkv_k_a'][l])
    rh, kh, vh, ah = heads(r), heads(k), heads(v), heads(a)
    y, S = _rwkv7_scan(rh, heads(decay), kh, vh, -kk, kk * ah, S0)
    mu = jnp.mean(y, -1, keepdims=True)
    var = jnp.mean(jnp.square(y - mu), -1, keepdims=True)
    yn = ((y - mu) * lax.rsqrt(var + RWKV_GN_EPS)).reshape(B, T, RWKV_WIDTH) * p['rwkv_lnx_g'][l] + p['rwkv_lnx_b'][l]
    bonus = jnp.sum(rh * kh * p['rwkv_r_k'][l], -1, keepdims=True) * vh
    out = ((yn + bonus.reshape(B, T, RWKV_WIDTH)) * g).astype(z.dtype)
    return out, S, z[:, -1]


def _gla_chunked(q, k, v, log_a, S0, C):
    B, T, H, _ = q.shape
    DV = v.shape[-1]
    N = T // C

    def blk(t):
        return t.astype(jnp.float32).reshape(B, N, C, H, t.shape[-1]).transpose(1, 0, 3, 2, 4)

    q, k, v, la = blk(q), blk(k), blk(v), blk(log_a)
    cum = jnp.cumsum(la, axis=3)
    mask = jnp.tril(jnp.ones((C, C), bool))[:, :, None]
    diff = cum[..., :, None, :] - cum[..., None, :, :]
    decay = jnp.exp(jnp.where(mask, diff, -jnp.inf))
    A = jnp.einsum('nbhijd,nbhjd->nbhij', q[..., :, None, :] * decay, k)
    o_intra = jnp.einsum('nbhij,nbhjv->nbhiv', A, v)
    q_in = q * jnp.exp(cum)
    k_in = k * jnp.exp(cum[..., -1:, :] - cum)
    a_tot = jnp.exp(cum[..., -1, :])

    def step(S, inp):
        qi, ki, vi, ai = inp
        o = jnp.einsum('bhcd,bhdv->bhcv', qi, S)
        S = S * ai[..., None] + jnp.einsum('bhcd,bhcv->bhdv', ki, vi)
        return S, o

    S, o_inter = lax.scan(step, S0.astype(jnp.float32), (q_in, k_in, v, a_tot))
    o = (o_intra + o_inter).transpose(1, 0, 3, 2, 4).reshape(B, T, H, DV)
    return o, S


def _gla(z, S0, l, p):
    B, T, _ = z.shape
    q, k, v, gl, gr = _split(z, [GLA_HEADS * GLA_DK, GLA_HEADS * GLA_DK, GLA_WIDTH, GLA_GATE_LORA, GLA_WIDTH])
    log_a = jax.nn.log_sigmoid((gl @ p['gla_w_g2'][l] + p['gla_b_g'][l]).astype(jnp.float32)) / GLA_TAU
    q = q.reshape(B, T, GLA_HEADS, GLA_DK) * (GLA_DK ** -0.5)
    k = k.reshape(B, T, GLA_HEADS, GLA_DK)
    v = v.reshape(B, T, GLA_HEADS, GLA_DV)
    o, S = _gla_chunked(q, k, v, log_a.reshape(B, T, GLA_HEADS, GLA_DK), S0, math.gcd(T, GLA_CHUNK))
    o = _rmsnorm(o, p['gla_norm_g'][l]).reshape(B, T, GLA_WIDTH).astype(z.dtype) * jax.nn.silu(gr)
    return o, S


def _swiglu(h, w1, w3, w2):
    return (jax.nn.silu(h @ w1) * (h @ w3)) @ w2


def _moe(h, router, w1, w3, w2):
    logits = (h @ router).astype(jnp.float32)
    top_v, top_i = lax.top_k(logits, TOP_K)
    gates = jax.nn.softmax(top_v, axis=-1)
    combine = jnp.sum(jax.nn.one_hot(top_i, N_EXPERTS, dtype=jnp.float32) * gates[..., None], axis=-2)
    y = jnp.zeros_like(h)
    for e in range(N_EXPERTS):
        y = y + combine[..., e:e + 1].astype(h.dtype) * _swiglu(h, w1[e], w3[e], w2[e])
    return y


def _layer(x, c, pos, l, p, past_lat, past_pe, rwkv_S0, shift0, gla_S0):
    mod = (jax.nn.silu(c) @ p['ada_w'][l] + p['ada_b'][l])[:, None, :]
    sh1, sc1, g1, sh2, sc2, g2 = jnp.split(mod, 6, axis=-1)
    h = x * (1.0 + sc1) + sh1
    z = h @ p['w_in'][l]
    z_mla, z_rwkv, z_gla = _split(z, [MLA_IN, RWKV_PROJ, GLA_PROJ])
    o_mla, lat, k_pe = _mla(z_mla, pos, l, p, past_lat, past_pe)
    o_rwkv, S_r, shift = _rwkv7(z_rwkv, shift0, rwkv_S0, l, p)
    o_gla, S_g = _gla(z_gla, gla_S0, l, p)
    mix = jnp.concatenate([o_mla, o_rwkv, o_gla], axis=-1) @ p['w_out'][l]
    x = _layernorm(ALPHA * x + (1.0 + g1) * mix, p['ln1_g'][l], p['ln1_b'][l])
    h = x * (1.0 + sc2) + sh2
    if l % 2 == 0:
        f = _swiglu(h, p['ffn_w1'][l // 2], p['ffn_w3'][l // 2], p['ffn_w2'][l // 2])
    else:
        f = _moe(h, p['moe_router'][l // 2], p['moe_w1'][l // 2], p['moe_w3'][l // 2], p['moe_w2'][l // 2])
    x = _layernorm(ALPHA * x + (1.0 + g2) * f, p['ln2_g'][l], p['ln2_b'][l])
    return x, lat, k_pe, S_r, shift, S_g


def setup_inputs(seed: int = 0) -> dict:
    key = jax.random.key(seed)
    ks = iter(jax.random.split(key, 64))
    f32 = jnp.float32

    def nrm(shape, scale=1.0):
        return jax.random.normal(next(ks), shape, f32) * scale

    def gain(shape):
        return 1.0 + 0.1 * jax.random.normal(next(ks), shape, f32)

    n_pages = PAST_LEN // PAGE_SIZE
    n_used = DEC_BATCH * n_pages
    n_pool = n_used + n_used // 4
    page_table = jax.random.permutation(next(ks), n_pool)[:n_used].reshape(DEC_BATCH, n_pages).astype(jnp.int32)
    D = D_MODEL
    return {
        'x_prompt': nrm((BATCH, SEQ, D)),
        'x_sample': nrm((DEC_BATCH, DEC_SEQ, D)),
        'cache_kv_latent': nrm((DEPTH, n_pool, PAGE_SIZE, MLA_KV_RANK)),
        'cache_k_rope': nrm((DEPTH, n_pool, PAGE_SIZE, MLA_ROPE)),
        'state_rwkv': nrm((DEPTH, DEC_BATCH, RWKV_HEADS, RWKV_HEAD, RWKV_HEAD), 0.5),
        'state_rwkv_shift': nrm((DEPTH, DEC_BATCH, RWKV_PROJ)),
        'state_gla': nrm((DEPTH, DEC_BATCH, GLA_HEADS, GLA_DK, GLA_DV), 0.5),
        'page_table': page_table,
        'c_prompt': nrm((BATCH, D)),
        'c_sample': nrm((DEC_BATCH, D)),
        'w_in': nrm((DEPTH, D, D_IN), D ** -0.5),
        'w_out': nrm((DEPTH, D_MIX, D), D_MIX ** -0.5 * BETA),
        'mla_q_norm': gain((DEPTH, MLA_Q_RANK)),
        'mla_kv_norm': gain((DEPTH, MLA_KV_RANK)),
        'mla_w_q_up': nrm((DEPTH, MLA_Q_RANK, MLA_HEADS, MLA_NOPE + MLA_ROPE), MLA_Q_RANK ** -0.5),
        'mla_w_uk': nrm((DEPTH, MLA_KV_RANK, MLA_HEADS, MLA_NOPE), MLA_KV_RANK ** -0.5),
        'mla_w_uv': nrm((DEPTH, MLA_KV_RANK, MLA_HEADS, MLA_V), MLA_KV_RANK ** -0.5),
        'rwkv_mu': jax.random.uniform(next(ks), (DEPTH, RWKV_PROJ), f32),
        'rwkv_w0': jax.random.uniform(next(ks), (DEPTH, RWKV_WIDTH), f32, -6.0, -1.0),
        'rwkv_w2': nrm((DEPTH, RWKV_W_LORA, RWKV_WIDTH), 0.1),
        'rwkv_a0': nrm((DEPTH, RWKV_WIDTH), 0.1),
        'rwkv_a2': nrm((DEPTH, RWKV_A_LORA, RWKV_WIDTH), RWKV_A_LORA ** -0.5),
        'rwkv_g2': nrm((DEPTH, RWKV_G_LORA, RWKV_WIDTH), RWKV_G_LORA ** -0.5),
        'rwkv_k_k': 0.85 + nrm((DEPTH, RWKV_WIDTH), 0.05),
        'rwkv_k_a': 1.0 + nrm((DEPTH, RWKV_WIDTH), 0.05),
        'rwkv_r_k': nrm((DEPTH, RWKV_HEADS, RWKV_HEAD), 0.1),
        'rwkv_lnx_g': gain((DEPTH, RWKV_WIDTH)),
        'rwkv_lnx_b': nrm((DEPTH, RWKV_WIDTH), 0.01),
        'gla_w_g2': nrm((DEPTH, GLA_GATE_LORA, GLA_HEADS * GLA_DK), GLA_GATE_LORA ** -0.5),
        'gla_b_g': nrm((DEPTH, GLA_HEADS * GLA_DK), 0.1),
        'gla_norm_g': gain((DEPTH, GLA_DV)),
        'ada_w': nrm((DEPTH, D, 6 * D), 0.3 * D ** -0.5),
        'ada_b': nrm((DEPTH, 6 * D), 0.02),
        'ln1_g': gain((DEPTH, D)),
        'ln1_b': nrm((DEPTH, D), 0.01),
        'ln2_g': gain((DEPTH, D)),
        'ln2_b': nrm((DEPTH, D), 0.01),
        'ffn_w1': nrm((N_DENSE, D, D_FF), D ** -0.5),
        'ffn_w3': nrm((N_DENSE, D, D_FF), D ** -0.5),
        'ffn_w2': nrm((N_DENSE, D_FF, D), D_FF ** -0.5 * BETA),
        'moe_router': nrm((N_MOE, D, N_EXPERTS), D ** -0.5),
        'moe_w1': nrm((N_MOE, N_EXPERTS, D, EXPERT_FF), D ** -0.5),
        'moe_w3': nrm((N_MOE, N_EXPERTS, D, EXPERT_FF), D ** -0.5),
        'moe_w2': nrm((N_MOE, N_EXPERTS, EXPERT_FF, D), EXPERT_FF ** -0.5 * BETA),
    }


def reference(x_prompt, x_sample, cache_kv_latent, cache_k_rope, state_rwkv, state_rwkv_shift, state_gla,
              page_table, c_prompt, c_sample, w_in, w_out, mla_q_norm, mla_kv_norm, mla_w_q_up, mla_w_uk,
              mla_w_uv, rwkv_mu, rwkv_w0, rwkv_w2, rwkv_a0, rwkv_a2, rwkv_g2, rwkv_k_k, rwkv_k_a, rwkv_r_k,
              rwkv_lnx_g, rwkv_lnx_b, gla_w_g2, gla_b_g, gla_norm_g, ada_w, ada_b, ln1_g, ln1_b, ln2_g, ln2_b,
              ffn_w1, ffn_w3, ffn_w2, moe_router, moe_w1, moe_w3, moe_w2):
    p = dict(w_in=w_in, w_out=w_out, mla_q_norm=mla_q_norm, mla_kv_norm=mla_kv_norm, mla_w_q_up=mla_w_q_up,
             mla_w_uk=mla_w_uk, mla_w_uv=mla_w_uv, rwkv_mu=rwkv_mu, rwkv_w0=rwkv_w0, rwkv_w2=rwkv_w2,
             rwkv_a0=rwkv_a0, rwkv_a2=rwkv_a2, rwkv_g2=rwkv_g2, rwkv_k_k=rwkv_k_k, rwkv_k_a=rwkv_k_a,
             rwkv_r_k=rwkv_r_k, rwkv_lnx_g=rwkv_lnx_g, rwkv_lnx_b=rwkv_lnx_b, gla_w_g2=gla_w_g2,
             gla_b_g=gla_b_g, gla_norm_g=gla_norm_g, ada_w=ada_w, ada_b=ada_b, ln1_g=ln1_g, ln1_b=ln1_b,
             ln2_g=ln2_g, ln2_b=ln2_b, ffn_w1=ffn_w1, ffn_w3=ffn_w3, ffn_w2=ffn_w2, moe_router=moe_router,
             moe_w1=moe_w1, moe_w3=moe_w3, moe_w2=moe_w2)
    dt = x_prompt.dtype
    Bp, Tp, _ = x_prompt.shape
    Bs, Ts, _ = x_sample.shape
    past_len = page_table.shape[1] * PAGE_SIZE
    pos_p = jnp.arange(Tp)
    pos_s = past_len + jnp.arange(Ts)
    rwkv0 = jnp.zeros((Bp, RWKV_HEADS, RWKV_HEAD, RWKV_HEAD), jnp.float32)
    shift_init = jnp.zeros((Bp, RWKV_PROJ), dt)
    gla0 = jnp.zeros((Bp, GLA_HEADS, GLA_DK, GLA_DV), jnp.float32)
    xp, xs = x_prompt, x_sample
    lat_p, pe_p, rw_p, sh_p, gl_p = [], [], [], [], []
    lat_s, pe_s, rw_s, sh_s, gl_s = [], [], [], [], []
    for l in range(DEPTH):
        xp, lat, kpe, Sr, sh, Sg = _layer(xp, c_prompt, pos_p, l, p, None, None, rwkv0, shift_init, gla0)
        lat_p.append(lat); pe_p.append(kpe); rw_p.append(Sr.astype(dt)); sh_p.append(sh); gl_p.append(Sg.astype(dt))
        past_lat = cache_kv_latent[l, page_table].reshape(Bs, past_len, MLA_KV_RANK)
        past_pe = cache_k_rope[l, page_table].reshape(Bs, past_len, MLA_ROPE)
        xs, lat, kpe, Sr, sh, Sg = _layer(xs, c_sample, pos_s, l, p, past_lat, past_pe,
                                          state_rwkv[l], state_rwkv_shift[l], state_gla[l])
        lat_s.append(lat); pe_s.append(kpe); rw_s.append(Sr.astype(dt)); sh_s.append(sh); gl_s.append(Sg.astype(dt))
    return (xp, xs,
            jnp.stack(lat_p), jnp.stack(pe_p), jnp.stack(rw_p), jnp.stack(sh_p), jnp.stack(gl_p),
            jnp.stack(lat_s), jnp.stack(pe_s), jnp.stack(rw_s), jnp.stack(sh_s), jnp.stack(gl_s))
```

```python
import functools

import jax
import jax.numpy as jnp
from jax import lax
from jax.experimental import pallas as pl
from jax.experimental.pallas import tpu as pltpu

F32 = jnp.float32
BF16 = jnp.bfloat16
HIGHEST = lax.Precision.HIGHEST

PAGE_SIZE = 128
MLA_HEADS = 8
MLA_NOPE = 128
MLA_ROPE = 64
MLA_V = 128
MLA_Q_RANK = 512
MLA_KV_RANK = 256
MLA_IN = MLA_Q_RANK + MLA_KV_RANK + MLA_ROPE
MLA_SCALE = (MLA_NOPE + MLA_ROPE) ** -0.5
ROPE_BASE = 10000.0
RWKV_HEADS = 8
RWKV_HEAD = 64
RWKV_WIDTH = RWKV_HEADS * RWKV_HEAD
RWKV_W_LORA = 64
RWKV_A_LORA = 64
RWKV_G_LORA = 128
RWKV_PROJ = 3 * RWKV_WIDTH + RWKV_W_LORA + RWKV_A_LORA + RWKV_G_LORA
RWKV_GN_EPS = 64e-5
GLA_HEADS = 4
GLA_DK = 64
GLA_DV = 128
GLA_WIDTH = GLA_HEADS * GLA_DV
GLA_GATE_LORA = 16
GLA_TAU = 16.0
GLA_PROJ = 2 * GLA_HEADS * GLA_DK + 2 * GLA_WIDTH + GLA_GATE_LORA
N_EXPERTS = 8
LN_EPS = 1e-5
RMS_EPS = 1e-6

Z_WIDTH = 4608
Z_MLA_W = 1024
Z_RWKV_R, Z_RWKV_K, Z_RWKV_V = 2, 3, 4
Z_RWKV_LORA = 10
Z_GLA_GL = 22
Z_GLA_QK, Z_GLA_V, Z_GLA_GR = 6, 7, 8

STEP_GROUP = 8
SCAN_BATCH = 2

VMEM_LIMIT_MB = 56


def _cp(sem, vmem_mb=VMEM_LIMIT_MB):
    return pltpu.CompilerParams(dimension_semantics=sem, vmem_limit_bytes=vmem_mb * 2**20)


def _pick(n, pref):
    if n <= pref:
        return n
    b = pref - pref % 8
    while b >= 8:
        if n % b == 0:
            return b
        b -= 8
    return n


def _silu(x):
    return x * jax.nn.sigmoid(x)


def _softplus(u):
    return jnp.maximum(u, 0.0) + jnp.log(1.0 + jnp.exp(-jnp.abs(u)))


def _layernorm(y, g, b):
    mu = jnp.mean(y, -1, keepdims=True)
    d = y - mu
    var = jnp.mean(d * d, -1, keepdims=True)
    return d * lax.rsqrt(var + LN_EPS) * g + b


def _ada_body(c_ref, w_ref, b_ref, o_ref):
    c = c_ref[...]
    s = _silu(c).astype(BF16)
    o_ref[...] = jnp.dot(s, w_ref[...].astype(BF16), preferred_element_type=F32) + b_ref[...]


def _ada_call(c_all, ada_w, ada_b):
    L, D, N6 = ada_w.shape
    Mc = c_all.shape[0]
    bn = 1024
    return pl.pallas_call(
        _ada_body,
        grid=(L, N6 // bn),
        in_specs=[
            pl.BlockSpec((Mc, D), lambda l, j: (0, 0)),
            pl.BlockSpec((None, D, bn), lambda l, j: (l, 0, j)),
            pl.BlockSpec((None, 1, bn), lambda l, j: (l, 0, j)),
        ],
        out_specs=pl.BlockSpec((None, Mc, bn), lambda l, j: (l, 0, j)),
        out_shape=jax.ShapeDtypeStruct((L, Mc, N6), F32),
        compiler_params=_cp(("parallel", "parallel")),
        name="ada_mod",
    )(c_all, ada_w, ada_b.reshape(L, 1, N6))


class _Stream:
    def __init__(self, B, T, mod, per_row, pos_base):
        self.B, self.T, self.N = B, T, B * T
        self.mod = mod
        self.per_row = per_row
        self.pos_base = pos_base

    def mod_spec(self, l, k, bm, D):
        if self.per_row:
            return pl.BlockSpec((None, bm, D), lambda i, *_: (l, i, k))
        nb = self.T // bm
        return pl.BlockSpec((None, None, 1, D), lambda i, *_: (l, i // nb, 0, k))

    def row_block(self, pref):
        return _pick(self.N if self.per_row else self.T, pref)


def _inproj_body(x_ref, sh_ref, sc_ref, w_ref, o_ref, h_s):
    @pl.when(pl.program_id(1) == 0)
    def _():
        h_s[...] = (x_ref[...] * (1.0 + sc_ref[...]) + sh_ref[...]).astype(BF16)

    o_ref[...] = jnp.dot(h_s[...], w_ref[...], preferred_element_type=F32)


def _inproj_call(st, x2d, l, w_in_p):
    N, D = x2d.shape
    Wz = w_in_p.shape[-1]
    bm = st.row_block(1024)
    bn = 512
    return pl.pallas_call(
        _inproj_body,
        grid=(N // bm, Wz // bn),
        in_specs=[
            pl.BlockSpec((bm, D), lambda i, j: (i, 0)),
            st.mod_spec(l, 0, bm, D),
            st.mod_spec(l, 1, bm, D),
            pl.BlockSpec((None, D, bn), lambda i, j: (l, 0, j)),
        ],
        out_specs=pl.BlockSpec((bm, bn), lambda i, j: (i, j)),
        out_shape=jax.ShapeDtypeStruct((N, Wz), F32),
        scratch_shapes=[pltpu.VMEM((bm, D), BF16)],
        compiler_params=_cp(("parallel", "arbitrary")),
        name="in_proj",
    )(x2d, st.mod, st.mod, w_in_p)


def _rope(x, cos, sin_signed, first_half):
    w = x.shape[1]
    swapped = jnp.where(first_half, pltpu.roll(x, w - 32, 1), pltpu.roll(x, 32, 1))
    return x * cos + swapped * sin_signed


def _mla_prep_body(z_ref, gq_ref, gkv_ref, wq_ref, wuk_ref, inv_ref, qlat_ref, qpe_ref, lat_ref, kpe_ref,
                   *, bm, period, pos_base):
    i = pl.program_id(0)
    z = z_ref[...]
    zq = z[:, :MLA_Q_RANK]
    zkv = z[:, MLA_Q_RANK:MLA_Q_RANK + MLA_KV_RANK]
    zpe = z[:, MLA_Q_RANK + MLA_KV_RANK:MLA_Q_RANK + MLA_KV_RANK + 128]
    qn = zq * lax.rsqrt(jnp.mean(zq * zq, -1, keepdims=True) + RMS_EPS) * gq_ref[...]
    q = jnp.dot(qn.astype(BF16), wq_ref[...], preferred_element_type=F32)
    lat_ref[...] = zkv * lax.rsqrt(jnp.mean(zkv * zkv, -1, keepdims=True) + RMS_EPS) * gkv_ref[...]

    row = lax.broadcasted_iota(jnp.int32, (bm, 128), 0) + i * bm
    pos = (pos_base + jnp.bitwise_and(row, period - 1)).astype(F32)
    ang = pos * inv_ref[...]
    cos = jnp.cos(ang)
    sin = jnp.sin(ang)
    first = jnp.bitwise_and(lax.broadcasted_iota(jnp.int32, (bm, 128), 1), 63) < 32
    sin_s = jnp.where(first, -sin, sin)
    kpe_ref[...] = _rope(zpe, cos, sin_s, first)[:, :MLA_ROPE]

    npe = MLA_HEADS * MLA_ROPE // 128
    cos4 = jnp.concatenate([cos] * npe, axis=1)
    sin4 = jnp.concatenate([sin_s] * npe, axis=1)
    first4 = jnp.bitwise_and(lax.broadcasted_iota(jnp.int32, (bm, 128 * npe), 1), 63) < 32
    q_pe = q[:, MLA_HEADS * MLA_NOPE:]
    qpe_ref[...] = (_rope(q_pe, cos4, sin4, first4) * MLA_SCALE).astype(qpe_ref.dtype)
    for h in range(MLA_HEADS):
        qh = q[:, h * MLA_NOPE:(h + 1) * MLA_NOPE].astype(BF16)
        ql = jnp.dot(qh, wuk_ref[h], preferred_element_type=F32) * MLA_SCALE
        qlat_ref[:, h * MLA_KV_RANK:(h + 1) * MLA_KV_RANK] = ql.astype(qlat_ref.dtype)


def _mla_prep_call(st, z, gq, gkv, wq_p, wuk_p, inv128, q_dtype):
    N = z.shape[0]
    bm = _pick(N, 512)
    HR = MLA_HEADS * MLA_KV_RANK
    HP = MLA_HEADS * MLA_ROPE
    assert st.T & (st.T - 1) == 0
    body = functools.partial(_mla_prep_body, bm=bm, period=st.T, pos_base=st.pos_base)
    const = lambda shape: pl.BlockSpec(shape, lambda i: (0,) * len(shape))
    return pl.pallas_call(
        body,
        grid=(N // bm,),
        in_specs=[
            pl.BlockSpec((bm, Z_MLA_W), lambda i: (i, 0)),
            const((1, MLA_Q_RANK)), const((1, MLA_KV_RANK)),
            const(wq_p.shape), const(wuk_p.shape), const((1, 128)),
        ],
        out_specs=[
            pl.BlockSpec((bm, HR), lambda i: (i, 0)),
            pl.BlockSpec((bm, HP), lambda i: (i, 0)),
            pl.BlockSpec((bm, MLA_KV_RANK), lambda i: (i, 0)),
            pl.BlockSpec((bm, MLA_ROPE), lambda i: (i, 0)),
        ],
        out_shape=[
            jax.ShapeDtypeStruct((N, HR), q_dtype),
            jax.ShapeDtypeStruct((N, HP), q_dtype),
            jax.ShapeDtypeStruct((N, MLA_KV_RANK), F32),
            jax.ShapeDtypeStruct((N, MLA_ROPE), F32),
        ],
        compiler_params=_cp(("parallel",)),
        name="mla_prep",
    )(z, gq, gkv, wq_p, wuk_p, inv128)


_NT = (((1,), (1,)), ((), ()))


def _softmax_update(s, h, m_s, l_s, acc_s, values_bf16):
    m_prev = m_s[h]
    m_new = jnp.maximum(m_prev, jnp.max(s, -1, keepdims=True))
    alpha = jnp.exp(m_prev - m_new)
    p = jnp.exp(s - m_new[:, :1])
    l_s[h] = alpha * l_s[h] + jnp.sum(p, -1, keepdims=True)
    acc_s[h] = acc_s[h] * alpha[:, :1] + jnp.dot(p.astype(BF16), values_bf16, preferred_element_type=F32)
    m_s[h] = m_new


def _attn_prompt_body(ql_ref, qp_ref, lat_ref, kpe_ref, wuv_ref, o_ref, m_s, l_s, acc_s, *, bq):
    qi = pl.program_id(1)
    ki = pl.program_id(2)

    @pl.when(ki == 0)
    def _():
        m_s[...] = jnp.full(m_s.shape, -jnp.inf, F32)
        l_s[...] = jnp.zeros(l_s.shape, F32)
        acc_s[...] = jnp.zeros(acc_s.shape, F32)

    def step(masked):
        latb = lat_ref[...].astype(BF16)
        kpb = kpe_ref[...].astype(BF16)
        if masked:
            causal = (lax.broadcasted_iota(jnp.int32, (bq, bq), 0)
                      >= lax.broadcasted_iota(jnp.int32, (bq, bq), 1))
        for h in range(MLA_HEADS):
            s = (lax.dot_general(ql_ref[:, h * MLA_KV_RANK:(h + 1) * MLA_KV_RANK], latb, _NT,
                                 preferred_element_type=F32)
                 + lax.dot_general(qp_ref[:, h * MLA_ROPE:(h + 1) * MLA_ROPE], kpb, _NT,
                                   preferred_element_type=F32))
            if masked:
                s = jnp.where(causal, s, -jnp.inf)
            _softmax_update(s, h, m_s, l_s, acc_s, latb)

    @pl.when(ki < qi)
    def _():
        step(False)

    @pl.when(ki == qi)
    def _():
        step(True)
        for h in range(MLA_HEADS):
            o_lat = (acc_s[h] / l_s[h][:, :1]).astype(BF16)
            o_ref[:, h * MLA_V:(h + 1) * MLA_V] = jnp.dot(
                o_lat, wuv_ref[h], preferred_element_type=F32).astype(o_ref.dtype)


def _attn_prompt_call(st, qlat, qpe, lat, kpe, wuv_p):
    B, T = st.B, st.T
    bq = _pick(T, 512)
    nq = T // bq
    HR = MLA_HEADS * MLA_KV_RANK
    HP = MLA_HEADS * MLA_ROPE
    kv = lambda b, qi, ki: (b * nq + jnp.minimum(ki, qi), 0)
    return pl.pallas_call(
        functools.partial(_attn_prompt_body, bq=bq),
        grid=(B, nq, nq),
        in_specs=[
            pl.BlockSpec((bq, HR), lambda b, qi, ki: (b * nq + qi, 0)),
            pl.BlockSpec((bq, HP), lambda b, qi, ki: (b * nq + qi, 0)),
            pl.BlockSpec((bq, MLA_KV_RANK), kv),
            pl.BlockSpec((bq, MLA_ROPE), kv),
            pl.BlockSpec(wuv_p.shape, lambda b, qi, ki: (0, 0, 0)),
        ],
        out_specs=pl.BlockSpec((bq, MLA_HEADS * MLA_V), lambda b, qi, ki: (b * nq + qi, 0)),
        out_shape=jax.ShapeDtypeStruct((B * T, MLA_HEADS * MLA_V), BF16),
        scratch_shapes=[
            pltpu.VMEM((MLA_HEADS, bq, 128), F32),
            pltpu.VMEM((MLA_HEADS, bq, 128), F32),
            pltpu.VMEM((MLA_HEADS, bq, MLA_KV_RANK), F32),
        ],
        compiler_params=_cp(("parallel", "parallel", "arbitrary")),
        name="attn_prompt",
    )(qlat, qpe, lat, kpe, wuv_p)


def _attn_sample_body(pt_ref, ql_ref, qp_ref, latn_ref, kpen_ref, wuv_ref, *rest, pg, ts):
    lat_pages = rest[:pg]
    pe_pages = rest[pg:2 * pg]
    o_ref = rest[2 * pg]
    ql_s, qp_s, m_s, l_s, acc_s, newk_s, newp_s = rest[2 * pg + 1:]
    p_id = pl.program_id(1)
    rows = MLA_HEADS * ts

    @pl.when(p_id == 0)
    def _():
        for h in range(MLA_HEADS):
            ql_s[h * ts:(h + 1) * ts, :] = ql_ref[:, h * MLA_KV_RANK:(h + 1) * MLA_KV_RANK]
            qp_s[h * ts:(h + 1) * ts, :] = qp_ref[:, h * MLA_ROPE:(h + 1) * MLA_ROPE]
        m_s[...] = jnp.full(m_s.shape, -jnp.inf, F32)
        l_s[...] = jnp.zeros(l_s.shape, F32)
        acc_s[...] = jnp.zeros(acc_s.shape, F32)

    qlb = ql_s[...].astype(BF16)
    qpb = qp_s[...].astype(BF16)

    def update(s, values):
        m_prev = m_s[...]
        m_new = jnp.maximum(m_prev, jnp.max(s, -1, keepdims=True))
        alpha = jnp.exp(m_prev - m_new)
        p = jnp.exp(s - m_new[:, :1])
        l_s[...] = alpha * l_s[...] + jnp.sum(p, -1, keepdims=True)
        pv = jnp.zeros((rows, MLA_KV_RANK), F32)
        for g, v in enumerate(values):
            pv = pv + jnp.dot(p[:, g * PAGE_SIZE:(g + 1) * PAGE_SIZE].astype(BF16), v,
                              preferred_element_type=F32)
        acc_s[...] = acc_s[...] * alpha[:, :1] + pv
        m_s[...] = m_new

    def scores(latb, kpb):
        return (lax.dot_general(qlb, latb, _NT, preferred_element_type=F32)
                + lax.dot_general(qpb, kpb, _NT, preferred_element_type=F32))

    lats = [r[...].astype(BF16) for r in lat_pages]
    s_all = jnp.concatenate([scores(lats[g], pe_pages[g][...].astype(BF16)) for g in range(pg)], axis=1)
    update(s_all, lats)

    @pl.when(p_id == pl.num_programs(1) - 1)
    def _():
        newk_s[...] = jnp.zeros(newk_s.shape, F32)
        newp_s[...] = jnp.zeros(newp_s.shape, F32)
        newk_s[0:ts, :] = latn_ref[...]
        newp_s[0:ts, :] = kpen_ref[...]
        latb = newk_s[...].astype(BF16)
        s = scores(latb, newp_s[...].astype(BF16))
        t_q = jnp.bitwise_and(lax.broadcasted_iota(jnp.int32, (rows, PAGE_SIZE), 0), ts - 1)
        t_k = lax.broadcasted_iota(jnp.int32, (rows, PAGE_SIZE), 1)
        update(jnp.where(t_k <= t_q, s, -jnp.inf), [latb])
        o_lat = (acc_s[...] / l_s[...][:, :1]).astype(BF16)
        for h in range(MLA_HEADS):
            o_h = jnp.dot(o_lat, wuv_ref[h], preferred_element_type=F32)
            o_ref[:, h * MLA_V:(h + 1) * MLA_V] = o_h[h * ts:(h + 1) * ts, :].astype(o_ref.dtype)


def _attn_sample_call(st, l, qlat, qpe, lat, kpe, cache_lat, cache_pe, page_table, wuv_p):
    S, ts = st.B, st.T
    P = page_table.shape[1]
    pg = 8 if P % 8 == 0 else (2 if P % 2 == 0 else 1)
    HR = MLA_HEADS * MLA_KV_RANK
    HP = MLA_HEADS * MLA_ROPE
    rows = MLA_HEADS * ts
    assert ts & (ts - 1) == 0

    def page_spec(width, g):
        return pl.BlockSpec((None, None, PAGE_SIZE, width), lambda s, p, pt: (l, pt[s, p * pg + g], 0, 0))

    grid_spec = pltpu.PrefetchScalarGridSpec(
        num_scalar_prefetch=1,
        grid=(S, P // pg),
        in_specs=[
            pl.BlockSpec((ts, HR), lambda s, p, pt: (s, 0)),
            pl.BlockSpec((ts, HP), lambda s, p, pt: (s, 0)),
            pl.BlockSpec((ts, MLA_KV_RANK), lambda s, p, pt: (s, 0)),
            pl.BlockSpec((ts, MLA_ROPE), lambda s, p, pt: (s, 0)),
            pl.BlockSpec(wuv_p.shape, lambda s, p, pt: (0, 0, 0)),
        ] + [page_spec(MLA_KV_RANK, g) for g in range(pg)] + [page_spec(MLA_ROPE, g) for g in range(pg)],
        out_specs=pl.BlockSpec((ts, MLA_HEADS * MLA_V), lambda s, p, pt: (s, 0)),
        scratch_shapes=[
            pltpu.VMEM((rows, MLA_KV_RANK), F32),
            pltpu.VMEM((rows, MLA_ROPE), F32),
            pltpu.VMEM((rows, 128), F32),
            pltpu.VMEM((rows, 128), F32),
            pltpu.VMEM((rows, MLA_KV_RANK), F32),
            pltpu.VMEM((PAGE_SIZE, MLA_KV_RANK), F32),
            pltpu.VMEM((PAGE_SIZE, MLA_ROPE), F32),
        ],
    )
    return pl.pallas_call(
        functools.partial(_attn_sample_body, pg=pg, ts=ts),
        grid_spec=grid_spec,
        out_shape=jax.ShapeDtypeStruct((S * ts, MLA_HEADS * MLA_V), F32),
        compiler_params=_cp(("parallel", "arbitrary")),
        name="attn_sample",
    )(page_table, qlat, qpe, lat, kpe, wuv_p, *([cache_lat] * pg), *([cache_pe] * pg))


def _pair_masks(rows):
    lane = lax.broadcasted_iota(jnp.int32, (rows, 128), 1)
    sub = lax.broadcasted_iota(jnp.int32, (rows, 128), 0)
    return lane, sub


def _rwkv_body(zr_ref, zk_ref, zv_ref, zl_ref, sh0_ref, s0_ref, mu_ref, w0_ref, w2_ref, a0_ref, a2_ref,
               g2_ref, kk_ref, ka_ref, rk_ref, lng_ref, lnb_ref, blk_ref, o_ref, sout_ref,
               carry_s, st_s, r_s, w_s, k_s, v_s, a_s, b_s, g_s, y_s, *, bb, tc):
    c = pl.program_id(1)
    W = RWKV_WIDTH
    npair = RWKV_HEADS // 2

    @pl.when(c == 0)
    def _():
        for b in range(bb):
            carry_s[b] = jnp.broadcast_to(sh0_ref[b], (8, RWKV_PROJ))
        st_s[...] = s0_ref[...].reshape(st_s.shape)

    first_row = lax.broadcasted_iota(jnp.int32, (tc, 1), 0) == 0
    blk = blk_ref[...]

    def head_sum(x):
        return jnp.dot(x, blk, precision=HIGHEST, preferred_element_type=F32)

    def mix(z, off):
        prev = jnp.where(first_row, carry_s[b][0:1, off:off + z.shape[1]], pltpu.roll(z, 1, 0))
        return z + (prev - z) * mu_ref[:, off:off + z.shape[1]]

    for b in range(bb):
        zr, zk, zv, zl = zr_ref[b], zk_ref[b], zv_ref[b], zl_ref[b]
        r = mix(zr, 0)
        k = mix(zk, W)
        v = mix(zv, 2 * W)
        lo = mix(zl, 3 * W)
        carry_s[b] = jnp.concatenate([zr[tc - 8:], zk[tc - 8:], zv[tc - 8:], zl[tc - 8:]], axis=1)[7:8] \
            + jnp.zeros((8, RWKV_PROJ), F32)
        wa = lo[:, :128]
        lw = jnp.dot(jnp.tanh(wa).astype(BF16), w2_ref[...], preferred_element_type=F32)
        la = jnp.dot(wa.astype(BF16), a2_ref[...], preferred_element_type=F32)
        g = jnp.dot(jax.nn.sigmoid(lo[:, 128:]).astype(BF16), g2_ref[...], preferred_element_type=F32)
        w = -_softplus(-(w0_ref[...] + lw)) - 0.5
        decay = jnp.exp(-jnp.exp(w))
        a = jax.nn.sigmoid(a0_ref[...] + la)
        kk = k * kk_ref[...]
        kk = kk / jnp.maximum(jnp.sqrt(head_sum(kk * kk)), 1e-12)
        k = k * (1.0 + (a - 1.0) * ka_ref[...])
        r_s[b], w_s[b], k_s[b], v_s[b] = r, decay, k, v
        a_s[b], b_s[b], g_s[b] = -kk, kk * a, g

    lane, sub = _pair_masks(RWKV_HEAD)
    lo_half = lane < RWKV_HEAD
    diag = jnp.bitwise_and(lane, RWKV_HEAD - 1) == sub
    diag_lo = jnp.logical_and(diag, lo_half)
    diag_hi = jnp.logical_and(diag, jnp.logical_not(lo_half))

    def seg_sum(x):
        s_lo = jnp.sum(jnp.where(lo_half, x, 0.0), -1, keepdims=True)
        s_hi = jnp.sum(jnp.where(lo_half, 0.0, x), -1, keepdims=True)
        return jnp.where(lo_half, s_lo, s_hi)

    def steps(gi, carry):
        t0 = pl.multiple_of(gi * STEP_GROUP, STEP_GROUP)
        for b in range(bb):
            for p in range(npair):
                cs = slice(128 * p, 128 * (p + 1))
                tile = lambda ref: ref[b, pl.ds(t0, STEP_GROUP), cs]
                rt, wt, kt, vt, at, bt = (tile(x) for x in (r_s, w_s, k_s, v_s, a_s, b_s))
                S = st_s[b * npair + p]
                ys = []
                for tt in range(STEP_GROUP):
                    row = lambda x: x[tt:tt + 1, :]
                    v_b = jnp.broadcast_to(row(vt), (RWKV_HEAD, 128))
                    v_col = jnp.where(
                        lo_half,
                        jnp.sum(jnp.where(diag_lo, v_b, 0.0), -1, keepdims=True),
                        jnp.sum(jnp.where(diag_hi, v_b, 0.0), -1, keepdims=True))
                    sa = seg_sum(S * row(at))
                    S = S * row(wt) + sa * row(bt) + v_col * row(kt)
                    y_col = seg_sum(S * row(rt))
                    ys.append(jnp.sum(jnp.where(diag, y_col, 0.0), 0, keepdims=True))
                st_s[b * npair + p] = S
                y_s[b, pl.ds(t0, STEP_GROUP), cs] = jnp.concatenate(ys, axis=0)
        return carry

    lax.fori_loop(0, tc // STEP_GROUP, steps, 0)

    for b in range(bb):
        y = y_s[b]
        mean = head_sum(y) * (1.0 / RWKV_HEAD)
        d = y - mean
        var = head_sum(d * d) * (1.0 / RWKV_HEAD)
        yn = d * lax.rsqrt(var + RWKV_GN_EPS) * lng_ref[...] + lnb_ref[...]
        bonus = head_sum(r_s[b] * k_s[b] * rk_ref[...]) * v_s[b]
        o_ref[b] = ((yn + bonus) * g_s[b]).astype(o_ref.dtype)

    @pl.when(c == pl.num_programs(1) - 1)
    def _():
        sout_ref[...] = st_s[...].reshape(sout_ref.shape)


def _pack_pairs(s):
    B, H, R, C = s.shape
    return s.reshape(B, H // 2, 2, R, C).transpose(0, 1, 3, 2, 4).reshape(B, H // 2, R, 2 * C)


def _unpack_pairs(s):
    B, HP, R, C2 = s.shape
    return s.reshape(B, HP, R, 2, C2 // 2).transpose(0, 1, 3, 2, 4).reshape(B, HP * 2, R, C2 // 2)


def _rwkv_call(st, z, shift0, s0, prm, blk, out_dtype):
    B, T = st.B, st.T
    z3 = z.reshape(B, T, Z_WIDTH)
    bb = SCAN_BATCH
    assert B % bb == 0 and T % STEP_GROUP == 0
    tc = _pick(T, 256)
    W = RWKV_WIDTH
    npair = RWKV_HEADS // 2
    s0p = _pack_pairs(s0)
    zspec = lambda w, idx: pl.BlockSpec((bb, tc, w), lambda g, c: (g, c, idx))
    const = lambda a: pl.BlockSpec(a.shape, lambda g, c: (0,) * a.ndim)
    consts = [prm[k] for k in ("mu", "w0", "w2", "a0", "a2", "g2", "k_k", "k_a", "r_k", "lnx_g", "lnx_b")] + [blk]
    o, sout = pl.pallas_call(
        functools.partial(_rwkv_body, bb=bb, tc=tc),
        grid=(B // bb, T // tc),
        in_specs=[
            zspec(W, Z_RWKV_R), zspec(W, Z_RWKV_K), zspec(W, Z_RWKV_V), zspec(256, Z_RWKV_LORA),
            pl.BlockSpec((bb, 1, RWKV_PROJ), lambda g, c: (g, 0, 0)),
            pl.BlockSpec((bb, npair, RWKV_HEAD, 128), lambda g, c: (g, 0, 0, 0)),
        ] + [const(a) for a in consts],
        out_specs=[
            pl.BlockSpec((bb, tc, W), lambda g, c: (g, c, 0)),
            pl.BlockSpec((bb, npair, RWKV_HEAD, 128), lambda g, c: (g, 0, 0, 0)),
        ],
        out_shape=[
            jax.ShapeDtypeStruct((B, T, W), out_dtype),
            jax.ShapeDtypeStruct((B, npair, RWKV_HEAD, 128), F32),
        ],
        scratch_shapes=[
            pltpu.VMEM((bb, 8, RWKV_PROJ), F32),
            pltpu.VMEM((bb * npair, RWKV_HEAD, 128), F32),
        ] + [pltpu.VMEM((bb, tc, W), F32)] * 8,
        compiler_params=_cp(("parallel", "arbitrary")),
        name="rwkv7",
    )(z3, z3, z3, z3, shift0.reshape(B, 1, RWKV_PROJ), s0p, *consts)
    return o.reshape(B * T, W), _unpack_pairs(sout)


def _gla_body(zqk_ref, zv_ref, zgr_ref, zgl_ref, s0_ref, wg_ref, bg_ref, ng_ref, o_ref, sout_ref,
              st_s, q_s, k_s, a_s, v_s, y_s, *, bb, tc):
    c = pl.program_id(1)
    npair = GLA_HEADS // 2
    HK = GLA_HEADS * GLA_DK

    @pl.when(c == 0)
    def _():
        st_s[...] = s0_ref[...].reshape(st_s.shape)

    for b in range(bb):
        zqk = zqk_ref[b]
        gate = jnp.dot(zgl_ref[b].astype(BF16), wg_ref[...], preferred_element_type=F32) + bg_ref[...]
        log_a = -_softplus(-gate) * (1.0 / GLA_TAU)
        q_s[b] = zqk[:, :HK] * (GLA_DK ** -0.5)
        k_s[b] = zqk[:, HK:]
        a_s[b] = jnp.exp(log_a)
        v_s[b] = zv_ref[b]

    lane, sub = _pair_masks(GLA_DV)
    lo_half = lane < GLA_DK
    eye = lane == sub

    def to_col(row):
        return jnp.sum(jnp.where(eye, jnp.broadcast_to(row, (GLA_DV, 128)), 0.0), -1, keepdims=True)

    def to_row(col):
        return jnp.sum(jnp.where(eye, col, 0.0), 0, keepdims=True)

    def steps(gi, carry):
        t0 = pl.multiple_of(gi * STEP_GROUP, STEP_GROUP)
        for b in range(bb):
            for p in range(npair):
                cs = slice(128 * p, 128 * (p + 1))
                v0 = slice(2 * GLA_DV * p, 2 * GLA_DV * p + GLA_DV)
                v1 = slice(2 * GLA_DV * p + GLA_DV, 2 * GLA_DV * (p + 1))
                qt, kt, at = (x[b, pl.ds(t0, STEP_GROUP), cs] for x in (q_s, k_s, a_s))
                v0t = v_s[b, pl.ds(t0, STEP_GROUP), v0]
                v1t = v_s[b, pl.ds(t0, STEP_GROUP), v1]
                S = st_s[b * npair + p]
                y0, y1 = [], []
                for tt in range(STEP_GROUP):
                    row = lambda x: x[tt:tt + 1, :]
                    v_col = jnp.where(lo_half, to_col(row(v0t)), to_col(row(v1t)))
                    S = S * row(at) + v_col * row(kt)
                    sq = S * row(qt)
                    y0.append(to_row(jnp.sum(jnp.where(lo_half, sq, 0.0), -1, keepdims=True)))
                    y1.append(to_row(jnp.sum(jnp.where(lo_half, 0.0, sq), -1, keepdims=True)))
                st_s[b * npair + p] = S
                y_s[b, pl.ds(t0, STEP_GROUP), v0] = jnp.concatenate(y0, axis=0)
                y_s[b, pl.ds(t0, STEP_GROUP), v1] = jnp.concatenate(y1, axis=0)
        return carry

    lax.fori_loop(0, tc // STEP_GROUP, steps, 0)

    for b in range(bb):
        gr = zgr_ref[b]
        for h in range(GLA_HEADS):
            hs = slice(h * GLA_DV, (h + 1) * GLA_DV)
            o = y_s[b, :, hs]
            on = o * lax.rsqrt(jnp.mean(o * o, -1, keepdims=True) + RMS_EPS) * ng_ref[...]
            o_ref[b, :, hs] = (on * _silu(gr[:, hs])).astype(o_ref.dtype)

    @pl.when(c == pl.num_programs(1) - 1)
    def _():
        sout_ref[...] = st_s[...].reshape(sout_ref.shape)


def _gla_call(st, z, s0, wg_p, bg, ng, out_dtype):
    B, T = st.B, st.T
    z3 = z.reshape(B, T, Z_WIDTH)
    bb = SCAN_BATCH
    assert B % bb == 0 and T % STEP_GROUP == 0
    tc = _pick(T, 256)
    npair = GLA_HEADS // 2
    HK = GLA_HEADS * GLA_DK
    s0p = _pack_pairs(s0.transpose(0, 1, 3, 2))
    zspec = lambda w, idx: pl.BlockSpec((bb, tc, w), lambda g, c: (g, c, idx))
    const = lambda a: pl.BlockSpec(a.shape, lambda g, c: (0,) * a.ndim)
    o, sout = pl.pallas_call(
        functools.partial(_gla_body, bb=bb, tc=tc),
        grid=(B // bb, T // tc),
        in_specs=[
            zspec(512, Z_GLA_QK), zspec(512, Z_GLA_V), zspec(512, Z_GLA_GR), zspec(128, Z_GLA_GL),
            pl.BlockSpec((bb, npair, GLA_DV, 128), lambda g, c: (g, 0, 0, 0)),
            const(wg_p), const(bg), const(ng),
        ],
        out_specs=[
            pl.BlockSpec((bb, tc, GLA_WIDTH), lambda g, c: (g, c, 0)),
            pl.BlockSpec((bb, npair, GLA_DV, 128), lambda g, c: (g, 0, 0, 0)),
        ],
        out_shape=[
            jax.ShapeDtypeStruct((B, T, GLA_WIDTH), out_dtype),
            jax.ShapeDtypeStruct((B, npair, GLA_DV, 128), F32),
        ],
        scratch_shapes=[
            pltpu.VMEM((bb * npair, GLA_DV, 128), F32),
            pltpu.VMEM((bb, tc, HK), F32), pltpu.VMEM((bb, tc, HK), F32), pltpu.VMEM((bb, tc, HK), F32),
            pltpu.VMEM((bb, tc, GLA_WIDTH), F32), pltpu.VMEM((bb, tc, GLA_WIDTH), F32),
        ],
        compiler_params=_cp(("parallel", "arbitrary")),
        name="gla",
    )(z3, z3, z3, z3, s0p, wg_p, bg, ng)
    return o.reshape(B * T, GLA_WIDTH), _unpack_pairs(sout).transpose(0, 1, 3, 2)


def _outproj_body(om_ref, or_ref, og_ref, x_ref, g1_ref, sh2_ref, sc2_ref, w_ref, lng_ref, lnb_ref,
                  x1_ref, h2_ref, *, alpha):
    wm = MLA_HEADS * MLA_V
    mix = (jnp.dot(om_ref[...].astype(BF16), w_ref[0:wm, :], preferred_element_type=F32)
           + jnp.dot(or_ref[...].astype(BF16), w_ref[wm:wm + RWKV_WIDTH, :], preferred_element_type=F32)
           + jnp.dot(og_ref[...].astype(BF16), w_ref[wm + RWKV_WIDTH:, :], preferred_element_type=F32))
    x1 = _layernorm(alpha * x_ref[...] + (1.0 + g1_ref[...]) * mix, lng_ref[...], lnb_ref[...])
    x1_ref[...] = x1
    h2_ref[...] = (x1 * (1.0 + sc2_ref[...]) + sh2_ref[...]).astype(BF16)


def _outproj_call(st, o_mla, o_rwkv, o_gla, x2d, l, w_out_b, ln_g, ln_b, alpha):
    N, D = x2d.shape
    bm = st.row_block(256)
    row = lambda w: pl.BlockSpec((bm, w), lambda i: (i, 0))
    return pl.pallas_call(
        functools.partial(_outproj_body, alpha=alpha),
        grid=(N // bm,),
        in_specs=[
            row(o_mla.shape[1]), row(o_rwkv.shape[1]), row(o_gla.shape[1]), row(D),
            st.mod_spec(l, 2, bm, D), st.mod_spec(l, 3, bm, D), st.mod_spec(l, 4, bm, D),
            pl.BlockSpec((None,) + w_out_b.shape[1:], lambda i: (l, 0, 0)),
            pl.BlockSpec((None, 1, D), lambda i: (l, 0, 0)),
            pl.BlockSpec((None, 1, D), lambda i: (l, 0, 0)),
        ],
        out_specs=[row(D), row(D)],
        out_shape=[jax.ShapeDtypeStruct((N, D), F32), jax.ShapeDtypeStruct((N, D), BF16)],
        compiler_params=_cp(("parallel",)),
        name="out_proj",
    )(o_mla, o_rwkv, o_gla, x2d, st.mod, st.mod, st.mod, w_out_b, ln_g, ln_b)


def _router_body(x1_ref, sh2_ref, sc2_ref, wr_ref, o_ref):
    h = x1_ref[...] * (1.0 + sc2_ref[...]) + sh2_ref[...]
    logits = jnp.dot(h, wr_ref[...], precision=HIGHEST, preferred_element_type=F32)
    lane = lax.broadcasted_iota(jnp.int32, logits.shape, 1)
    lane_f = lane.astype(F32)
    lg = jnp.where(lane < N_EXPERTS, logits, -jnp.inf)
    v1 = jnp.max(lg, -1, keepdims=True)
    i1 = jnp.min(jnp.where(lg == v1, lane_f, 128.0), -1, keepdims=True)
    lg2 = jnp.where(lane_f == i1, -jnp.inf, lg)
    v2 = jnp.max(lg2, -1, keepdims=True)
    i2 = jnp.min(jnp.where(lg2 == v2, lane_f, 128.0), -1, keepdims=True)
    e = jnp.exp(v2 - v1)
    g1 = 1.0 / (1.0 + e)
    g2 = e * g1
    o_ref[...] = jnp.where(lane == 0, g1, jnp.where(lane == 1, g2, jnp.where(
        lane == 2, i1, jnp.where(lane == 3, i2, 0.0))))


def _router_call(st, x1, l, wr_p):
    N, D = x1.shape
    bm = st.row_block(512)
    return pl.pallas_call(
        _router_body,
        grid=(N // bm,),
        in_specs=[
            pl.BlockSpec((bm, D), lambda i: (i, 0)),
            st.mod_spec(l, 3, bm, D), st.mod_spec(l, 4, bm, D),
            pl.BlockSpec(wr_p.shape, lambda i: (0, 0)),
        ],
        out_specs=pl.BlockSpec((bm, 128), lambda i: (i, 0)),
        out_shape=jax.ShapeDtypeStruct((N, 128), F32),
        compiler_params=_cp(("parallel",)),
        name="moe_router",
    )(x1, st.mod, st.mod, wr_p)


def _ffn_up_body(te_ref, tv_ref, h_ref, w1_ref, w3_ref, o_ref):
    t = pl.program_id(0)

    @pl.when(tv_ref[t] != 0)
    def _():
        h = h_ref[...]
        a = jnp.dot(h, w1_ref[...].astype(BF16), preferred_element_type=F32)
        b = jnp.dot(h, w3_ref[...].astype(BF16), preferred_element_type=F32)
        o_ref[...] = (_silu(a) * b).astype(BF16)

    @pl.when(tv_ref[t] == 0)
    def _():
        o_ref[...] = jnp.zeros(o_ref.shape, BF16)


def _ffn_down_body(te_ref, tv_ref, g_ref, w2_ref, o_ref):
    t = pl.program_id(0)

    @pl.when(tv_ref[t] != 0)
    def _():
        o_ref[...] = jnp.dot(g_ref[...], w2_ref[...].astype(BF16), preferred_element_type=F32)

    @pl.when(tv_ref[t] == 0)
    def _():
        o_ref[...] = jnp.zeros(o_ref.shape, F32)


def _ffn_call(h_rows, tile_expert, tile_valid, w1, w3, w2, bm):
    R, D = h_rows.shape
    F = w1.shape[-1]
    nt = R // bm
    bf = 512
    up = pl.pallas_call(
        _ffn_up_body,
        grid_spec=pltpu.PrefetchScalarGridSpec(
            num_scalar_prefetch=2,
            grid=(nt, F // bf),
            in_specs=[
                pl.BlockSpec((bm, D), lambda t, j, te, tv: (t, 0)),
                pl.BlockSpec((None, D, bf), lambda t, j, te, tv: (te[t], 0, j)),
                pl.BlockSpec((None, D, bf), lambda t, j, te, tv: (te[t], 0, j)),
            ],
            out_specs=pl.BlockSpec((bm, bf), lambda t, j, te, tv: (t, j)),
        ),
        out_shape=jax.ShapeDtypeStruct((R, F), BF16),
        compiler_params=_cp(("parallel", "arbitrary")),
        name="ffn_up",
    )(tile_expert, tile_valid, h_rows, w1, w3)
    bn = 256
    return pl.pallas_call(
        _ffn_down_body,
        grid_spec=pltpu.PrefetchScalarGridSpec(
            num_scalar_prefetch=2,
            grid=(nt, D // bn),
            in_specs=[
                pl.BlockSpec((bm, F), lambda t, n, te, tv: (t, 0)),
                pl.BlockSpec((None, F, bn), lambda t, n, te, tv: (te[t], 0, n)),
            ],
            out_specs=pl.BlockSpec((bm, bn), lambda t, n, te, tv: (t, n)),
        ),
        out_shape=jax.ShapeDtypeStruct((R, D), F32),
        compiler_params=_cp(("parallel", "arbitrary")),
        name="ffn_down",
    )(tile_expert, tile_valid, up, w2)


def _combine_body(*refs, alpha, nterm):
    x1_ref, g2_ref, lng_ref, lnb_ref = refs[:4]
    f_refs = refs[4:4 + nterm]
    o_ref = refs[-1]
    if nterm == 1:
        f = f_refs[0][...]
    else:
        gates = refs[4 + nterm][...]
        f = f_refs[0][...] * gates[:, 0:1] + f_refs[1][...] * gates[:, 1:2]
    o_ref[...] = _layernorm(alpha * x1_ref[...] + (1.0 + g2_ref[...]) * f, lng_ref[...], lnb_ref[...])


def _combine_call(st, x1, l, ln_g, ln_b, alpha, terms, gates=None):
    N, D = x1.shape
    bm = st.row_block(512)
    row = lambda w: pl.BlockSpec((bm, w), lambda i: (i, 0))
    args = [x1, st.mod, ln_g, ln_b] + list(terms) + ([gates] if gates is not None else [])
    return pl.pallas_call(
        functools.partial(_combine_body, alpha=alpha, nterm=len(terms)),
        grid=(N // bm,),
        in_specs=[
            row(D), st.mod_spec(l, 5, bm, D),
            pl.BlockSpec((None, 1, D), lambda i: (l, 0, 0)),
            pl.BlockSpec((None, 1, D), lambda i: (l, 0, 0)),
        ] + [row(D)] * len(terms) + ([row(128)] if gates is not None else []),
        out_specs=row(D),
        out_shape=jax.ShapeDtypeStruct((N, D), F32),
        compiler_params=_cp(("parallel",)),
        name="ffn_residual",
    )(*args)


def _repack_w_in(w_in):
    L, D, _ = w_in.shape
    w = w_in.astype(BF16)
    zeros = lambda n: jnp.zeros((L, D, n), BF16)
    r0 = MLA_IN
    g0 = MLA_IN + RWKV_PROJ
    HK = GLA_HEADS * GLA_DK
    gq, gk, gv = g0, g0 + HK, g0 + 2 * HK
    ggl = gv + GLA_WIDTH
    ggr = ggl + GLA_GATE_LORA
    parts = [
        w[:, :, :MLA_IN], zeros(Z_MLA_W - MLA_IN),
        w[:, :, r0:r0 + RWKV_PROJ],
        w[:, :, ggl:ggr], zeros(256 - GLA_GATE_LORA),
        w[:, :, gq:gv], w[:, :, gv:ggl], w[:, :, ggr:ggr + GLA_WIDTH],
    ]
    out = jnp.concatenate(parts, axis=-1)
    assert out.shape[-1] == Z_WIDTH
    return out


def _moe_plan(gi, bm):
    n = gi.shape[0]
    E = N_EXPERTS
    experts = gi[:, 2:4].astype(jnp.int32).reshape(-1)
    onehot = (experts[:, None] == jnp.arange(E, dtype=jnp.int32)[None, :]).astype(jnp.int32)
    counts = jnp.sum(onehot, axis=0)
    tiles = (counts + bm - 1) // bm
    tile_end = jnp.cumsum(tiles)
    tile_start = tile_end - tiles
    rank = jnp.take_along_axis(jnp.cumsum(onehot, axis=0) - 1, experts[:, None], axis=1)[:, 0]
    pos = tile_start[experts] * bm + rank
    nt = (2 * n + bm - 1) // bm + E
    src = jnp.zeros((nt * bm,), jnp.int32).at[pos].set(jnp.arange(2 * n, dtype=jnp.int32) // 2)
    t_ids = jnp.arange(nt, dtype=jnp.int32)
    tile_expert = jnp.minimum(jnp.sum((t_ids[:, None] >= tile_end[None, :]).astype(jnp.int32), axis=1), E - 1)
    tile_valid = (t_ids < tile_end[-1]).astype(jnp.int32)
    return src, pos.reshape(n, 2), tile_expert, tile_valid


def kernel(x_prompt, x_sample, cache_kv_latent, cache_k_rope, state_rwkv, state_rwkv_shift, state_gla, page_table, c_prompt, c_sample, w_in, w_out, mla_q_norm, mla_kv_norm, mla_w_q_up, mla_w_uk, mla_w_uv, rwkv_mu, rwkv_w0, rwkv_w2, rwkv_a0, rwkv_a2, rwkv_g2, rwkv_k_k, rwkv_k_a, rwkv_r_k, rwkv_lnx_g, rwkv_lnx_b, gla_w_g2, gla_b_g, gla_norm_g, ada_w, ada_b, ln1_g, ln1_b, ln2_g, ln2_b, ffn_w1, ffn_w3, ffn_w2, moe_router, moe_w1, moe_w3, moe_w2):
    Bp, Tp, D = x_prompt.shape
    Bs, Ts, _ = x_sample.shape
    L = w_in.shape[0]
    past_len = page_table.shape[1] * PAGE_SIZE
    alpha = (2 * L) ** 0.25
    dt = x_prompt.dtype

    n_c = Bp + Bs
    c_all = jnp.concatenate([c_prompt, c_sample, jnp.zeros((-n_c % 8, D), F32)], axis=0)
    mod = _ada_call(c_all, ada_w, ada_b)
    st_p = _Stream(Bp, Tp, mod[:, :Bp].reshape(L, Bp, 1, 6 * D), False, 0)
    st_s = _Stream(Bs, Ts, jnp.repeat(mod[:, Bp:Bp + Bs], Ts, axis=1), True, past_len)

    w_in_p = _repack_w_in(w_in)
    w_out_b = w_out.astype(BF16)
    wq = mla_w_q_up.astype(BF16)
    wq_p = jnp.concatenate([wq[..., :MLA_NOPE].reshape(L, MLA_Q_RANK, -1),
                            wq[..., MLA_NOPE:].reshape(L, MLA_Q_RANK, -1)], axis=-1)
    wuk_p = mla_w_uk.astype(BF16).transpose(0, 2, 3, 1)
    wuv_p = mla_w_uv.astype(BF16).transpose(0, 2, 1, 3)
    inv = 1.0 / (ROPE_BASE ** (jnp.arange(0, MLA_ROPE, 2, dtype=F32) / MLA_ROPE))
    inv128 = jnp.tile(inv, 4).reshape(1, 128)
    zpad = lambda a, n: jnp.concatenate([a, jnp.zeros((L, n) + a.shape[2:], a.dtype)], axis=1)
    zpre = lambda a, n: jnp.concatenate([jnp.zeros((L, n) + a.shape[2:], a.dtype), a], axis=1)
    rw_w2 = zpad(rwkv_w2.astype(BF16), RWKV_A_LORA)
    rw_a2 = zpre(rwkv_a2.astype(BF16), RWKV_W_LORA)
    gla_wg = zpad(gla_w_g2.astype(BF16), 128 - GLA_GATE_LORA)
    head_of = jnp.arange(RWKV_WIDTH, dtype=jnp.int32) // RWKV_HEAD
    blk = (head_of[:, None] == head_of[None, :]).astype(F32)
    row1 = lambda a: a.reshape(1, -1)

    xp = x_prompt.reshape(Bp * Tp, D)
    xs = x_sample.reshape(Bs * Ts, D)
    zeros_shift = jnp.zeros((Bp, RWKV_PROJ), dt)
    zeros_rwkv = jnp.zeros((Bp, RWKV_HEADS, RWKV_HEAD, RWKV_HEAD), F32)
    zeros_gla = jnp.zeros((Bp, GLA_HEADS, GLA_DK, GLA_DV), F32)
    outs_p = [[] for _ in range(5)]
    outs_s = [[] for _ in range(5)]

    for l in range(L):
        rw = dict(mu=row1(rwkv_mu[l]), w0=row1(rwkv_w0[l]), w2=rw_w2[l], a0=row1(rwkv_a0[l]), a2=rw_a2[l],
                  g2=rwkv_g2[l].astype(BF16), k_k=row1(rwkv_k_k[l]), k_a=row1(rwkv_k_a[l]),
                  r_k=row1(rwkv_r_k[l]), lnx_g=row1(rwkv_lnx_g[l]), lnx_b=row1(rwkv_lnx_b[l]))
        streams = []
        for st, x2d, sample in ((st_p, xp, False), (st_s, xs, True)):
            z = _inproj_call(st, x2d, l, w_in_p)
            qlat, qpe, lat, kpe = _mla_prep_call(st, z, row1(mla_q_norm[l]), row1(mla_kv_norm[l]),
                                                 wq_p[l], wuk_p[l], inv128, F32 if sample else BF16)
            if sample:
                o_mla = _attn_sample_call(st, l, qlat, qpe, lat, kpe, cache_kv_latent, cache_k_rope,
                                          page_table, wuv_p[l])
                shift0, s_r0, s_g0 = state_rwkv_shift[l], state_rwkv[l], state_gla[l]
            else:
                o_mla = _attn_prompt_call(st, qlat, qpe, lat, kpe, wuv_p[l])
                shift0, s_r0, s_g0 = zeros_shift, zeros_rwkv, zeros_gla
            o_dt = F32 if sample else BF16
            o_rwkv, s_r = _rwkv_call(st, z, shift0, s_r0, rw, blk, o_dt)
            o_gla, s_g = _gla_call(st, z, s_g0, gla_wg[l], row1(gla_b_g[l]), row1(gla_norm_g[l]), o_dt)
            x1, h2 = _outproj_call(st, o_mla, o_rwkv, o_gla, x2d, l, w_out_b, ln1_g.reshape(L, 1, D),
                                   ln1_b.reshape(L, 1, D), alpha)
            shift = z.reshape(st.B, st.T, Z_WIDTH)[:, -1, Z_MLA_W:Z_MLA_W + RWKV_PROJ]
            acc = outs_s if sample else outs_p
            for lst, val in zip(acc, (lat.reshape(st.B, st.T, -1), kpe.reshape(st.B, st.T, -1),
                                      s_r.astype(dt), shift, s_g.astype(dt))):
                lst.append(val)
            streams.append((st, x1, h2))

        (_, x1p, h2p), (_, x1s, h2s) = streams
        h_all = jnp.concatenate([h2p, h2s], axis=0)
        n_all = h_all.shape[0]
        n_p = h2p.shape[0]
        bm = _pick(n_all, 1024)
        lng, lnb = ln2_g.reshape(L, 1, D), ln2_b.reshape(L, 1, D)
        if l % 2 == 0:
            e = l // 2
            nt = n_all // bm
            f = _ffn_call(h_all, jnp.zeros((nt,), jnp.int32), jnp.ones((nt,), jnp.int32),
                          ffn_w1[e:e + 1], ffn_w3[e:e + 1], ffn_w2[e:e + 1], bm)
            xp = _combine_call(st_p, x1p, l, lng, lnb, alpha, [f[:n_p]])
            xs = _combine_call(st_s, x1s, l, lng, lnb, alpha, [f[n_p:]])
        else:
            e = l // 2
            wr_p = jnp.concatenate([moe_router[e], jnp.zeros((D, 128 - N_EXPERTS), F32)], axis=1)
            gi = jnp.concatenate([_router_call(st_p, x1p, l, wr_p), _router_call(st_s, x1s, l, wr_p)], axis=0)
            src, pos, tile_expert, tile_valid = _moe_plan(gi, bm)
            f_sorted = _ffn_call(h_all[src], tile_expert, tile_valid, moe_w1[e], moe_w3[e], moe_w2[e], bm)
            f0 = f_sorted[pos[:, 0]]
            f1 = f_sorted[pos[:, 1]]
            xp = _combine_call(st_p, x1p, l, lng, lnb, alpha, [f0[:n_p], f1[:n_p]], gi[:n_p])
            xs = _combine_call(st_s, x1s, l, lng, lnb, alpha, [f0[n_p:], f1[n_p:]], gi[n_p:])

    stack = lambda lst: jnp.stack(lst)
    return (xp.reshape(Bp, Tp, D), xs.reshape(Bs, Ts, D),
            *[stack(v) for v in outs_p], *[stack(v) for v in outs_s])
```

```python
import functools

import jax
import jax.numpy as jnp
from jax import lax
from jax.experimental import pallas as pl
from jax.experimental.pallas import tpu as pltpu

F32 = jnp.float32
BF16 = jnp.bfloat16
HIGHEST = lax.Precision.HIGHEST

PAGE_SIZE = 128
MLA_HEADS = 8
MLA_NOPE = 128
MLA_ROPE = 64
MLA_V = 128
MLA_Q_RANK = 512
MLA_KV_RANK = 256
MLA_IN = MLA_Q_RANK + MLA_KV_RANK + MLA_ROPE
MLA_SCALE = (MLA_NOPE + MLA_ROPE) ** -0.5
ROPE_BASE = 10000.0
RWKV_HEADS = 8
RWKV_HEAD = 64
RWKV_WIDTH = RWKV_HEADS * RWKV_HEAD
RWKV_W_LORA = 64
RWKV_A_LORA = 64
RWKV_G_LORA = 128
RWKV_PROJ = 3 * RWKV_WIDTH + RWKV_W_LORA + RWKV_A_LORA + RWKV_G_LORA
RWKV_GN_EPS = 64e-5
GLA_HEADS = 4
GLA_DK = 64
GLA_DV = 128
GLA_WIDTH = GLA_HEADS * GLA_DV
GLA_GATE_LORA = 16
GLA_TAU = 16.0
GLA_PROJ = 2 * GLA_HEADS * GLA_DK + 2 * GLA_WIDTH + GLA_GATE_LORA
N_EXPERTS = 8
LN_EPS = 1e-5
RMS_EPS = 1e-6

Z_WIDTH = 4608
Z_MLA_W = 1024
Z_RWKV_R, Z_RWKV_K, Z_RWKV_V = 2, 3, 4
Z_RWKV_LORA = 10
Z_GLA_GL = 22
Z_GLA_QK, Z_GLA_V, Z_GLA_GR = 6, 7, 8

SAMPLE_PAGES_PER_STEP = 32
SCAN_BLOCK = 16
SCAN_BATCH = 4

VMEM_LIMIT_MB = 56


def _cp(sem, vmem_mb=VMEM_LIMIT_MB):
    return pltpu.CompilerParams(dimension_semantics=sem, vmem_limit_bytes=vmem_mb * 2**20)


def _pick(n, pref):
    if n <= pref:
        return n
    b = pref - pref % 8
    while b >= 8:
        if n % b == 0:
            return b
        b -= 8
    return n


def _silu(x):
    return x * jax.nn.sigmoid(x)


def _softplus(u):
    return jnp.maximum(u, 0.0) + jnp.log(1.0 + jnp.exp(-jnp.abs(u)))


def _lane_group_sum(x, group):
    axis = x.ndim - 1
    width = x.shape[axis]
    lane = lax.broadcasted_iota(jnp.int32, x.shape, axis)
    s = 1
    while s < group:
        partner = jnp.where(jnp.bitwise_and(lane, s) == 0,
                            pltpu.roll(x, width - s, axis), pltpu.roll(x, s, axis))
        x = x + partner
        s *= 2
    return x


def _layernorm(y, g, b):
    mu = jnp.mean(y, -1, keepdims=True)
    d = y - mu
    var = jnp.mean(d * d, -1, keepdims=True)
    return d * lax.rsqrt(var + LN_EPS) * g + b


def _ada_body(c_ref, w_ref, b_ref, o_ref):
    c = c_ref[...]
    s = _silu(c).astype(BF16)
    o_ref[...] = jnp.dot(s, w_ref[...].astype(BF16), preferred_element_type=F32) + b_ref[...]


def _ada_call(c_all, ada_w, ada_b):
    L, D, N6 = ada_w.shape
    Mc = c_all.shape[0]
    bn = 1024
    return pl.pallas_call(
        _ada_body,
        grid=(L, N6 // bn),
        in_specs=[
            pl.BlockSpec((Mc, D), lambda l, j: (0, 0)),
            pl.BlockSpec((None, D, bn), lambda l, j: (l, 0, j)),
            pl.BlockSpec((None, 1, bn), lambda l, j: (l, 0, j)),
        ],
        out_specs=pl.BlockSpec((None, Mc, bn), lambda l, j: (l, 0, j)),
        out_shape=jax.ShapeDtypeStruct((L, Mc, N6), F32),
        compiler_params=_cp(("parallel", "parallel")),
        name="ada_mod",
    )(c_all, ada_w, ada_b.reshape(L, 1, N6))


class _Stream:
    def __init__(self, B, T, mod, per_row, pos_base):
        self.B, self.T, self.N = B, T, B * T
        self.mod = mod
        self.per_row = per_row
        self.pos_base = pos_base

    def mod_spec(self, l, k, bm, D):
        if self.per_row:
            return pl.BlockSpec((None, bm, D), lambda i, *_: (l, i, k))
        nb = self.T // bm
        return pl.BlockSpec((None, None, 1, D), lambda i, *_: (l, i // nb, 0, k))

    def row_block(self, pref):
        return _pick(self.N if self.per_row else self.T, pref)


def _inproj_body(x_ref, sh_ref, sc_ref, w_ref, o_ref, h_s):
    @pl.when(pl.program_id(1) == 0)
    def _():
        h_s[...] = (x_ref[...] * (1.0 + sc_ref[...]) + sh_ref[...]).astype(BF16)

    o_ref[...] = jnp.dot(h_s[...], w_ref[...], preferred_element_type=F32)


def _inproj_call(st, x2d, l, w_in_p):
    N, D = x2d.shape
    Wz = w_in_p.shape[-1]
    bm = st.row_block(1024)
    bn = 512
    return pl.pallas_call(
        _inproj_body,
        grid=(N // bm, Wz // bn),
        in_specs=[
            pl.BlockSpec((bm, D), lambda i, j: (i, 0)),
            st.mod_spec(l, 0, bm, D),
            st.mod_spec(l, 1, bm, D),
            pl.BlockSpec((None, D, bn), lambda i, j: (l, 0, j)),
        ],
        out_specs=pl.BlockSpec((bm, bn), lambda i, j: (i, j)),
        out_shape=jax.ShapeDtypeStruct((N, Wz), F32),
        scratch_shapes=[pltpu.VMEM((bm, D), BF16)],
        compiler_params=_cp(("parallel", "arbitrary")),
        name="in_proj",
    )(x2d, st.mod, st.mod, w_in_p)


def _rope(x, cos, sin_signed, first_half):
    w = x.shape[1]
    swapped = jnp.where(first_half, pltpu.roll(x, w - 32, 1), pltpu.roll(x, 32, 1))
    return x * cos + swapped * sin_signed


def _mla_prep_body(z_ref, gq_ref, gkv_ref, wq_ref, wuk_ref, inv_ref, qlat_ref, qpe_ref, lat_ref, kpe_ref,
                   *, bm, period, pos_base):
    i = pl.program_id(0)
    z = z_ref[...]
    zq = z[:, :MLA_Q_RANK]
    zkv = z[:, MLA_Q_RANK:MLA_Q_RANK + MLA_KV_RANK]
    zpe = z[:, MLA_Q_RANK + MLA_KV_RANK:MLA_Q_RANK + MLA_KV_RANK + 128]
    qn = zq * lax.rsqrt(jnp.mean(zq * zq, -1, keepdims=True) + RMS_EPS) * gq_ref[...]
    q = jnp.dot(qn.astype(BF16), wq_ref[...], preferred_element_type=F32)
    lat_ref[...] = zkv * lax.rsqrt(jnp.mean(zkv * zkv, -1, keepdims=True) + RMS_EPS) * gkv_ref[...]

    row = lax.broadcasted_iota(jnp.int32, (bm, 128), 0) + i * bm
    pos = (pos_base + jnp.bitwise_and(row, period - 1)).astype(F32)
    ang = pos * inv_ref[...]
    cos = jnp.cos(ang)
    sin = jnp.sin(ang)
    first = jnp.bitwise_and(lax.broadcasted_iota(jnp.int32, (bm, 128), 1), 63) < 32
    sin_s = jnp.where(first, -sin, sin)
    kpe_ref[...] = _rope(zpe, cos, sin_s, first)[:, :MLA_ROPE]

    npe = MLA_HEADS * MLA_ROPE // 128
    cos4 = jnp.concatenate([cos] * npe, axis=1)
    sin4 = jnp.concatenate([sin_s] * npe, axis=1)
    first4 = jnp.bitwise_and(lax.broadcasted_iota(jnp.int32, (bm, 128 * npe), 1), 63) < 32
    q_pe = q[:, MLA_HEADS * MLA_NOPE:]
    qpe_ref[...] = (_rope(q_pe, cos4, sin4, first4) * MLA_SCALE).astype(qpe_ref.dtype)
    for h in range(MLA_HEADS):
        qh = q[:, h * MLA_NOPE:(h + 1) * MLA_NOPE].astype(BF16)
        ql = jnp.dot(qh, wuk_ref[h], preferred_element_type=F32) * MLA_SCALE
        qlat_ref[:, h * MLA_KV_RANK:(h + 1) * MLA_KV_RANK] = ql.astype(qlat_ref.dtype)


def _mla_prep_call(st, z, gq, gkv, wq_p, wuk_p, inv128, q_dtype):
    N = z.shape[0]
    bm = _pick(N, 512)
    HR = MLA_HEADS * MLA_KV_RANK
    HP = MLA_HEADS * MLA_ROPE
    assert st.T & (st.T - 1) == 0
    body = functools.partial(_mla_prep_body, bm=bm, period=st.T, pos_base=st.pos_base)
    const = lambda shape: pl.BlockSpec(shape, lambda i: (0,) * len(shape))
    return pl.pallas_call(
        body,
        grid=(N // bm,),
        in_specs=[
            pl.BlockSpec((bm, Z_MLA_W), lambda i: (i, 0)),
            const((1, MLA_Q_RANK)), const((1, MLA_KV_RANK)),
            const(wq_p.shape), const(wuk_p.shape), const((1, 128)),
        ],
        out_specs=[
            pl.BlockSpec((bm, HR), lambda i: (i, 0)),
            pl.BlockSpec((bm, HP), lambda i: (i, 0)),
            pl.BlockSpec((bm, MLA_KV_RANK), lambda i: (i, 0)),
            pl.BlockSpec((bm, MLA_ROPE), lambda i: (i, 0)),
        ],
        out_shape=[
            jax.ShapeDtypeStruct((N, HR), q_dtype),
            jax.ShapeDtypeStruct((N, HP), q_dtype),
            jax.ShapeDtypeStruct((N, MLA_KV_RANK), F32),
            jax.ShapeDtypeStruct((N, MLA_ROPE), F32),
        ],
        compiler_params=_cp(("parallel",)),
        name="mla_prep",
    )(z, gq, gkv, wq_p, wuk_p, inv128)


_NT = (((1,), (1,)), ((), ()))


def _softmax_update(s, h, m_s, l_s, acc_s, values_bf16):
    m_prev = m_s[h]
    m_new = jnp.maximum(m_prev, jnp.max(s, -1, keepdims=True))
    alpha = jnp.exp(m_prev - m_new)
    p = jnp.exp(s - m_new[:, :1])
    l_s[h] = alpha * l_s[h] + jnp.sum(p, -1, keepdims=True)
    acc_s[h] = acc_s[h] * alpha[:, :1] + jnp.dot(p.astype(BF16), values_bf16, preferred_element_type=F32)
    m_s[h] = m_new


def _attn_prompt_body(ql_ref, qp_ref, lat_ref, kpe_ref, wuv_ref, o_ref, m_s, l_s, acc_s, *, bq):
    qi = pl.program_id(1)
    ki = pl.program_id(2)

    @pl.when(ki == 0)
    def _():
        m_s[...] = jnp.full(m_s.shape, -jnp.inf, F32)
        l_s[...] = jnp.zeros(l_s.shape, F32)
        acc_s[...] = jnp.zeros(acc_s.shape, F32)

    def step(masked):
        latb = lat_ref[...].astype(BF16)
        kpb = kpe_ref[...].astype(BF16)
        if masked:
            causal = (lax.broadcasted_iota(jnp.int32, (bq, bq), 0)
                      >= lax.broadcasted_iota(jnp.int32, (bq, bq), 1))
        for h in range(MLA_HEADS):
            s = (lax.dot_general(ql_ref[:, h * MLA_KV_RANK:(h + 1) * MLA_KV_RANK], latb, _NT,
                                 preferred_element_type=F32)
                 + lax.dot_general(qp_ref[:, h * MLA_ROPE:(h + 1) * MLA_ROPE], kpb, _NT,
                                   preferred_element_type=F32))
            if masked:
                s = jnp.where(causal, s, -jnp.inf)
            _softmax_update(s, h, m_s, l_s, acc_s, latb)

    @pl.when(ki < qi)
    def _():
        step(False)

    @pl.when(ki == qi)
    def _():
        step(True)
        for h in range(MLA_HEADS):
            o_lat = (acc_s[h] / l_s[h][:, :1]).astype(BF16)
            o_ref[:, h * MLA_V:(h + 1) * MLA_V] = jnp.dot(
                o_lat, wuv_ref[h], preferred_element_type=F32).astype(o_ref.dtype)


def _attn_prompt_call(st, qlat, qpe, lat, kpe, wuv_p):
    B, T = st.B, st.T
    bq = _pick(T, 512)
    nq = T // bq
    HR = MLA_HEADS * MLA_KV_RANK
    HP = MLA_HEADS * MLA_ROPE
    kv = lambda b, qi, ki: (b * nq + jnp.minimum(ki, qi), 0)
    return pl.pallas_call(
        functools.partial(_attn_prompt_body, bq=bq),
        grid=(B, nq, nq),
        in_specs=[
            pl.BlockSpec((bq, HR), lambda b, qi, ki: (b * nq + qi, 0)),
            pl.BlockSpec((bq, HP), lambda b, qi, ki: (b * nq + qi, 0)),
            pl.BlockSpec((bq, MLA_KV_RANK), kv),
            pl.BlockSpec((bq, MLA_ROPE), kv),
            pl.BlockSpec(wuv_p.shape, lambda b, qi, ki: (0, 0, 0)),
        ],
        out_specs=pl.BlockSpec((bq, MLA_HEADS * MLA_V), lambda b, qi, ki: (b * nq + qi, 0)),
        out_shape=jax.ShapeDtypeStruct((B * T, MLA_HEADS * MLA_V), BF16),
        scratch_shapes=[
            pltpu.VMEM((MLA_HEADS, bq, 128), F32),
            pltpu.VMEM((MLA_HEADS, bq, 128), F32),
            pltpu.VMEM((MLA_HEADS, bq, MLA_KV_RANK), F32),
        ],
        compiler_params=_cp(("parallel", "parallel", "arbitrary")),
        name="attn_prompt",
    )(qlat, qpe, lat, kpe, wuv_p)


def _attn_sample_body(pt_ref, ql_ref, qp_ref, latn_ref, kpen_ref, wuv_ref, *rest, pg, ts):
    lat_pages = rest[:pg]
    pe_pages = rest[pg:2 * pg]
    o_ref = rest[2 * pg]
    ql_s, qp_s, m_s, l_s, acc_s, newk_s, newp_s = rest[2 * pg + 1:]
    p_id = pl.program_id(1)
    rows = MLA_HEADS * ts

    @pl.when(p_id == 0)
    def _():
        for h in range(MLA_HEADS):
            ql_s[h * ts:(h + 1) * ts, :] = ql_ref[:, h * MLA_KV_RANK:(h + 1) * MLA_KV_RANK]
            qp_s[h * ts:(h + 1) * ts, :] = qp_ref[:, h * MLA_ROPE:(h + 1) * MLA_ROPE]
        m_s[...] = jnp.full(m_s.shape, -jnp.inf, F32)
        l_s[...] = jnp.zeros(l_s.shape, F32)
        acc_s[...] = jnp.zeros(acc_s.shape, F32)

    qlb = ql_s[...].astype(BF16)
    qpb = qp_s[...].astype(BF16)

    def update(s, values):
        m_prev = m_s[...]
        m_new = jnp.maximum(m_prev, jnp.max(s, -1, keepdims=True))
        alpha = jnp.exp(m_prev - m_new)
        p = jnp.exp(s - m_new[:, :1])
        l_s[...] = alpha * l_s[...] + jnp.sum(p, -1, keepdims=True)
        pv = jnp.zeros((rows, MLA_KV_RANK), F32)
        for g, v in enumerate(values):
            pv = pv + jnp.dot(p[:, g * PAGE_SIZE:(g + 1) * PAGE_SIZE].astype(BF16), v,
                              preferred_element_type=F32)
        acc_s[...] = acc_s[...] * alpha[:, :1] + pv
        m_s[...] = m_new

    def scores(latb, kpe_t):
        return (lax.dot_general(qlb, latb, _NT, preferred_element_type=F32)
                + jnp.dot(qpb, kpe_t, preferred_element_type=F32))

    lats = [r[...].astype(BF16) for r in lat_pages]
    s_all = jnp.concatenate([scores(lats[g], pe_pages[g][...].astype(BF16)) for g in range(pg)], axis=1)
    update(s_all, lats)

    @pl.when(p_id == pl.num_programs(1) - 1)
    def _():
        newk_s[...] = jnp.zeros(newk_s.shape, F32)
        newp_s[...] = jnp.zeros(newp_s.shape, F32)
        newk_s[0:ts, :] = latn_ref[...]
        newp_s[0:ts, :] = kpen_ref[...]
        latb = newk_s[...].astype(BF16)
        s = scores(latb, newp_s[...].T.astype(BF16))
        t_q = jnp.bitwise_and(lax.broadcasted_iota(jnp.int32, (rows, PAGE_SIZE), 0), ts - 1)
        t_k = lax.broadcasted_iota(jnp.int32, (rows, PAGE_SIZE), 1)
        update(jnp.where(t_k <= t_q, s, -jnp.inf), [latb])
        o_lat = (acc_s[...] / l_s[...][:, :1]).astype(BF16)
        for h in range(MLA_HEADS):
            o_h = jnp.dot(o_lat, wuv_ref[h], preferred_element_type=F32)
            o_ref[:, h * MLA_V:(h + 1) * MLA_V] = o_h[h * ts:(h + 1) * ts, :].astype(o_ref.dtype)


def _attn_sample_call(st, l, qlat, qpe, lat, kpe, cache_lat, cache_pe, page_table, wuv_p):
    S, ts = st.B, st.T
    P = page_table.shape[1]
    pg = next(n for n in (SAMPLE_PAGES_PER_STEP, 8, 2, 1) if P % n == 0)
    HR = MLA_HEADS * MLA_KV_RANK
    HP = MLA_HEADS * MLA_ROPE
    rows = MLA_HEADS * ts
    assert ts & (ts - 1) == 0
    cache_pe_t = jnp.swapaxes(cache_pe, 2, 3)

    def page_spec(shape, g):
        return pl.BlockSpec((None, None) + shape, lambda s, p, pt: (l, pt[s, p * pg + g], 0, 0))

    grid_spec = pltpu.PrefetchScalarGridSpec(
        num_scalar_prefetch=1,
        grid=(S, P // pg),
        in_specs=[
            pl.BlockSpec((ts, HR), lambda s, p, pt: (s, 0)),
            pl.BlockSpec((ts, HP), lambda s, p, pt: (s, 0)),
            pl.BlockSpec((ts, MLA_KV_RANK), lambda s, p, pt: (s, 0)),
            pl.BlockSpec((ts, MLA_ROPE), lambda s, p, pt: (s, 0)),
            pl.BlockSpec(wuv_p.shape, lambda s, p, pt: (0, 0, 0)),
        ] + [page_spec((PAGE_SIZE, MLA_KV_RANK), g) for g in range(pg)]
          + [page_spec((MLA_ROPE, PAGE_SIZE), g) for g in range(pg)],
        out_specs=pl.BlockSpec((ts, MLA_HEADS * MLA_V), lambda s, p, pt: (s, 0)),
        scratch_shapes=[
            pltpu.VMEM((rows, MLA_KV_RANK), F32),
            pltpu.VMEM((rows, MLA_ROPE), F32),
            pltpu.VMEM((rows, 128), F32),
            pltpu.VMEM((rows, 128), F32),
            pltpu.VMEM((rows, MLA_KV_RANK), F32),
            pltpu.VMEM((PAGE_SIZE, MLA_KV_RANK), F32),
            pltpu.VMEM((PAGE_SIZE, MLA_ROPE), F32),
        ],
    )
    return pl.pallas_call(
        functools.partial(_attn_sample_body, pg=pg, ts=ts),
        grid_spec=grid_spec,
        out_shape=jax.ShapeDtypeStruct((S * ts, MLA_HEADS * MLA_V), F32),
        compiler_params=_cp(("parallel", "arbitrary")),
        name="attn_sample",
    )(page_table, qlat, qpe, lat, kpe, wuv_p, *([cache_lat] * pg), *([cache_pe_t] * pg))


def _scan_tiling(B, T):
    bb = min(B, SCAN_BATCH)
    sb = SCAN_BLOCK if T % SCAN_BLOCK == 0 else 8
    assert B % bb == 0 and T % sb == 0
    return bb, _pick(T, 256), sb


def _block_cumsum_matrices(tc, sb):
    t = jnp.arange(tc, dtype=jnp.int32)
    same = (t[:, None] // sb) == (t[None, :] // sb)
    return (jnp.logical_and(same, t[None, :] <= t[:, None]).astype(F32), same.astype(F32))


def _rwkv_body(zr_ref, zk_ref, zv_ref, zl_ref, sh0_ref, s0_ref, mu_ref, w0_ref, w2_ref, a0_ref, a2_ref,
               g2_ref, kk_ref, ka_ref, rk_ref, lng_ref, lnb_ref, blk_ref, ltri_ref, bones_ref, o_ref, sout_ref,
               carry_s, st_s, r_s, k_s, v_s, g_s, ae_s, re_s, bq_s, kq_s, bh_s, kh_s, gb_s, y_s, *, bb, tc, sb):
    c = pl.program_id(1)
    W = RWKV_WIDTH
    npair = RWKV_HEADS // 2

    @pl.when(c == 0)
    def _():
        for b in range(bb):
            carry_s[b] = jnp.broadcast_to(sh0_ref[b], (8, RWKV_PROJ))
        st_s[...] = s0_ref[...].reshape(st_s.shape)

    first_row = lax.broadcasted_iota(jnp.int32, (tc, 1), 0) == 0
    def head_sum(x):
        if x.shape[0] <= 64:
            return _lane_group_sum(x, RWKV_HEAD)
        hi = x.astype(BF16)
        lo = (x - hi.astype(F32)).astype(BF16)
        return (jnp.dot(hi, blk_ref[...], preferred_element_type=F32)
                + jnp.dot(lo, blk_ref[...], preferred_element_type=F32))

    def mix(z, off):
        prev = jnp.where(first_row, carry_s[b][0:1, off:off + z.shape[1]], pltpu.roll(z, 1, 0))
        return z + (prev - z) * mu_ref[:, off:off + z.shape[1]]

    for b in range(bb):
        zr, zk, zv, zl = zr_ref[b], zk_ref[b], zv_ref[b], zl_ref[b]
        r = mix(zr, 0)
        k = mix(zk, W)
        v = mix(zv, 2 * W)
        lo = mix(zl, 3 * W)
        carry_s[b] = jnp.concatenate([zr[tc - 8:], zk[tc - 8:], zv[tc - 8:], zl[tc - 8:]], axis=1)[7:8] \
            + jnp.zeros((8, RWKV_PROJ), F32)
        wa = lo[:, :128]
        lw = jnp.dot(jnp.tanh(wa).astype(BF16), w2_ref[...], preferred_element_type=F32)
        la = jnp.dot(wa.astype(BF16), a2_ref[...], preferred_element_type=F32)
        g = jnp.dot(jax.nn.sigmoid(lo[:, 128:]).astype(BF16), g2_ref[...], preferred_element_type=F32)
        log_w = -jnp.exp(-_softplus(-(w0_ref[...] + lw)) - 0.5)
        a = jax.nn.sigmoid(a0_ref[...] + la)
        kk = k * kk_ref[...]
        kk = kk / jnp.maximum(jnp.sqrt(head_sum(kk * kk)), 1e-12)
        k = k * (1.0 + (a - 1.0) * ka_ref[...])
        cw = jnp.dot(ltri_ref[...], log_w, precision=HIGHEST, preferred_element_type=F32)
        cl = jnp.dot(bones_ref[...], log_w, precision=HIGHEST, preferred_element_type=F32)
        inv_g = jnp.exp(-cw)
        to_end = jnp.exp(cl - cw)
        r_s[b], k_s[b], v_s[b], g_s[b] = r, k, v, g
        ae_s[b] = -kk * jnp.exp(cw - log_w)
        re_s[b] = r * jnp.exp(cw)
        bq_s[b] = kk * a * inv_g
        kq_s[b] = k * inv_g
        bh_s[b] = kk * a * to_end
        kh_s[b] = k * to_end
        gb_s[b] = jnp.exp(cl)

    lo_half = lax.broadcasted_iota(jnp.int32, (sb, 128), 1) < RWKV_HEAD
    lo_state = lax.broadcasted_iota(jnp.int32, (RWKV_HEAD, 128), 1) < RWKV_HEAD
    crow = lax.broadcasted_iota(jnp.int32, (4 * sb, 2 * sb), 0)
    ccol = lax.broadcasted_iota(jnp.int32, (4 * sb, 2 * sb), 1)
    t_idx = jnp.bitwise_and(crow, sb - 1)
    u_idx = jnp.bitwise_and(ccol, sb - 1)
    coef_mask = t_idx > u_idx - jnp.where(crow >= 2 * sb, 1, 0)
    lo_of = lambda x: jnp.where(lo_half, x, 0.0)
    hi_of = lambda x: jnp.where(lo_half, 0.0, x)

    def block(bi, carry):
        t0 = pl.multiple_of(bi * sb, sb)
        for b in range(bb):
            for p in range(npair):
                cs = slice(128 * p, 128 * (p + 1))
                tile = lambda ref: ref[b, pl.ds(t0, sb), cs]
                ae, re, bq, kq, bh, kh, vb, gb = (tile(x) for x in (ae_s, re_s, bq_s, kq_s, bh_s, kh_s, v_s, gb_s))
                S = st_s[b * npair + p]
                lhs = jnp.concatenate([lo_of(ae), hi_of(ae), lo_of(re), hi_of(re)], axis=0).astype(BF16)
                rhs = jnp.concatenate([S, bq, kq], axis=0).astype(BF16)
                m = lax.dot_general(lhs, rhs, _NT, preferred_element_type=F32)
                x = jnp.concatenate([m[0:sb, 0:RWKV_HEAD], m[sb:2 * sb, 0:RWKV_HEAD]], axis=1)
                y = jnp.concatenate([m[2 * sb:3 * sb, 0:RWKV_HEAD], m[3 * sb:, 0:RWKV_HEAD]], axis=1)
                coef = jnp.where(coef_mask, m[:, RWKV_HEAD:], 0.0)
                v_split = jnp.concatenate([lo_of(vb), hi_of(vb)], axis=0).astype(BF16)
                lak = jnp.concatenate([coef[0:sb, sb:], coef[sb:2 * sb, sb:]], axis=1).astype(BF16)
                x = x + jnp.dot(lak, v_split, preferred_element_type=F32)
                lab0 = coef[0:sb, 0:sb]
                lab1 = coef[sb:2 * sb, 0:sb]
                for u in range(sb - 1):
                    x = x + jnp.where(lo_half, lab0[:, u:u + 1], lab1[:, u:u + 1]) * x[u:u + 1, :]
                pv = jnp.concatenate([lo_of(x), lo_of(vb), hi_of(x), hi_of(vb)], axis=0).astype(BF16)
                n01 = jnp.concatenate([coef[2 * sb:3 * sb], coef[3 * sb:]], axis=1).astype(BF16)
                y_s[b, pl.ds(t0, sb), cs] = y + jnp.dot(n01, pv, preferred_element_type=F32)
                upd = lax.dot_general(jnp.concatenate([x, vb], axis=0).astype(BF16),
                                      jnp.concatenate([bh, kh], axis=0).astype(BF16),
                                      (((0,), (0,)), ((), ())), preferred_element_type=F32)
                st_s[b * npair + p] = S * gb[0:1, :] + jnp.where(lo_state, upd[0:RWKV_HEAD], upd[RWKV_HEAD:])
        return carry

    lax.fori_loop(0, tc // sb, block, 0)

    for b in range(bb):
        y = y_s[b]
        mean = head_sum(y) * (1.0 / RWKV_HEAD)
        d = y - mean
        var = head_sum(d * d) * (1.0 / RWKV_HEAD)
        yn = d * lax.rsqrt(var + RWKV_GN_EPS) * lng_ref[...] + lnb_ref[...]
        bonus = head_sum(r_s[b] * k_s[b] * rk_ref[...]) * v_s[b]
        o_ref[b] = ((yn + bonus) * g_s[b]).astype(o_ref.dtype)

    @pl.when(c == pl.num_programs(1) - 1)
    def _():
        sout_ref[...] = st_s[...].reshape(sout_ref.shape)


def _pack_pairs(s):
    B, H, R, C = s.shape
    return s.reshape(B, H // 2, 2, R, C).transpose(0, 1, 3, 2, 4).reshape(B, H // 2, R, 2 * C)


def _unpack_pairs(s):
    B, HP, R, C2 = s.shape
    return s.reshape(B, HP, R, 2, C2 // 2).transpose(0, 1, 3, 2, 4).reshape(B, HP * 2, R, C2 // 2)


def _rwkv_call(st, z, shift0, s0, prm, out_dtype):
    B, T = st.B, st.T
    z3 = z.reshape(B, T, Z_WIDTH)
    bb, tc, sb = _scan_tiling(B, T)
    W = RWKV_WIDTH
    npair = RWKV_HEADS // 2
    s0p = _pack_pairs(s0)
    zspec = lambda w, idx: pl.BlockSpec((bb, tc, w), lambda g, c: (g, c, idx))
    const = lambda a: pl.BlockSpec(a.shape, lambda g, c: (0,) * a.ndim)
    consts = [prm[k] for k in ("mu", "w0", "w2", "a0", "a2", "g2", "k_k", "k_a", "r_k", "lnx_g", "lnx_b")]
    head_of = jnp.arange(W, dtype=jnp.int32) // RWKV_HEAD
    consts += [(head_of[:, None] == head_of[None, :]).astype(BF16), *_block_cumsum_matrices(tc, sb)]
    o, sout = pl.pallas_call(
        functools.partial(_rwkv_body, bb=bb, tc=tc, sb=sb),
        grid=(B // bb, T // tc),
        in_specs=[
            zspec(W, Z_RWKV_R), zspec(W, Z_RWKV_K), zspec(W, Z_RWKV_V), zspec(256, Z_RWKV_LORA),
            pl.BlockSpec((bb, 1, RWKV_PROJ), lambda g, c: (g, 0, 0)),
            pl.BlockSpec((bb, npair, RWKV_HEAD, 128), lambda g, c: (g, 0, 0, 0)),
        ] + [const(a) for a in consts],
        out_specs=[
            pl.BlockSpec((bb, tc, W), lambda g, c: (g, c, 0)),
            pl.BlockSpec((bb, npair, RWKV_HEAD, 128), lambda g, c: (g, 0, 0, 0)),
        ],
        out_shape=[
            jax.ShapeDtypeStruct((B, T, W), out_dtype),
            jax.ShapeDtypeStruct((B, npair, RWKV_HEAD, 128), F32),
        ],
        scratch_shapes=[
            pltpu.VMEM((bb, 8, RWKV_PROJ), F32),
            pltpu.VMEM((bb * npair, RWKV_HEAD, 128), F32),
        ] + [pltpu.VMEM((bb, tc, W), F32)] * 12,
        compiler_params=_cp(("parallel", "arbitrary")),
        name="rwkv7",
    )(z3, z3, z3, z3, shift0.reshape(B, 1, RWKV_PROJ), s0p, *consts)
    return o.reshape(B * T, W), _unpack_pairs(sout)


def _gla_body(zqk_ref, zv_ref, zgr_ref, zgl_ref, s0_ref, wg_ref, bg_ref, ng_ref, ltri_ref, bones_ref,
              o_ref, sout_ref, st_s, q_s, k_s, cw_s, qe_s, ke_s, ab_s, v_s, y_s, *, bb, tc, sb):
    c = pl.program_id(1)
    npair = GLA_HEADS // 2
    HK = GLA_HEADS * GLA_DK

    @pl.when(c == 0)
    def _():
        st_s[...] = s0_ref[...].reshape(st_s.shape)

    for b in range(bb):
        zqk = zqk_ref[b]
        gate = jnp.dot(zgl_ref[b].astype(BF16), wg_ref[...], preferred_element_type=F32) + bg_ref[...]
        log_a = -_softplus(-gate) * (1.0 / GLA_TAU)
        cw = jnp.dot(ltri_ref[...], log_a, precision=HIGHEST, preferred_element_type=F32)
        cl = jnp.dot(bones_ref[...], log_a, precision=HIGHEST, preferred_element_type=F32)
        q = zqk[:, :HK] * (GLA_DK ** -0.5)
        k = zqk[:, HK:]
        q_s[b], k_s[b], cw_s[b], v_s[b] = q, k, cw, zv_ref[b]
        qe_s[b] = q * jnp.exp(cw)
        ke_s[b] = k * jnp.exp(cl - cw)
        ab_s[b] = jnp.exp(cl)

    lo_half = lax.broadcasted_iota(jnp.int32, (sb, 128), 1) < GLA_DK
    t_idx = lax.broadcasted_iota(jnp.int32, (sb, 128), 0)
    lo_of = lambda x: jnp.where(lo_half, x, 0.0)
    hi_of = lambda x: jnp.where(lo_half, 0.0, x)

    def block(bi, carry):
        t0 = pl.multiple_of(bi * sb, sb)
        for b in range(bb):
            for p in range(npair):
                cs = slice(128 * p, 128 * (p + 1))
                v0 = slice(2 * GLA_DV * p, 2 * GLA_DV * p + GLA_DV)
                v1 = slice(2 * GLA_DV * p + GLA_DV, 2 * GLA_DV * (p + 1))
                qb, kb, cwb, qeb, keb, abb = (x[b, pl.ds(t0, sb), cs] for x in (q_s, k_s, cw_s, qe_s, ke_s, ab_s))
                v0b = v_s[b, pl.ds(t0, sb), v0]
                v1b = v_s[b, pl.ds(t0, sb), v1]
                S = st_s[b * npair + p]
                inter = lax.dot_general(jnp.concatenate([lo_of(qeb), hi_of(qeb)], axis=0).astype(BF16),
                                        S.astype(BF16), _NT, preferred_element_type=F32)
                o0 = inter[0:sb]
                o1 = inter[sb:]
                for u in range(sb):
                    decay = jnp.exp(jnp.minimum(cwb - cwb[u:u + 1, :], 0.0))
                    w_tu = jnp.where(t_idx >= u, qb * decay * kb[u:u + 1, :], 0.0)
                    o0 = o0 + jnp.sum(lo_of(w_tu), -1, keepdims=True) * v0b[u:u + 1, :]
                    o1 = o1 + jnp.sum(hi_of(w_tu), -1, keepdims=True) * v1b[u:u + 1, :]
                y_s[b, pl.ds(t0, sb), v0] = o0
                y_s[b, pl.ds(t0, sb), v1] = o1
                upd = lax.dot_general(jnp.concatenate([v0b, v1b], axis=0).astype(BF16),
                                      jnp.concatenate([lo_of(keb), hi_of(keb)], axis=0).astype(BF16),
                                      (((0,), (0,)), ((), ())), preferred_element_type=F32)
                st_s[b * npair + p] = S * abb[0:1, :] + upd
        return carry

    lax.fori_loop(0, tc // sb, block, 0)

    for b in range(bb):
        gr = zgr_ref[b]
        for h in range(GLA_HEADS):
            hs = slice(h * GLA_DV, (h + 1) * GLA_DV)
            o = y_s[b, :, hs]
            on = o * lax.rsqrt(jnp.mean(o * o, -1, keepdims=True) + RMS_EPS) * ng_ref[...]
            o_ref[b, :, hs] = (on * _silu(gr[:, hs])).astype(o_ref.dtype)

    @pl.when(c == pl.num_programs(1) - 1)
    def _():
        sout_ref[...] = st_s[...].reshape(sout_ref.shape)


def _gla_call(st, z, s0, wg_p, bg, ng, out_dtype):
    B, T = st.B, st.T
    z3 = z.reshape(B, T, Z_WIDTH)
    bb, tc, sb = _scan_tiling(B, T)
    npair = GLA_HEADS // 2
    HK = GLA_HEADS * GLA_DK
    s0p = _pack_pairs(s0.transpose(0, 1, 3, 2))
    zspec = lambda w, idx: pl.BlockSpec((bb, tc, w), lambda g, c: (g, c, idx))
    const = lambda a: pl.BlockSpec(a.shape, lambda g, c: (0,) * a.ndim)
    ltri, bones = _block_cumsum_matrices(tc, sb)
    o, sout = pl.pallas_call(
        functools.partial(_gla_body, bb=bb, tc=tc, sb=sb),
        grid=(B // bb, T // tc),
        in_specs=[
            zspec(512, Z_GLA_QK), zspec(512, Z_GLA_V), zspec(512, Z_GLA_GR), zspec(128, Z_GLA_GL),
            pl.BlockSpec((bb, npair, GLA_DV, 128), lambda g, c: (g, 0, 0, 0)),
            const(wg_p), const(bg), const(ng), const(ltri), const(bones),
        ],
        out_specs=[
            pl.BlockSpec((bb, tc, GLA_WIDTH), lambda g, c: (g, c, 0)),
            pl.BlockSpec((bb, npair, GLA_DV, 128), lambda g, c: (g, 0, 0, 0)),
        ],
        out_shape=[
            jax.ShapeDtypeStruct((B, T, GLA_WIDTH), out_dtype),
            jax.ShapeDtypeStruct((B, npair, GLA_DV, 128), F32),
        ],
        scratch_shapes=[
            pltpu.VMEM((bb * npair, GLA_DV, 128), F32),
        ] + [pltpu.VMEM((bb, tc, HK), F32)] * 6 + [pltpu.VMEM((bb, tc, GLA_WIDTH), F32)] * 2,
        compiler_params=_cp(("parallel", "arbitrary")),
        name="gla",
    )(z3, z3, z3, z3, s0p, wg_p, bg, ng, ltri, bones)
    return o.reshape(B * T, GLA_WIDTH), _unpack_pairs(sout).transpose(0, 1, 3, 2)


def _outproj_body(om_ref, or_ref, og_ref, x_ref, g1_ref, sh2_ref, sc2_ref, w_ref, lng_ref, lnb_ref,
                  x1_ref, h2_ref, *, alpha):
    wm = MLA_HEADS * MLA_V
    mix = (jnp.dot(om_ref[...].astype(BF16), w_ref[0:wm, :], preferred_element_type=F32)
           + jnp.dot(or_ref[...].astype(BF16), w_ref[wm:wm + RWKV_WIDTH, :], preferred_element_type=F32)
           + jnp.dot(og_ref[...].astype(BF16), w_ref[wm + RWKV_WIDTH:, :], preferred_element_type=F32))
    x1 = _layernorm(alpha * x_ref[...] + (1.0 + g1_ref[...]) * mix, lng_ref[...], lnb_ref[...])
    x1_ref[...] = x1
    h2_ref[...] = x1 * (1.0 + sc2_ref[...]) + sh2_ref[...]


def _outproj_call(st, o_mla, o_rwkv, o_gla, x2d, l, w_out_b, ln_g, ln_b, alpha):
    N, D = x2d.shape
    bm = st.row_block(256)
    row = lambda w: pl.BlockSpec((bm, w), lambda i: (i, 0))
    return pl.pallas_call(
        functools.partial(_outproj_body, alpha=alpha),
        grid=(N // bm,),
        in_specs=[
            row(o_mla.shape[1]), row(o_rwkv.shape[1]), row(o_gla.shape[1]), row(D),
            st.mod_spec(l, 2, bm, D), st.mod_spec(l, 3, bm, D), st.mod_spec(l, 4, bm, D),
            pl.BlockSpec((None,) + w_out_b.shape[1:], lambda i: (l, 0, 0)),
            pl.BlockSpec((None, 1, D), lambda i: (l, 0, 0)),
            pl.BlockSpec((None, 1, D), lambda i: (l, 0, 0)),
        ],
        out_specs=[row(D), row(D)],
        out_shape=[jax.ShapeDtypeStruct((N, D), F32), jax.ShapeDtypeStruct((N, D), F32)],
        compiler_params=_cp(("parallel",)),
        name="out_proj",
    )(o_mla, o_rwkv, o_gla, x2d, st.mod, st.mod, st.mod, w_out_b, ln_g, ln_b)


def _router_body(x1_ref, sh2_ref, sc2_ref, wr_ref, o_ref):
    h = x1_ref[...] * (1.0 + sc2_ref[...]) + sh2_ref[...]
    logits = jnp.dot(h, wr_ref[...], precision=HIGHEST, preferred_element_type=F32)
    lane = lax.broadcasted_iota(jnp.int32, logits.shape, 1)
    lane_f = lane.astype(F32)
    lg = jnp.where(lane < N_EXPERTS, logits, -jnp.inf)
    v1 = jnp.max(lg, -1, keepdims=True)
    i1 = jnp.min(jnp.where(lg == v1, lane_f, 128.0), -1, keepdims=True)
    lg2 = jnp.where(lane_f == i1, -jnp.inf, lg)
    v2 = jnp.max(lg2, -1, keepdims=True)
    i2 = jnp.min(jnp.where(lg2 == v2, lane_f, 128.0), -1, keepdims=True)
    e = jnp.exp(v2 - v1)
    g1 = 1.0 / (1.0 + e)
    g2 = e * g1
    o_ref[...] = jnp.where(lane == 0, g1, jnp.where(lane == 1, g2, jnp.where(
        lane == 2, i1, jnp.where(lane == 3, i2, 0.0))))


def _router_call(st, x1, l, wr_p):
    N, D = x1.shape
    bm = st.row_block(512)
    return pl.pallas_call(
        _router_body,
        grid=(N // bm,),
        in_specs=[
            pl.BlockSpec((bm, D), lambda i: (i, 0)),
            st.mod_spec(l, 3, bm, D), st.mod_spec(l, 4, bm, D),
            pl.BlockSpec(wr_p.shape, lambda i: (0, 0)),
        ],
        out_specs=pl.BlockSpec((bm, 128), lambda i: (i, 0)),
        out_shape=jax.ShapeDtypeStruct((N, 128), F32),
        compiler_params=_cp(("parallel",)),
        name="moe_router",
    )(x1, st.mod, st.mod, wr_p)


def _ffn_up_body(te_ref, tv_ref, h_ref, w1_ref, w3_ref, o_ref):
    t = pl.program_id(0)

    @pl.when(tv_ref[t] != 0)
    def _():
        h = h_ref[...].astype(BF16)
        a = jnp.dot(h, w1_ref[...].astype(BF16), preferred_element_type=F32)
        b = jnp.dot(h, w3_ref[...].astype(BF16), preferred_element_type=F32)
        o_ref[...] = (_silu(a) * b).astype(BF16)

    @pl.when(tv_ref[t] == 0)
    def _():
        o_ref[...] = jnp.zeros(o_ref.shape, BF16)


def _ffn_down_body(te_ref, tv_ref, g_ref, w2_ref, o_ref):
    t = pl.program_id(0)

    @pl.when(tv_ref[t] != 0)
    def _():
        o_ref[...] = jnp.dot(g_ref[...], w2_ref[...].astype(BF16), preferred_element_type=F32)

    @pl.when(tv_ref[t] == 0)
    def _():
        o_ref[...] = jnp.zeros(o_ref.shape, F32)


def _ffn_call(h_rows, tile_expert, tile_valid, w1, w3, w2, bm):
    R, D = h_rows.shape
    F = w1.shape[-1]
    nt = R // bm
    bf = 512
    up = pl.pallas_call(
        _ffn_up_body,
        grid_spec=pltpu.PrefetchScalarGridSpec(
            num_scalar_prefetch=2,
            grid=(nt, F // bf),
            in_specs=[
                pl.BlockSpec((bm, D), lambda t, j, te, tv: (t, 0)),
                pl.BlockSpec((None, D, bf), lambda t, j, te, tv: (te[t], 0, j)),
                pl.BlockSpec((None, D, bf), lambda t, j, te, tv: (te[t], 0, j)),
            ],
            out_specs=pl.BlockSpec((bm, bf), lambda t, j, te, tv: (t, j)),
        ),
        out_shape=jax.ShapeDtypeStruct((R, F), BF16),
        compiler_params=_cp(("parallel", "arbitrary")),
        name="ffn_up",
    )(tile_expert, tile_valid, h_rows, w1, w3)
    bn = 256
    return pl.pallas_call(
        _ffn_down_body,
        grid_spec=pltpu.PrefetchScalarGridSpec(
            num_scalar_prefetch=2,
            grid=(nt, D // bn),
            in_specs=[
                pl.BlockSpec((bm, F), lambda t, n, te, tv: (t, 0)),
                pl.BlockSpec((None, F, bn), lambda t, n, te, tv: (te[t], 0, n)),
            ],
            out_specs=pl.BlockSpec((bm, bn), lambda t, n, te, tv: (t, n)),
        ),
        out_shape=jax.ShapeDtypeStruct((R, D), F32),
        compiler_params=_cp(("parallel", "arbitrary")),
        name="ffn_down",
    )(tile_expert, tile_valid, up, w2)


def _combine_body(*refs, alpha, nterm):
    x1_ref, g2_ref, lng_ref, lnb_ref = refs[:4]
    f_refs = refs[4:4 + nterm]
    o_ref = refs[-1]
    if nterm == 1:
        f = f_refs[0][...]
    else:
        gates = refs[4 + nterm][...]
        f = f_refs[0][...] * gates[:, 0:1] + f_refs[1][...] * gates[:, 1:2]
    o_ref[...] = _layernorm(alpha * x1_ref[...] + (1.0 + g2_ref[...]) * f, lng_ref[...], lnb_ref[...])


def _combine_call(st, x1, l, ln_g, ln_b, alpha, terms, gates=None):
    N, D = x1.shape
    bm = st.row_block(512)
    row = lambda w: pl.BlockSpec((bm, w), lambda i: (i, 0))
    args = [x1, st.mod, ln_g, ln_b] + list(terms) + ([gates] if gates is not None else [])
    return pl.pallas_call(
        functools.partial(_combine_body, alpha=alpha, nterm=len(terms)),
        grid=(N // bm,),
        in_specs=[
            row(D), st.mod_spec(l, 5, bm, D),
            pl.BlockSpec((None, 1, D), lambda i: (l, 0, 0)),
            pl.BlockSpec((None, 1, D), lambda i: (l, 0, 0)),
        ] + [row(D)] * len(terms) + ([row(128)] if gates is not None else []),
        out_specs=row(D),
        out_shape=jax.ShapeDtypeStruct((N, D), F32),
        compiler_params=_cp(("parallel",)),
        name="ffn_residual",
    )(*args)


def _repack_w_in(w_in):
    L, D, _ = w_in.shape
    w = w_in.astype(BF16)
    zeros = lambda n: jnp.zeros((L, D, n), BF16)
    r0 = MLA_IN
    g0 = MLA_IN + RWKV_PROJ
    HK = GLA_HEADS * GLA_DK
    gq, gk, gv = g0, g0 + HK, g0 + 2 * HK
    ggl = gv + GLA_WIDTH
    ggr = ggl + GLA_GATE_LORA
    parts = [
        w[:, :, :MLA_IN], zeros(Z_MLA_W - MLA_IN),
        w[:, :, r0:r0 + RWKV_PROJ],
        w[:, :, ggl:ggr], zeros(256 - GLA_GATE_LORA),
        w[:, :, gq:gv], w[:, :, gv:ggl], w[:, :, ggr:ggr + GLA_WIDTH],
    ]
    out = jnp.concatenate(parts, axis=-1)
    assert out.shape[-1] == Z_WIDTH
    return out


def _moe_plan(gi, bm):
    n = gi.shape[0]
    E = N_EXPERTS
    experts = gi[:, 2:4].astype(jnp.int32).reshape(-1)
    onehot = (experts[:, None] == jnp.arange(E, dtype=jnp.int32)[None, :]).astype(jnp.int32)
    counts = jnp.sum(onehot, axis=0)
    tiles = (counts + bm - 1) // bm
    tile_end = jnp.cumsum(tiles)
    tile_start = tile_end - tiles
    rank = jnp.take_along_axis(jnp.cumsum(onehot, axis=0) - 1, experts[:, None], axis=1)[:, 0]
    pos = tile_start[experts] * bm + rank
    nt = (2 * n + bm - 1) // bm + E
    src = jnp.zeros((nt * bm,), jnp.int32).at[pos].set(jnp.arange(2 * n, dtype=jnp.int32) // 2)
    t_ids = jnp.arange(nt, dtype=jnp.int32)
    tile_expert = jnp.minimum(jnp.sum((t_ids[:, None] >= tile_end[None, :]).astype(jnp.int32), axis=1), E - 1)
    tile_valid = (t_ids < tile_end[-1]).astype(jnp.int32)
    return src, pos.reshape(n, 2), tile_expert, tile_valid


def kernel(x_prompt, x_sample, cache_kv_latent, cache_k_rope, state_rwkv, state_rwkv_shift, state_gla, page_table, c_prompt, c_sample, w_in, w_out, mla_q_norm, mla_kv_norm, mla_w_q_up, mla_w_uk, mla_w_uv, rwkv_mu, rwkv_w0, rwkv_w2, rwkv_a0, rwkv_a2, rwkv_g2, rwkv_k_k, rwkv_k_a, rwkv_r_k, rwkv_lnx_g, rwkv_lnx_b, gla_w_g2, gla_b_g, gla_norm_g, ada_w, ada_b, ln1_g, ln1_b, ln2_g, ln2_b, ffn_w1, ffn_w3, ffn_w2, moe_router, moe_w1, moe_w3, moe_w2):
    Bp, Tp, D = x_prompt.shape
    Bs, Ts, _ = x_sample.shape
    L = w_in.shape[0]
    past_len = page_table.shape[1] * PAGE_SIZE
    alpha = (2 * L) ** 0.25
    dt = x_prompt.dtype

    n_c = Bp + Bs
    c_all = jnp.concatenate([c_prompt, c_sample, jnp.zeros((-n_c % 8, D), F32)], axis=0)
    mod = _ada_call(c_all, ada_w, ada_b)
    st_p = _Stream(Bp, Tp, mod[:, :Bp].reshape(L, Bp, 1, 6 * D), False, 0)
    st_s = _Stream(Bs, Ts, jnp.repeat(mod[:, Bp:Bp + Bs], Ts, axis=1), True, past_len)

    w_in_p = _repack_w_in(w_in)
    w_out_b = w_out.astype(BF16)
    wq = mla_w_q_up.astype(BF16)
    wq_p = jnp.concatenate([wq[..., :MLA_NOPE].reshape(L, MLA_Q_RANK, -1),
                            wq[..., MLA_NOPE:].reshape(L, MLA_Q_RANK, -1)], axis=-1)
    wuk_p = mla_w_uk.astype(BF16).transpose(0, 2, 3, 1)
    wuv_p = mla_w_uv.astype(BF16).transpose(0, 2, 1, 3)
    inv = 1.0 / (ROPE_BASE ** (jnp.arange(0, MLA_ROPE, 2, dtype=F32) / MLA_ROPE))
    inv128 = jnp.tile(inv, 4).reshape(1, 128)
    zpad = lambda a, n: jnp.concatenate([a, jnp.zeros((L, n) + a.shape[2:], a.dtype)], axis=1)
    zpre = lambda a, n: jnp.concatenate([jnp.zeros((L, n) + a.shape[2:], a.dtype), a], axis=1)
    rw_w2 = zpad(rwkv_w2.astype(BF16), RWKV_A_LORA)
    rw_a2 = zpre(rwkv_a2.astype(BF16), RWKV_W_LORA)
    gla_wg = zpad(gla_w_g2.astype(BF16), 128 - GLA_GATE_LORA)
    row1 = lambda a: a.reshape(1, -1)

    xp = x_prompt.reshape(Bp * Tp, D)
    xs = x_sample.reshape(Bs * Ts, D)
    zeros_shift = jnp.zeros((Bp, RWKV_PROJ), dt)
    zeros_rwkv = jnp.zeros((Bp, RWKV_HEADS, RWKV_HEAD, RWKV_HEAD), F32)
    zeros_gla = jnp.zeros((Bp, GLA_HEADS, GLA_DK, GLA_DV), F32)
    outs_p = [[] for _ in range(5)]
    outs_s = [[] for _ in range(5)]

    for l in range(L):
        rw = dict(mu=row1(rwkv_mu[l]), w0=row1(rwkv_w0[l]), w2=rw_w2[l], a0=row1(rwkv_a0[l]), a2=rw_a2[l],
                  g2=rwkv_g2[l].astype(BF16), k_k=row1(rwkv_k_k[l]), k_a=row1(rwkv_k_a[l]),
                  r_k=row1(rwkv_r_k[l]), lnx_g=row1(rwkv_lnx_g[l]), lnx_b=row1(rwkv_lnx_b[l]))
        streams = []
        for st, x2d, sample in ((st_p, xp, False), (st_s, xs, True)):
            z = _inproj_call(st, x2d, l, w_in_p)
            qlat, qpe, lat, kpe = _mla_prep_call(st, z, row1(mla_q_norm[l]), row1(mla_kv_norm[l]),
                                                 wq_p[l], wuk_p[l], inv128, F32 if sample else BF16)
            if sample:
                o_mla = _attn_sample_call(st, l, qlat, qpe, lat, kpe, cache_kv_latent, cache_k_rope,
                                          page_table, wuv_p[l])
                shift0, s_r0, s_g0 = state_rwkv_shift[l], state_rwkv[l], state_gla[l]
            else:
                o_mla = _attn_prompt_call(st, qlat, qpe, lat, kpe, wuv_p[l])
                shift0, s_r0, s_g0 = zeros_shift, zeros_rwkv, zeros_gla
            o_dt = F32 if sample else BF16
            o_rwkv, s_r = _rwkv_call(st, z, shift0, s_r0, rw, o_dt)
            o_gla, s_g = _gla_call(st, z, s_g0, gla_wg[l], row1(gla_b_g[l]), row1(gla_norm_g[l]), o_dt)
            x1, h2 = _outproj_call(st, o_mla, o_rwkv, o_gla, x2d, l, w_out_b, ln1_g.reshape(L, 1, D),
                                   ln1_b.reshape(L, 1, D), alpha)
            shift = z.reshape(st.B, st.T, Z_WIDTH)[:, -1, Z_MLA_W:Z_MLA_W + RWKV_PROJ]
            acc = outs_s if sample else outs_p
            for lst, val in zip(acc, (lat.reshape(st.B, st.T, -1), kpe.reshape(st.B, st.T, -1),
                                      s_r.astype(dt), shift, s_g.astype(dt))):
                lst.append(val)
            streams.append((st, x1, h2))

        (_, x1p, h2p), (_, x1s, h2s) = streams
        h_all = jnp.concatenate([h2p, h2s], axis=0)
        n_all = h_all.shape[0]
        n_p = h2p.shape[0]
        bm = _pick(n_all, 1024)
        lng, lnb = ln2_g.reshape(L, 1, D), ln2_b.reshape(L, 1, D)
        if l % 2 == 0:
            e = l // 2
            nt = n_all // bm
            f = _ffn_call(h_all, jnp.zeros((nt,), jnp.int32), jnp.ones((nt,), jnp.int32),
                          ffn_w1[e:e + 1], ffn_w3[e:e + 1], ffn_w2[e:e + 1], bm)
            xp = _combine_call(st_p, x1p, l, lng, lnb, alpha, [f[:n_p]])
            xs = _combine_call(st_s, x1s, l, lng, lnb, alpha, [f[n_p:]])
        else:
            e = l // 2
            wr_p = jnp.concatenate([moe_router[e], jnp.zeros((D, 128 - N_EXPERTS), F32)], axis=1)
            gi = jnp.concatenate([_router_call(st_p, x1p, l, wr_p), _router_call(st_s, x1s, l, wr_p)], axis=0)
            src, pos, tile_expert, tile_valid = _moe_plan(gi, bm)
            f_sorted = _ffn_call(h_all[src], tile_expert, tile_valid, moe_w1[e], moe_w3[e], moe_w2[e], bm)
            f0 = f_sorted[pos[:, 0]]
            f1 = f_sorted[pos[:, 1]]
            xp = _combine_call(st_p, x1p, l, lng, lnb, alpha, [f0[:n_p], f1[:n_p]], gi[:n_p])
            xs = _combine_call(st_s, x1s, l, lng, lnb, alpha, [f0[n_p:], f1[n_p:]], gi[n_p:])

    stack = lambda lst: jnp.stack(lst)
    return (xp.reshape(Bp, Tp, D), xs.reshape(Bs, Ts, D),
            *[stack(v) for v in outs_p], *[stack(v) for v in outs_s])
```

```python
import functools

import jax
import jax.numpy as jnp
from jax import lax
from jax.experimental import pallas as pl
from jax.experimental.pallas import tpu as pltpu

F32 = jnp.float32
BF16 = jnp.bfloat16
HIGHEST = lax.Precision.HIGHEST

PAGE_SIZE = 128
MLA_HEADS = 8
MLA_NOPE = 128
MLA_ROPE = 64
MLA_V = 128
MLA_Q_RANK = 512
MLA_KV_RANK = 256
MLA_IN = MLA_Q_RANK + MLA_KV_RANK + MLA_ROPE
MLA_SCALE = (MLA_NOPE + MLA_ROPE) ** -0.5
ROPE_BASE = 10000.0
RWKV_HEADS = 8
RWKV_HEAD = 64
RWKV_WIDTH = RWKV_HEADS * RWKV_HEAD
RWKV_W_LORA = 64
RWKV_A_LORA = 64
RWKV_G_LORA = 128
RWKV_PROJ = 3 * RWKV_WIDTH + RWKV_W_LORA + RWKV_A_LORA + RWKV_G_LORA
RWKV_GN_EPS = 64e-5
GLA_HEADS = 4
GLA_DK = 64
GLA_DV = 128
GLA_WIDTH = GLA_HEADS * GLA_DV
GLA_GATE_LORA = 16
GLA_TAU = 16.0
GLA_PROJ = 2 * GLA_HEADS * GLA_DK + 2 * GLA_WIDTH + GLA_GATE_LORA
N_EXPERTS = 8
LN_EPS = 1e-5
RMS_EPS = 1e-6

Z_WIDTH = 4608
Z_MLA_W = 1024
Z_RWKV_R, Z_RWKV_K, Z_RWKV_V = 2, 3, 4
Z_RWKV_LORA = 10
Z_GLA_GL = 22
Z_GLA_QK, Z_GLA_V, Z_GLA_GR = 6, 7, 8

SAMPLE_PAGES_PER_STEP = 64
SCAN_BLOCK = 16
SCAN_BATCH = 4

VMEM_LIMIT_MB = 56


def _cp(sem, vmem_mb=VMEM_LIMIT_MB):
    return pltpu.CompilerParams(dimension_semantics=sem, vmem_limit_bytes=vmem_mb * 2**20)


def _pick(n, pref):
    if n <= pref:
        return n
    b = pref - pref % 8
    while b >= 8:
        if n % b == 0:
            return b
        b -= 8
    return n


def _silu(x):
    return x * jax.nn.sigmoid(x)


def _softplus(u):
    return jnp.maximum(u, 0.0) + jnp.log(1.0 + jnp.exp(-jnp.abs(u)))


def _lane_group_sum(x, group):
    axis = x.ndim - 1
    width = x.shape[axis]
    lane = lax.broadcasted_iota(jnp.int32, x.shape, axis)
    s = 1
    while s < group:
        partner = jnp.where(jnp.bitwise_and(lane, s) == 0,
                            pltpu.roll(x, width - s, axis), pltpu.roll(x, s, axis))
        x = x + partner
        s *= 2
    return x


def _layernorm(y, g, b):
    mu = jnp.mean(y, -1, keepdims=True)
    d = y - mu
    var = jnp.mean(d * d, -1, keepdims=True)
    return d * lax.rsqrt(var + LN_EPS) * g + b


def _ada_body(c_ref, w_ref, b_ref, o_ref):
    c = c_ref[...]
    s = _silu(c).astype(BF16)
    o_ref[...] = jnp.dot(s, w_ref[...].astype(BF16), preferred_element_type=F32) + b_ref[...]


def _ada_call(c_all, ada_w, ada_b):
    L, D, N6 = ada_w.shape
    Mc = c_all.shape[0]
    bn = 1024
    return pl.pallas_call(
        _ada_body,
        grid=(L, N6 // bn),
        in_specs=[
            pl.BlockSpec((Mc, D), lambda l, j: (0, 0)),
            pl.BlockSpec((None, D, bn), lambda l, j: (l, 0, j)),
            pl.BlockSpec((None, 1, bn), lambda l, j: (l, 0, j)),
        ],
        out_specs=pl.BlockSpec((None, Mc, bn), lambda l, j: (l, 0, j)),
        out_shape=jax.ShapeDtypeStruct((L, Mc, N6), F32),
        compiler_params=_cp(("parallel", "parallel")),
        name="ada_mod",
    )(c_all, ada_w, ada_b.reshape(L, 1, N6))


class _Stream:
    def __init__(self, B, T, mod, per_row, pos_base):
        self.B, self.T, self.N = B, T, B * T
        self.mod = mod
        self.per_row = per_row
        self.pos_base = pos_base

    def mod_spec(self, l, k, bm, D):
        if self.per_row:
            return pl.BlockSpec((None, bm, D), lambda i, *_: (l, i, k))
        nb = self.T // bm
        return pl.BlockSpec((None, None, 1, D), lambda i, *_: (l, i // nb, 0, k))

    def row_block(self, pref):
        return _pick(self.N if self.per_row else self.T, pref)


def _inproj_body(x_ref, sh_ref, sc_ref, w_ref, o_ref, h_s):
    @pl.when(pl.program_id(1) == 0)
    def _():
        h_s[...] = (x_ref[...] * (1.0 + sc_ref[...]) + sh_ref[...]).astype(BF16)

    o_ref[...] = jnp.dot(h_s[...], w_ref[...], preferred_element_type=F32)


def _inproj_call(st, x2d, l, w_in_p):
    N, D = x2d.shape
    Wz = w_in_p.shape[-1]
    bm = st.row_block(1024)
    bn = 512
    return pl.pallas_call(
        _inproj_body,
        grid=(N // bm, Wz // bn),
        in_specs=[
            pl.BlockSpec((bm, D), lambda i, j: (i, 0)),
            st.mod_spec(l, 0, bm, D),
            st.mod_spec(l, 1, bm, D),
            pl.BlockSpec((None, D, bn), lambda i, j: (l, 0, j)),
        ],
        out_specs=pl.BlockSpec((bm, bn), lambda i, j: (i, j)),
        out_shape=jax.ShapeDtypeStruct((N, Wz), F32),
        scratch_shapes=[pltpu.VMEM((bm, D), BF16)],
        compiler_params=_cp(("parallel", "arbitrary")),
        name="in_proj",
    )(x2d, st.mod, st.mod, w_in_p)


def _rope(x, cos, sin_signed, first_half):
    w = x.shape[1]
    swapped = jnp.where(first_half, pltpu.roll(x, w - 32, 1), pltpu.roll(x, 32, 1))
    return x * cos + swapped * sin_signed


def _mla_prep_body(z_ref, gq_ref, gkv_ref, wq_ref, wuk_ref, inv_ref, qlat_ref, qpe_ref, lat_ref, kpe_ref,
                   *, bm, period, pos_base):
    i = pl.program_id(0)
    z = z_ref[...]
    zq = z[:, :MLA_Q_RANK]
    zkv = z[:, MLA_Q_RANK:MLA_Q_RANK + MLA_KV_RANK]
    zpe = z[:, MLA_Q_RANK + MLA_KV_RANK:MLA_Q_RANK + MLA_KV_RANK + 128]
    qn = zq * lax.rsqrt(jnp.mean(zq * zq, -1, keepdims=True) + RMS_EPS) * gq_ref[...]
    q = jnp.dot(qn.astype(BF16), wq_ref[...], preferred_element_type=F32)
    lat_ref[...] = zkv * lax.rsqrt(jnp.mean(zkv * zkv, -1, keepdims=True) + RMS_EPS) * gkv_ref[...]

    row = lax.broadcasted_iota(jnp.int32, (bm, 128), 0) + i * bm
    pos = (pos_base + jnp.bitwise_and(row, period - 1)).astype(F32)
    ang = pos * inv_ref[...]
    cos = jnp.cos(ang)
    sin = jnp.sin(ang)
    first = jnp.bitwise_and(lax.broadcasted_iota(jnp.int32, (bm, 128), 1), 63) < 32
    sin_s = jnp.where(first, -sin, sin)
    kpe_ref[...] = _rope(zpe, cos, sin_s, first)[:, :MLA_ROPE]

    npe = MLA_HEADS * MLA_ROPE // 128
    cos4 = jnp.concatenate([cos] * npe, axis=1)
    sin4 = jnp.concatenate([sin_s] * npe, axis=1)
    first4 = jnp.bitwise_and(lax.broadcasted_iota(jnp.int32, (bm, 128 * npe), 1), 63) < 32
    q_pe = q[:, MLA_HEADS * MLA_NOPE:]
    qpe_ref[...] = (_rope(q_pe, cos4, sin4, first4) * MLA_SCALE).astype(qpe_ref.dtype)
    for h in range(MLA_HEADS):
        qh = q[:, h * MLA_NOPE:(h + 1) * MLA_NOPE].astype(BF16)
        ql = jnp.dot(qh, wuk_ref[h], preferred_element_type=F32) * MLA_SCALE
        qlat_ref[:, h * MLA_KV_RANK:(h + 1) * MLA_KV_RANK] = ql.astype(qlat_ref.dtype)


def _mla_prep_call(st, z, gq, gkv, wq_p, wuk_p, inv128, q_dtype):
    N = z.shape[0]
    bm = _pick(N, 512)
    HR = MLA_HEADS * MLA_KV_RANK
    HP = MLA_HEADS * MLA_ROPE
    assert st.T & (st.T - 1) == 0
    body = functools.partial(_mla_prep_body, bm=bm, period=st.T, pos_base=st.pos_base)
    const = lambda shape: pl.BlockSpec(shape, lambda i: (0,) * len(shape))
    return pl.pallas_call(
        body,
        grid=(N // bm,),
        in_specs=[
            pl.BlockSpec((bm, Z_MLA_W), lambda i: (i, 0)),
            const((1, MLA_Q_RANK)), const((1, MLA_KV_RANK)),
            const(wq_p.shape), const(wuk_p.shape), const((1, 128)),
        ],
        out_specs=[
            pl.BlockSpec((bm, HR), lambda i: (i, 0)),
            pl.BlockSpec((bm, HP), lambda i: (i, 0)),
            pl.BlockSpec((bm, MLA_KV_RANK), lambda i: (i, 0)),
            pl.BlockSpec((bm, MLA_ROPE), lambda i: (i, 0)),
        ],
        out_shape=[
            jax.ShapeDtypeStruct((N, HR), q_dtype),
            jax.ShapeDtypeStruct((N, HP), q_dtype),
            jax.ShapeDtypeStruct((N, MLA_KV_RANK), F32),
            jax.ShapeDtypeStruct((N, MLA_ROPE), F32),
        ],
        compiler_params=_cp(("parallel",)),
        name="mla_prep",
    )(z, gq, gkv, wq_p, wuk_p, inv128)


_NT = (((1,), (1,)), ((), ()))


def _softmax_update(s, h, m_s, l_s, acc_s, values_bf16):
    m_prev = m_s[h]
    m_new = jnp.maximum(m_prev, jnp.max(s, -1, keepdims=True))
    alpha = jnp.exp(m_prev - m_new)
    p = jnp.exp(s - m_new[:, :1])
    l_s[h] = alpha * l_s[h] + jnp.sum(p, -1, keepdims=True)
    acc_s[h] = acc_s[h] * alpha[:, :1] + jnp.dot(p.astype(BF16), values_bf16, preferred_element_type=F32)
    m_s[h] = m_new


def _attn_prompt_body(ql_ref, qp_ref, lat_ref, kpe_ref, wuv_ref, o_ref, m_s, l_s, acc_s, *, bq):
    qi = pl.program_id(1)
    ki = pl.program_id(2)

    @pl.when(ki == 0)
    def _():
        m_s[...] = jnp.full(m_s.shape, -jnp.inf, F32)
        l_s[...] = jnp.zeros(l_s.shape, F32)
        acc_s[...] = jnp.zeros(acc_s.shape, F32)

    def step(masked):
        latb = lat_ref[...].astype(BF16)
        kpb = kpe_ref[...].astype(BF16)
        if masked:
            causal = (lax.broadcasted_iota(jnp.int32, (bq, bq), 0)
                      >= lax.broadcasted_iota(jnp.int32, (bq, bq), 1))
        for h in range(MLA_HEADS):
            s = (lax.dot_general(ql_ref[:, h * MLA_KV_RANK:(h + 1) * MLA_KV_RANK], latb, _NT,
                                 preferred_element_type=F32)
                 + lax.dot_general(qp_ref[:, h * MLA_ROPE:(h + 1) * MLA_ROPE], kpb, _NT,
                                   preferred_element_type=F32))
            if masked:
                s = jnp.where(causal, s, -jnp.inf)
            _softmax_update(s, h, m_s, l_s, acc_s, latb)

    @pl.when(ki < qi)
    def _():
        step(False)

    @pl.when(ki == qi)
    def _():
        step(True)
        for h in range(MLA_HEADS):
            o_lat = (acc_s[h] / l_s[h][:, :1]).astype(BF16)
            o_ref[:, h * MLA_V:(h + 1) * MLA_V] = jnp.dot(
                o_lat, wuv_ref[h], preferred_element_type=F32).astype(o_ref.dtype)


def _attn_prompt_call(st, qlat, qpe, lat, kpe, wuv_p):
    B, T = st.B, st.T
    bq = _pick(T, 512)
    nq = T // bq
    HR = MLA_HEADS * MLA_KV_RANK
    HP = MLA_HEADS * MLA_ROPE
    kv = lambda b, qi, ki: (b * nq + jnp.minimum(ki, qi), 0)
    return pl.pallas_call(
        functools.partial(_attn_prompt_body, bq=bq),
        grid=(B, nq, nq),
        in_specs=[
            pl.BlockSpec((bq, HR), lambda b, qi, ki: (b * nq + qi, 0)),
            pl.BlockSpec((bq, HP), lambda b, qi, ki: (b * nq + qi, 0)),
            pl.BlockSpec((bq, MLA_KV_RANK), kv),
            pl.BlockSpec((bq, MLA_ROPE), kv),
            pl.BlockSpec(wuv_p.shape, lambda b, qi, ki: (0, 0, 0)),
        ],
        out_specs=pl.BlockSpec((bq, MLA_HEADS * MLA_V), lambda b, qi, ki: (b * nq + qi, 0)),
        out_shape=jax.ShapeDtypeStruct((B * T, MLA_HEADS * MLA_V), BF16),
        scratch_shapes=[
            pltpu.VMEM((MLA_HEADS, bq, 128), F32),
            pltpu.VMEM((MLA_HEADS, bq, 128), F32),
            pltpu.VMEM((MLA_HEADS, bq, MLA_KV_RANK), F32),
        ],
        compiler_params=_cp(("parallel", "parallel", "arbitrary")),
        name="attn_prompt",
    )(qlat, qpe, lat, kpe, wuv_p)


def _attn_sample_body(pt_ref, ql_ref, qp_ref, latn_ref, kpen_ref, wuv_ref, *rest, pg, ts):
    lat_pages = rest[:pg]
    pe_pages = rest[pg:2 * pg]
    o_ref = rest[2 * pg]
    ql_s, qp_s, m_s, l_s, acc_s, newk_s, newp_s = rest[2 * pg + 1:]
    p_id = pl.program_id(1)
    rows = MLA_HEADS * ts

    @pl.when(p_id == 0)
    def _():
        for h in range(MLA_HEADS):
            ql_s[h * ts:(h + 1) * ts, :] = ql_ref[:, h * MLA_KV_RANK:(h + 1) * MLA_KV_RANK]
            qp_s[h * ts:(h + 1) * ts, :] = qp_ref[:, h * MLA_ROPE:(h + 1) * MLA_ROPE]
        m_s[...] = jnp.full(m_s.shape, -jnp.inf, F32)
        l_s[...] = jnp.zeros(l_s.shape, F32)
        acc_s[...] = jnp.zeros(acc_s.shape, F32)

    qlb = ql_s[...].astype(BF16)
    qpb = qp_s[...].astype(BF16)

    def update(s, values):
        m_prev = m_s[...]
        m_new = jnp.maximum(m_prev, jnp.max(s, -1, keepdims=True))
        alpha = jnp.exp(m_prev - m_new)
        p = jnp.exp(s - m_new[:, :1])
        l_s[...] = alpha * l_s[...] + jnp.sum(p, -1, keepdims=True)
        pv = jnp.zeros((rows, MLA_KV_RANK), F32)
        for g, v in enumerate(values):
            pv = pv + jnp.dot(p[:, g * PAGE_SIZE:(g + 1) * PAGE_SIZE].astype(BF16), v,
                              preferred_element_type=F32)
        acc_s[...] = acc_s[...] * alpha[:, :1] + pv
        m_s[...] = m_new

    def scores(latb, kpe_t):
        return (lax.dot_general(qlb, latb, _NT, preferred_element_type=F32)
                + jnp.dot(qpb, kpe_t, preferred_element_type=F32))

    lats = [r[...].astype(BF16) for r in lat_pages]
    s_all = jnp.concatenate([scores(lats[g], pe_pages[g][...].astype(BF16)) for g in range(pg)], axis=1)
    update(s_all, lats)

    @pl.when(p_id == pl.num_programs(1) - 1)
    def _():
        newk_s[...] = jnp.zeros(newk_s.shape, F32)
        newp_s[...] = jnp.zeros(newp_s.shape, F32)
        newk_s[0:ts, :] = latn_ref[...]
        newp_s[0:ts, :] = kpen_ref[...]
        latb = newk_s[...].astype(BF16)
        s = scores(latb, newp_s[...].T.astype(BF16))
        t_q = jnp.bitwise_and(lax.broadcasted_iota(jnp.int32, (rows, PAGE_SIZE), 0), ts - 1)
        t_k = lax.broadcasted_iota(jnp.int32, (rows, PAGE_SIZE), 1)
        update(jnp.where(t_k <= t_q, s, -jnp.inf), [latb])
        o_lat = (acc_s[...] / l_s[...][:, :1]).astype(BF16)
        for h in range(MLA_HEADS):
            o_h = jnp.dot(o_lat, wuv_ref[h], preferred_element_type=F32)
            o_ref[:, h * MLA_V:(h + 1) * MLA_V] = o_h[h * ts:(h + 1) * ts, :].astype(o_ref.dtype)


def _attn_sample_call(st, l, qlat, qpe, lat, kpe, cache_lat, cache_pe, page_table, wuv_p):
    S, ts = st.B, st.T
    P = page_table.shape[1]
    pg = next(n for n in (SAMPLE_PAGES_PER_STEP, 8, 2, 1) if P % n == 0)
    HR = MLA_HEADS * MLA_KV_RANK
    HP = MLA_HEADS * MLA_ROPE
    rows = MLA_HEADS * ts
    assert ts & (ts - 1) == 0
    cache_pe_t = jnp.swapaxes(cache_pe, 2, 3)

    def page_spec(shape, g):
        return pl.BlockSpec((None, None) + shape, lambda s, p, pt: (l, pt[s, p * pg + g], 0, 0))

    grid_spec = pltpu.PrefetchScalarGridSpec(
        num_scalar_prefetch=1,
        grid=(S, P // pg),
        in_specs=[
            pl.BlockSpec((ts, HR), lambda s, p, pt: (s, 0)),
            pl.BlockSpec((ts, HP), lambda s, p, pt: (s, 0)),
            pl.BlockSpec((ts, MLA_KV_RANK), lambda s, p, pt: (s, 0)),
            pl.BlockSpec((ts, MLA_ROPE), lambda s, p, pt: (s, 0)),
            pl.BlockSpec(wuv_p.shape, lambda s, p, pt: (0, 0, 0)),
        ] + [page_spec((PAGE_SIZE, MLA_KV_RANK), g) for g in range(pg)]
          + [page_spec((MLA_ROPE, PAGE_SIZE), g) for g in range(pg)],
        out_specs=pl.BlockSpec((ts, MLA_HEADS * MLA_V), lambda s, p, pt: (s, 0)),
        scratch_shapes=[
            pltpu.VMEM((rows, MLA_KV_RANK), F32),
            pltpu.VMEM((rows, MLA_ROPE), F32),
            pltpu.VMEM((rows, 128), F32),
            pltpu.VMEM((rows, 128), F32),
            pltpu.VMEM((rows, MLA_KV_RANK), F32),
            pltpu.VMEM((PAGE_SIZE, MLA_KV_RANK), F32),
            pltpu.VMEM((PAGE_SIZE, MLA_ROPE), F32),
        ],
    )
    return pl.pallas_call(
        functools.partial(_attn_sample_body, pg=pg, ts=ts),
        grid_spec=grid_spec,
        out_shape=jax.ShapeDtypeStruct((S * ts, MLA_HEADS * MLA_V), F32),
        compiler_params=_cp(("parallel", "arbitrary")),
        name="attn_sample",
    )(page_table, qlat, qpe, lat, kpe, wuv_p, *([cache_lat] * pg), *([cache_pe_t] * pg))


def _scan_tiling(B, T):
    bb = min(B, SCAN_BATCH)
    sb = SCAN_BLOCK if T % SCAN_BLOCK == 0 else 8
    assert B % bb == 0 and T % sb == 0
    return bb, _pick(T, 256), sb


def _block_cumsum_matrices(tc, sb):
    t = jnp.arange(tc, dtype=jnp.int32)
    same = (t[:, None] // sb) == (t[None, :] // sb)
    return (jnp.logical_and(same, t[None, :] <= t[:, None]).astype(F32), same.astype(F32))


def _rwkv_body(zr_ref, zk_ref, zv_ref, zl_ref, sh0_ref, s0_ref, mu_ref, w0_ref, w2_ref, a0_ref, a2_ref,
               g2_ref, kk_ref, ka_ref, rk_ref, lng_ref, lnb_ref, blk_ref, ltri_ref, bones_ref, o_ref, sout_ref,
               carry_s, st_s, r_s, k_s, v_s, g_s, ae_s, re_s, bq_s, kq_s, bh_s, kh_s, gb_s, y_s, *, bb, tc, sb):
    c = pl.program_id(1)
    W = RWKV_WIDTH
    npair = RWKV_HEADS // 2

    @pl.when(c == 0)
    def _():
        for b in range(bb):
            carry_s[b] = jnp.broadcast_to(sh0_ref[b], (8, RWKV_PROJ))
        st_s[...] = s0_ref[...].reshape(st_s.shape)

    first_row = lax.broadcasted_iota(jnp.int32, (tc, 1), 0) == 0
    def head_sum(x):
        if x.shape[0] <= 64:
            return _lane_group_sum(x, RWKV_HEAD)
        hi = x.astype(BF16)
        lo = (x - hi.astype(F32)).astype(BF16)
        return (jnp.dot(hi, blk_ref[...], preferred_element_type=F32)
                + jnp.dot(lo, blk_ref[...], preferred_element_type=F32))

    def mix(z, off):
        prev = jnp.where(first_row, carry_s[b][0:1, off:off + z.shape[1]], pltpu.roll(z, 1, 0))
        return z + (prev - z) * mu_ref[:, off:off + z.shape[1]]

    for b in range(bb):
        zr, zk, zv, zl = zr_ref[b], zk_ref[b], zv_ref[b], zl_ref[b]
        r = mix(zr, 0)
        k = mix(zk, W)
        v = mix(zv, 2 * W)
        lo = mix(zl, 3 * W)
        carry_s[b] = jnp.concatenate([zr[tc - 8:], zk[tc - 8:], zv[tc - 8:], zl[tc - 8:]], axis=1)[7:8] \
            + jnp.zeros((8, RWKV_PROJ), F32)
        wa = lo[:, :128]
        lw = jnp.dot(jnp.tanh(wa).astype(BF16), w2_ref[...], preferred_element_type=F32)
        la = jnp.dot(wa.astype(BF16), a2_ref[...], preferred_element_type=F32)
        g = jnp.dot(jax.nn.sigmoid(lo[:, 128:]).astype(BF16), g2_ref[...], preferred_element_type=F32)
        log_w = -jnp.exp(-_softplus(-(w0_ref[...] + lw)) - 0.5)
        a = jax.nn.sigmoid(a0_ref[...] + la)
        kk = k * kk_ref[...]
        kk = kk / jnp.maximum(jnp.sqrt(head_sum(kk * kk)), 1e-12)
        k = k * (1.0 + (a - 1.0) * ka_ref[...])
        cw = jnp.dot(ltri_ref[...], log_w, precision=HIGHEST, preferred_element_type=F32)
        cl = jnp.dot(bones_ref[...], log_w, precision=HIGHEST, preferred_element_type=F32)
        inv_g = jnp.exp(-cw)
        to_end = jnp.exp(cl - cw)
        r_s[b], k_s[b], v_s[b], g_s[b] = r, k, v, g
        ae_s[b] = -kk * jnp.exp(cw - log_w)
        re_s[b] = r * jnp.exp(cw)
        bq_s[b] = kk * a * inv_g
        kq_s[b] = k * inv_g
        bh_s[b] = kk * a * to_end
        kh_s[b] = k * to_end
        gb_s[b] = jnp.exp(cl)

    lo_half = lax.broadcasted_iota(jnp.int32, (sb, 128), 1) < RWKV_HEAD
    lo_state = lax.broadcasted_iota(jnp.int32, (RWKV_HEAD, 128), 1) < RWKV_HEAD
    crow = lax.broadcasted_iota(jnp.int32, (4 * sb, 2 * sb), 0)
    ccol = lax.broadcasted_iota(jnp.int32, (4 * sb, 2 * sb), 1)
    t_idx = jnp.bitwise_and(crow, sb - 1)
    u_idx = jnp.bitwise_and(ccol, sb - 1)
    coef_mask = t_idx > u_idx - jnp.where(crow >= 2 * sb, 1, 0)
    lo_of = lambda x: jnp.where(lo_half, x, 0.0)
    hi_of = lambda x: jnp.where(lo_half, 0.0, x)

    pairs = [(b, p) for b in range(bb) for p in range(npair)]

    def block(bi, carry):
        t0 = pl.multiple_of(bi * sb, sb)
        tile = lambda ref, b, p: ref[b, pl.ds(t0, sb), 128 * p:128 * (p + 1)]
        vbs = [tile(v_s, b, p) for b, p in pairs]
        v_splits = [jnp.concatenate([lo_of(v), hi_of(v)], axis=0).astype(BF16) for v in vbs]
        coefs, cols, xs_v = [], [], []
        for i, (b, p) in enumerate(pairs):
            ae, re = tile(ae_s, b, p), tile(re_s, b, p)
            lhs = jnp.concatenate([lo_of(ae), hi_of(ae), lo_of(re), hi_of(re)], axis=0).astype(BF16)
            rhs = jnp.concatenate([tile(bq_s, b, p), tile(kq_s, b, p)], axis=0).astype(BF16)
            coef = jnp.where(coef_mask, lax.dot_general(lhs, rhs, _NT, preferred_element_type=F32), 0.0)
            coefs.append(coef)
            lak = jnp.concatenate([coef[0:sb, sb:], coef[sb:2 * sb, sb:]], axis=1).astype(BF16)
            xs_v.append(jnp.dot(lak, v_splits[i], preferred_element_type=F32))
            cols.append([jnp.where(lo_half, coef[0:sb, u:u + 1], coef[sb:2 * sb, u:u + 1]) for u in range(sb - 1)])
        states, xs, ys = [], [], []
        for i, (b, p) in enumerate(pairs):
            S = st_s[b * npair + p]
            states.append(S)
            s_bd = jnp.concatenate([jnp.where(lo_state, S, 0.0), jnp.where(lo_state, 0.0, S)], axis=0)
            ar = jnp.concatenate([tile(ae_s, b, p), tile(re_s, b, p)], axis=0).astype(BF16)
            xy = lax.dot_general(ar, s_bd.astype(BF16), _NT, preferred_element_type=F32)
            xs.append(xy[0:sb] + xs_v[i])
            ys.append(xy[sb:])
        for u in range(sb - 1):
            xs = [x + cols[i][u] * x[u:u + 1, :] for i, x in enumerate(xs)]
        for i, (b, p) in enumerate(pairs):
            x, vb, coef = xs[i], vbs[i], coefs[i]
            pv = jnp.concatenate([lo_of(x), lo_of(vb), hi_of(x), hi_of(vb)], axis=0).astype(BF16)
            n01 = jnp.concatenate([coef[2 * sb:3 * sb], coef[3 * sb:]], axis=1).astype(BF16)
            y_s[b, pl.ds(t0, sb), 128 * p:128 * (p + 1)] = ys[i] + jnp.dot(n01, pv, preferred_element_type=F32)
            upd = lax.dot_general(jnp.concatenate([x, vb], axis=0).astype(BF16),
                                  jnp.concatenate([tile(bh_s, b, p), tile(kh_s, b, p)], axis=0).astype(BF16),
                                  (((0,), (0,)), ((), ())), preferred_element_type=F32)
            st_s[b * npair + p] = (states[i] * tile(gb_s, b, p)[0:1, :]
                                   + jnp.where(lo_state, upd[0:RWKV_HEAD], upd[RWKV_HEAD:]))
        return carry

    lax.fori_loop(0, tc // sb, block, 0)

    for b in range(bb):
        y = y_s[b]
        mean = head_sum(y) * (1.0 / RWKV_HEAD)
        d = y - mean
        var = head_sum(d * d) * (1.0 / RWKV_HEAD)
        yn = d * lax.rsqrt(var + RWKV_GN_EPS) * lng_ref[...] + lnb_ref[...]
        bonus = head_sum(r_s[b] * k_s[b] * rk_ref[...]) * v_s[b]
        o_ref[b] = ((yn + bonus) * g_s[b]).astype(o_ref.dtype)

    @pl.when(c == pl.num_programs(1) - 1)
    def _():
        sout_ref[...] = st_s[...].reshape(sout_ref.shape)


def _pack_pairs(s):
    B, H, R, C = s.shape
    return s.reshape(B, H // 2, 2, R, C).transpose(0, 1, 3, 2, 4).reshape(B, H // 2, R, 2 * C)


def _unpack_pairs(s):
    B, HP, R, C2 = s.shape
    return s.reshape(B, HP, R, 2, C2 // 2).transpose(0, 1, 3, 2, 4).reshape(B, HP * 2, R, C2 // 2)


def _rwkv_call(st, z, shift0, s0, prm, out_dtype):
    B, T = st.B, st.T
    z3 = z.reshape(B, T, Z_WIDTH)
    bb, tc, sb = _scan_tiling(B, T)
    W = RWKV_WIDTH
    npair = RWKV_HEADS // 2
    s0p = _pack_pairs(s0)
    zspec = lambda w, idx: pl.BlockSpec((bb, tc, w), lambda g, c: (g, c, idx))
    const = lambda a: pl.BlockSpec(a.shape, lambda g, c: (0,) * a.ndim)
    consts = [prm[k] for k in ("mu", "w0", "w2", "a0", "a2", "g2", "k_k", "k_a", "r_k", "lnx_g", "lnx_b")]
    head_of = jnp.arange(W, dtype=jnp.int32) // RWKV_HEAD
    consts += [(head_of[:, None] == head_of[None, :]).astype(BF16), *_block_cumsum_matrices(tc, sb)]
    o, sout = pl.pallas_call(
        functools.partial(_rwkv_body, bb=bb, tc=tc, sb=sb),
        grid=(B // bb, T // tc),
        in_specs=[
            zspec(W, Z_RWKV_R), zspec(W, Z_RWKV_K), zspec(W, Z_RWKV_V), zspec(256, Z_RWKV_LORA),
            pl.BlockSpec((bb, 1, RWKV_PROJ), lambda g, c: (g, 0, 0)),
            pl.BlockSpec((bb, npair, RWKV_HEAD, 128), lambda g, c: (g, 0, 0, 0)),
        ] + [const(a) for a in consts],
        out_specs=[
            pl.BlockSpec((bb, tc, W), lambda g, c: (g, c, 0)),
            pl.BlockSpec((bb, npair, RWKV_HEAD, 128), lambda g, c: (g, 0, 0, 0)),
        ],
        out_shape=[
            jax.ShapeDtypeStruct((B, T, W), out_dtype),
            jax.ShapeDtypeStruct((B, npair, RWKV_HEAD, 128), F32),
        ],
        scratch_shapes=[
            pltpu.VMEM((bb, 8, RWKV_PROJ), F32),
            pltpu.VMEM((bb * npair, RWKV_HEAD, 128), F32),
        ] + [pltpu.VMEM((bb, tc, W), F32)] * 12,
        compiler_params=_cp(("parallel", "arbitrary")),
        name="rwkv7",
    )(z3, z3, z3, z3, shift0.reshape(B, 1, RWKV_PROJ), s0p, *consts)
    return o.reshape(B * T, W), _unpack_pairs(sout)


def _gla_body(zqk_ref, zv_ref, zgr_ref, zgl_ref, s0_ref, wg_ref, bg_ref, ng_ref, ltri_ref, bones_ref,
              o_ref, sout_ref, st_s, q_s, k_s, cw_s, qe_s, ke_s, ab_s, v_s, y_s, *, bb, tc, sb):
    c = pl.program_id(1)
    npair = GLA_HEADS // 2
    HK = GLA_HEADS * GLA_DK

    @pl.when(c == 0)
    def _():
        st_s[...] = s0_ref[...].reshape(st_s.shape)

    for b in range(bb):
        zqk = zqk_ref[b]
        gate = jnp.dot(zgl_ref[b].astype(BF16), wg_ref[...], preferred_element_type=F32) + bg_ref[...]
        log_a = -_softplus(-gate) * (1.0 / GLA_TAU)
        cw = jnp.dot(ltri_ref[...], log_a, precision=HIGHEST, preferred_element_type=F32)
        cl = jnp.dot(bones_ref[...], log_a, precision=HIGHEST, preferred_element_type=F32)
        q = zqk[:, :HK] * (GLA_DK ** -0.5)
        k = zqk[:, HK:]
        q_s[b], k_s[b], cw_s[b], v_s[b] = q, k, cw, zv_ref[b]
        qe_s[b] = q * jnp.exp(cw)
        ke_s[b] = k * jnp.exp(cl - cw)
        ab_s[b] = jnp.exp(cl)

    lo_half = lax.broadcasted_iota(jnp.int32, (sb, 128), 1) < GLA_DK
    t_idx = lax.broadcasted_iota(jnp.int32, (sb, 128), 0)
    lo_of = lambda x: jnp.where(lo_half, x, 0.0)
    hi_of = lambda x: jnp.where(lo_half, 0.0, x)

    def block(bi, carry):
        t0 = pl.multiple_of(bi * sb, sb)
        for b in range(bb):
            for p in range(npair):
                cs = slice(128 * p, 128 * (p + 1))
                v0 = slice(2 * GLA_DV * p, 2 * GLA_DV * p + GLA_DV)
                v1 = slice(2 * GLA_DV * p + GLA_DV, 2 * GLA_DV * (p + 1))
                qb, kb, cwb, qeb, keb, abb = (x[b, pl.ds(t0, sb), cs] for x in (q_s, k_s, cw_s, qe_s, ke_s, ab_s))
                v0b = v_s[b, pl.ds(t0, sb), v0]
                v1b = v_s[b, pl.ds(t0, sb), v1]
                S = st_s[b * npair + p]
                inter = lax.dot_general(jnp.concatenate([lo_of(qeb), hi_of(qeb)], axis=0).astype(BF16),
                                        S.astype(BF16), _NT, preferred_element_type=F32)
                o0 = inter[0:sb]
                o1 = inter[sb:]
                for u in range(sb):
                    decay = jnp.exp(jnp.minimum(cwb - cwb[u:u + 1, :], 0.0))
                    w_tu = jnp.where(t_idx >= u, qb * decay * kb[u:u + 1, :], 0.0)
                    o0 = o0 + jnp.sum(lo_of(w_tu), -1, keepdims=True) * v0b[u:u + 1, :]
                    o1 = o1 + jnp.sum(hi_of(w_tu), -1, keepdims=True) * v1b[u:u + 1, :]
                y_s[b, pl.ds(t0, sb), v0] = o0
                y_s[b, pl.ds(t0, sb), v1] = o1
                upd = lax.dot_general(jnp.concatenate([v0b, v1b], axis=0).astype(BF16),
                                      jnp.concatenate([lo_of(keb), hi_of(keb)], axis=0).astype(BF16),
                                      (((0,), (0,)), ((), ())), preferred_element_type=F32)
                st_s[b * npair + p] = S * abb[0:1, :] + upd
        return carry

    lax.fori_loop(0, tc // sb, block, 0)

    for b in range(bb):
        gr = zgr_ref[b]
        for h in range(GLA_HEADS):
            hs = slice(h * GLA_DV, (h + 1) * GLA_DV)
            o = y_s[b, :, hs]
            on = o * lax.rsqrt(jnp.mean(o * o, -1, keepdims=True) + RMS_EPS) * ng_ref[...]
            o_ref[b, :, hs] = (on * _silu(gr[:, hs])).astype(o_ref.dtype)

    @pl.when(c == pl.num_programs(1) - 1)
    def _():
        sout_ref[...] = st_s[...].reshape(sout_ref.shape)


def _gla_call(st, z, s0, wg_p, bg, ng, out_dtype):
    B, T = st.B, st.T
    z3 = z.reshape(B, T, Z_WIDTH)
    bb, tc, sb = _scan_tiling(B, T)
    npair = GLA_HEADS // 2
    HK = GLA_HEADS * GLA_DK
    s0p = _pack_pairs(s0.transpose(0, 1, 3, 2))
    zspec = lambda w, idx: pl.BlockSpec((bb, tc, w), lambda g, c: (g, c, idx))
    const = lambda a: pl.BlockSpec(a.shape, lambda g, c: (0,) * a.ndim)
    ltri, bones = _block_cumsum_matrices(tc, sb)
    o, sout = pl.pallas_call(
        functools.partial(_gla_body, bb=bb, tc=tc, sb=sb),
        grid=(B // bb, T // tc),
        in_specs=[
            zspec(512, Z_GLA_QK), zspec(512, Z_GLA_V), zspec(512, Z_GLA_GR), zspec(128, Z_GLA_GL),
            pl.BlockSpec((bb, npair, GLA_DV, 128), lambda g, c: (g, 0, 0, 0)),
            const(wg_p), const(bg), const(ng), const(ltri), const(bones),
        ],
        out_specs=[
            pl.BlockSpec((bb, tc, GLA_WIDTH), lambda g, c: (g, c, 0)),
            pl.BlockSpec((bb, npair, GLA_DV, 128), lambda g, c: (g, 0, 0, 0)),
        ],
        out_shape=[
            jax.ShapeDtypeStruct((B, T, GLA_WIDTH), out_dtype),
            jax.ShapeDtypeStruct((B, npair, GLA_DV, 128), F32),
        ],
        scratch_shapes=[
            pltpu.VMEM((bb * npair, GLA_DV, 128), F32),
        ] + [pltpu.VMEM((bb, tc, HK), F32)] * 6 + [pltpu.VMEM((bb, tc, GLA_WIDTH), F32)] * 2,
        compiler_params=_cp(("parallel", "arbitrary")),
        name="gla",
    )(z3, z3, z3, z3, s0p, wg_p, bg, ng, ltri, bones)
    return o.reshape(B * T, GLA_WIDTH), _unpack_pairs(sout).transpose(0, 1, 3, 2)


def _outproj_body(om_ref, or_ref, og_ref, x_ref, g1_ref, sh2_ref, sc2_ref, w_ref, lng_ref, lnb_ref,
                  x1_ref, h2_ref, *, alpha):
    wm = MLA_HEADS * MLA_V
    mix = (jnp.dot(om_ref[...].astype(BF16), w_ref[0:wm, :], preferred_element_type=F32)
           + jnp.dot(or_ref[...].astype(BF16), w_ref[wm:wm + RWKV_WIDTH, :], preferred_element_type=F32)
           + jnp.dot(og_ref[...].astype(BF16), w_ref[wm + RWKV_WIDTH:, :], preferred_element_type=F32))
    x1 = _layernorm(alpha * x_ref[...] + (1.0 + g1_ref[...]) * mix, lng_ref[...], lnb_ref[...])
    x1_ref[...] = x1
    h2_ref[...] = x1 * (1.0 + sc2_ref[...]) + sh2_ref[...]


def _outproj_call(st, o_mla, o_rwkv, o_gla, x2d, l, w_out_b, ln_g, ln_b, alpha):
    N, D = x2d.shape
    bm = st.row_block(256)
    row = lambda w: pl.BlockSpec((bm, w), lambda i: (i, 0))
    return pl.pallas_call(
        functools.partial(_outproj_body, alpha=alpha),
        grid=(N // bm,),
        in_specs=[
            row(o_mla.shape[1]), row(o_rwkv.shape[1]), row(o_gla.shape[1]), row(D),
            st.mod_spec(l, 2, bm, D), st.mod_spec(l, 3, bm, D), st.mod_spec(l, 4, bm, D),
            pl.BlockSpec((None,) + w_out_b.shape[1:], lambda i: (l, 0, 0)),
            pl.BlockSpec((None, 1, D), lambda i: (l, 0, 0)),
            pl.BlockSpec((None, 1, D), lambda i: (l, 0, 0)),
        ],
        out_specs=[row(D), row(D)],
        out_shape=[jax.ShapeDtypeStruct((N, D), F32), jax.ShapeDtypeStruct((N, D), F32)],
        compiler_params=_cp(("parallel",)),
        name="out_proj",
    )(o_mla, o_rwkv, o_gla, x2d, st.mod, st.mod, st.mod, w_out_b, ln_g, ln_b)


def _router_body(x1_ref, sh2_ref, sc2_ref, wr_ref, o_ref):
    h = x1_ref[...] * (1.0 + sc2_ref[...]) + sh2_ref[...]
    logits = jnp.dot(h, wr_ref[...], precision=HIGHEST, preferred_element_type=F32)
    lane = lax.broadcasted_iota(jnp.int32, logits.shape, 1)
    lane_f = lane.astype(F32)
    lg = jnp.where(lane < N_EXPERTS, logits, -jnp.inf)
    v1 = jnp.max(lg, -1, keepdims=True)
    i1 = jnp.min(jnp.where(lg == v1, lane_f, 128.0), -1, keepdims=True)
    lg2 = jnp.where(lane_f == i1, -jnp.inf, lg)
    v2 = jnp.max(lg2, -1, keepdims=True)
    i2 = jnp.min(jnp.where(lg2 == v2, lane_f, 128.0), -1, keepdims=True)
    e = jnp.exp(v2 - v1)
    g1 = 1.0 / (1.0 + e)
    g2 = e * g1
    o_ref[...] = jnp.where(lane == 0, g1, jnp.where(lane == 1, g2, jnp.where(
        lane == 2, i1, jnp.where(lane == 3, i2, 0.0))))


def _router_call(st, x1, l, wr_p):
    N, D = x1.shape
    bm = st.row_block(512)
    return pl.pallas_call(
        _router_body,
        grid=(N // bm,),
        in_specs=[
            pl.BlockSpec((bm, D), lambda i: (i, 0)),
            st.mod_spec(l, 3, bm, D), st.mod_spec(l, 4, bm, D),
            pl.BlockSpec(wr_p.shape, lambda i: (0, 0)),
        ],
        out_specs=pl.BlockSpec((bm, 128), lambda i: (i, 0)),
        out_shape=jax.ShapeDtypeStruct((N, 128), F32),
        compiler_params=_cp(("parallel",)),
        name="moe_router",
    )(x1, st.mod, st.mod, wr_p)


def _ffn_up_body(te_ref, tv_ref, h_ref, w1_ref, w3_ref, o_ref):
    t = pl.program_id(0)

    @pl.when(tv_ref[t] != 0)
    def _():
        h = h_ref[...].astype(BF16)
        a = jnp.dot(h, w1_ref[...].astype(BF16), preferred_element_type=F32)
        b = jnp.dot(h, w3_ref[...].astype(BF16), preferred_element_type=F32)
        o_ref[...] = (_silu(a) * b).astype(BF16)

    @pl.when(tv_ref[t] == 0)
    def _():
        o_ref[...] = jnp.zeros(o_ref.shape, BF16)


def _ffn_down_body(te_ref, tv_ref, g_ref, w2_ref, o_ref):
    t = pl.program_id(0)

    @pl.when(tv_ref[t] != 0)
    def _():
        o_ref[...] = jnp.dot(g_ref[...], w2_ref[...].astype(BF16), preferred_element_type=F32)

    @pl.when(tv_ref[t] == 0)
    def _():
        o_ref[...] = jnp.zeros(o_ref.shape, F32)


def _ffn_up_gather_body(te_ref, tv_ref, src_ref, h_hbm, w1_ref, w3_ref, o_ref, rows_s, hb_s, sem):
    t = pl.program_id(0)
    bm = rows_s.shape[0]

    @pl.when(jnp.logical_and(pl.program_id(1) == 0, tv_ref[t] != 0))
    def _():
        def issue(r, carry):
            pltpu.make_async_copy(h_hbm.at[pl.ds(src_ref[t * bm + r], 1)], rows_s.at[pl.ds(r, 1)], sem).start()
            return carry

        lax.fori_loop(0, bm, issue, 0, unroll=8)
        pltpu.make_async_copy(h_hbm.at[pl.ds(0, bm)], rows_s, sem).wait()
        hb_s[...] = rows_s[...].astype(BF16)

    @pl.when(tv_ref[t] != 0)
    def _():
        h = hb_s[...]
        a = jnp.dot(h, w1_ref[...].astype(BF16), preferred_element_type=F32)
        b = jnp.dot(h, w3_ref[...].astype(BF16), preferred_element_type=F32)
        o_ref[...] = (_silu(a) * b).astype(BF16)

    @pl.when(tv_ref[t] == 0)
    def _():
        o_ref[...] = jnp.zeros(o_ref.shape, BF16)


def _ffn_call(h, tile_expert, tile_valid, w1, w3, w2, bm, src=None):
    D = h.shape[1]
    F = w1.shape[-1]
    R = h.shape[0] if src is None else src.shape[0]
    nt = R // bm
    bf = 512
    if src is None:
        up = pl.pallas_call(
            _ffn_up_body,
            grid_spec=pltpu.PrefetchScalarGridSpec(
                num_scalar_prefetch=2,
                grid=(nt, F // bf),
                in_specs=[
                    pl.BlockSpec((bm, D), lambda t, j, te, tv: (t, 0)),
                    pl.BlockSpec((None, D, bf), lambda t, j, te, tv: (te[t], 0, j)),
                    pl.BlockSpec((None, D, bf), lambda t, j, te, tv: (te[t], 0, j)),
                ],
                out_specs=pl.BlockSpec((bm, bf), lambda t, j, te, tv: (t, j)),
            ),
            out_shape=jax.ShapeDtypeStruct((R, F), BF16),
            compiler_params=_cp(("parallel", "arbitrary")),
            name="ffn_up",
        )(tile_expert, tile_valid, h, w1, w3)
    else:
        up = pl.pallas_call(
            _ffn_up_gather_body,
            grid_spec=pltpu.PrefetchScalarGridSpec(
                num_scalar_prefetch=3,
                grid=(nt, F // bf),
                in_specs=[
                    pl.BlockSpec(memory_space=pl.ANY),
                    pl.BlockSpec((None, D, bf), lambda t, j, te, tv, sr: (te[t], 0, j)),
                    pl.BlockSpec((None, D, bf), lambda t, j, te, tv, sr: (te[t], 0, j)),
                ],
                out_specs=pl.BlockSpec((bm, bf), lambda t, j, te, tv, sr: (t, j)),
                scratch_shapes=[pltpu.VMEM((bm, D), h.dtype), pltpu.VMEM((bm, D), BF16),
                                pltpu.SemaphoreType.DMA(())],
            ),
            out_shape=jax.ShapeDtypeStruct((R, F), BF16),
            compiler_params=_cp(("arbitrary", "arbitrary")),
            name="ffn_up_gather",
        )(tile_expert, tile_valid, src, h, w1, w3)
    bn = 256
    return pl.pallas_call(
        _ffn_down_body,
        grid_spec=pltpu.PrefetchScalarGridSpec(
            num_scalar_prefetch=2,
            grid=(nt, D // bn),
            in_specs=[
                pl.BlockSpec((bm, F), lambda t, n, te, tv: (t, 0)),
                pl.BlockSpec((None, F, bn), lambda t, n, te, tv: (te[t], 0, n)),
            ],
            out_specs=pl.BlockSpec((bm, bn), lambda t, n, te, tv: (t, n)),
        ),
        out_shape=jax.ShapeDtypeStruct((R, D), F32),
        compiler_params=_cp(("parallel", "arbitrary")),
        name="ffn_down",
    )(tile_expert, tile_valid, up, w2)


def _combine_body(*refs, alpha, nterm):
    x1_ref, g2_ref, lng_ref, lnb_ref = refs[:4]
    f_refs = refs[4:4 + nterm]
    o_ref = refs[-1]
    if nterm == 1:
        f = f_refs[0][...]
    else:
        gates = refs[4 + nterm][...]
        f = f_refs[0][...] * gates[:, 0:1] + f_refs[1][...] * gates[:, 1:2]
    o_ref[...] = _layernorm(alpha * x1_ref[...] + (1.0 + g2_ref[...]) * f, lng_ref[...], lnb_ref[...])


def _combine_call(st, x1, l, ln_g, ln_b, alpha, terms, gates=None, row0=0):
    N, D = x1.shape
    bm = st.row_block(512)
    assert row0 % bm == 0
    row = lambda w: pl.BlockSpec((bm, w), lambda i: (i, 0))
    shared = lambda w: pl.BlockSpec((bm, w), lambda i: (i + row0 // bm, 0))
    args = [x1, st.mod, ln_g, ln_b] + list(terms) + ([gates] if gates is not None else [])
    return pl.pallas_call(
        functools.partial(_combine_body, alpha=alpha, nterm=len(terms)),
        grid=(N // bm,),
        in_specs=[
            row(D), st.mod_spec(l, 5, bm, D),
            pl.BlockSpec((None, 1, D), lambda i: (l, 0, 0)),
            pl.BlockSpec((None, 1, D), lambda i: (l, 0, 0)),
        ] + [shared(D)] * len(terms) + ([shared(128)] if gates is not None else []),
        out_specs=row(D),
        out_shape=jax.ShapeDtypeStruct((N, D), F32),
        compiler_params=_cp(("parallel",)),
        name="ffn_residual",
    )(*args)


def _repack_w_in(w_in):
    L, D, _ = w_in.shape
    w = w_in.astype(BF16)
    zeros = lambda n: jnp.zeros((L, D, n), BF16)
    r0 = MLA_IN
    g0 = MLA_IN + RWKV_PROJ
    HK = GLA_HEADS * GLA_DK
    gq, gk, gv = g0, g0 + HK, g0 + 2 * HK
    ggl = gv + GLA_WIDTH
    ggr = ggl + GLA_GATE_LORA
    parts = [
        w[:, :, :MLA_IN], zeros(Z_MLA_W - MLA_IN),
        w[:, :, r0:r0 + RWKV_PROJ],
        w[:, :, ggl:ggr], zeros(256 - GLA_GATE_LORA),
        w[:, :, gq:gv], w[:, :, gv:ggl], w[:, :, ggr:ggr + GLA_WIDTH],
    ]
    out = jnp.concatenate(parts, axis=-1)
    assert out.shape[-1] == Z_WIDTH
    return out


def _moe_plan(gi, bm):
    n = gi.shape[0]
    E = N_EXPERTS
    experts = gi[:, 2:4].astype(jnp.int32).reshape(-1)
    onehot = (experts[:, None] == jnp.arange(E, dtype=jnp.int32)[None, :]).astype(jnp.int32)
    counts = jnp.sum(onehot, axis=0)
    tiles = (counts + bm - 1) // bm
    tile_end = jnp.cumsum(tiles)
    tile_start = tile_end - tiles
    rank = jnp.take_along_axis(jnp.cumsum(onehot, axis=0) - 1, experts[:, None], axis=1)[:, 0]
    pos = tile_start[experts] * bm + rank
    nt = (2 * n + bm - 1) // bm + E
    src = jnp.zeros((nt * bm,), jnp.int32).at[pos].set(jnp.arange(2 * n, dtype=jnp.int32) // 2)
    t_ids = jnp.arange(nt, dtype=jnp.int32)
    tile_expert = jnp.minimum(jnp.sum((t_ids[:, None] >= tile_end[None, :]).astype(jnp.int32), axis=1), E - 1)
    tile_valid = (t_ids < tile_end[-1]).astype(jnp.int32)
    return src, pos.reshape(n, 2), tile_expert, tile_valid


def kernel(x_prompt, x_sample, cache_kv_latent, cache_k_rope, state_rwkv, state_rwkv_shift, state_gla, page_table, c_prompt, c_sample, w_in, w_out, mla_q_norm, mla_kv_norm, mla_w_q_up, mla_w_uk, mla_w_uv, rwkv_mu, rwkv_w0, rwkv_w2, rwkv_a0, rwkv_a2, rwkv_g2, rwkv_k_k, rwkv_k_a, rwkv_r_k, rwkv_lnx_g, rwkv_lnx_b, gla_w_g2, gla_b_g, gla_norm_g, ada_w, ada_b, ln1_g, ln1_b, ln2_g, ln2_b, ffn_w1, ffn_w3, ffn_w2, moe_router, moe_w1, moe_w3, moe_w2):
    Bp, Tp, D = x_prompt.shape
    Bs, Ts, _ = x_sample.shape
    L = w_in.shape[0]
    past_len = page_table.shape[1] * PAGE_SIZE
    alpha = (2 * L) ** 0.25
    dt = x_prompt.dtype

    n_c = Bp + Bs
    c_all = jnp.concatenate([c_prompt, c_sample, jnp.zeros((-n_c % 8, D), F32)], axis=0)
    mod = _ada_call(c_all, ada_w, ada_b)
    st_p = _Stream(Bp, Tp, mod[:, :Bp].reshape(L, Bp, 1, 6 * D), False, 0)
    st_s = _Stream(Bs, Ts, jnp.repeat(mod[:, Bp:Bp + Bs], Ts, axis=1), True, past_len)

    w_in_p = _repack_w_in(w_in)
    w_out_b = w_out.astype(BF16)
    wq = mla_w_q_up.astype(BF16)
    wq_p = jnp.concatenate([wq[..., :MLA_NOPE].reshape(L, MLA_Q_RANK, -1),
                            wq[..., MLA_NOPE:].reshape(L, MLA_Q_RANK, -1)], axis=-1)
    wuk_p = mla_w_uk.astype(BF16).transpose(0, 2, 3, 1)
    wuv_p = mla_w_uv.astype(BF16).transpose(0, 2, 1, 3)
    inv = 1.0 / (ROPE_BASE ** (jnp.arange(0, MLA_ROPE, 2, dtype=F32) / MLA_ROPE))
    inv128 = jnp.tile(inv, 4).reshape(1, 128)
    zpad = lambda a, n: jnp.concatenate([a, jnp.zeros((L, n) + a.shape[2:], a.dtype)], axis=1)
    zpre = lambda a, n: jnp.concatenate([jnp.zeros((L, n) + a.shape[2:], a.dtype), a], axis=1)
    rw_w2 = zpad(rwkv_w2.astype(BF16), RWKV_A_LORA)
    rw_a2 = zpre(rwkv_a2.astype(BF16), RWKV_W_LORA)
    gla_wg = zpad(gla_w_g2.astype(BF16), 128 - GLA_GATE_LORA)
    row1 = lambda a: a.reshape(1, -1)

    xp = x_prompt.reshape(Bp * Tp, D)
    xs = x_sample.reshape(Bs * Ts, D)
    zeros_shift = jnp.zeros((Bp, RWKV_PROJ), dt)
    zeros_rwkv = jnp.zeros((Bp, RWKV_HEADS, RWKV_HEAD, RWKV_HEAD), F32)
    zeros_gla = jnp.zeros((Bp, GLA_HEADS, GLA_DK, GLA_DV), F32)
    outs_p = [[] for _ in range(5)]
    outs_s = [[] for _ in range(5)]

    for l in range(L):
        rw = dict(mu=row1(rwkv_mu[l]), w0=row1(rwkv_w0[l]), w2=rw_w2[l], a0=row1(rwkv_a0[l]), a2=rw_a2[l],
                  g2=rwkv_g2[l].astype(BF16), k_k=row1(rwkv_k_k[l]), k_a=row1(rwkv_k_a[l]),
                  r_k=row1(rwkv_r_k[l]), lnx_g=row1(rwkv_lnx_g[l]), lnx_b=row1(rwkv_lnx_b[l]))
        streams = []
        for st, x2d, sample in ((st_p, xp, False), (st_s, xs, True)):
            z = _inproj_call(st, x2d, l, w_in_p)
            qlat, qpe, lat, kpe = _mla_prep_call(st, z, row1(mla_q_norm[l]), row1(mla_kv_norm[l]),
                                                 wq_p[l], wuk_p[l], inv128, F32 if sample else BF16)
            if sample:
                o_mla = _attn_sample_call(st, l, qlat, qpe, lat, kpe, cache_kv_latent, cache_k_rope,
                                          page_table, wuv_p[l])
                shift0, s_r0, s_g0 = state_rwkv_shift[l], state_rwkv[l], state_gla[l]
            else:
                o_mla = _attn_prompt_call(st, qlat, qpe, lat, kpe, wuv_p[l])
                shift0, s_r0, s_g0 = zeros_shift, zeros_rwkv, zeros_gla
            o_dt = F32 if sample else BF16
            o_rwkv, s_r = _rwkv_call(st, z, shift0, s_r0, rw, o_dt)
            o_gla, s_g = _gla_call(st, z, s_g0, gla_wg[l], row1(gla_b_g[l]), row1(gla_norm_g[l]), o_dt)
            x1, h2 = _outproj_call(st, o_mla, o_rwkv, o_gla, x2d, l, w_out_b, ln1_g.reshape(L, 1, D),
                                   ln1_b.reshape(L, 1, D), alpha)
            shift = z.reshape(st.B, st.T, Z_WIDTH)[:, -1, Z_MLA_W:Z_MLA_W + RWKV_PROJ]
            acc = outs_s if sample else outs_p
            for lst, val in zip(acc, (lat.reshape(st.B, st.T, -1), kpe.reshape(st.B, st.T, -1),
                                      s_r.astype(dt), shift, s_g.astype(dt))):
                lst.append(val)
            streams.append((st, x1, h2))

        (_, x1p, h2p), (_, x1s, h2s) = streams
        h_all = jnp.concatenate([h2p, h2s], axis=0)
        n_all = h_all.shape[0]
        n_p = h2p.shape[0]
        bm = _pick(n_all, 1024)
        lng, lnb = ln2_g.reshape(L, 1, D), ln2_b.reshape(L, 1, D)
        if l % 2 == 0:
            e = l // 2
            nt = n_all // bm
            f = _ffn_call(h_all, jnp.zeros((nt,), jnp.int32), jnp.ones((nt,), jnp.int32),
                          ffn_w1[e:e + 1], ffn_w3[e:e + 1], ffn_w2[e:e + 1], bm)
            xp = _combine_call(st_p, x1p, l, lng, lnb, alpha, [f])
            xs = _combine_call(st_s, x1s, l, lng, lnb, alpha, [f], row0=n_p)
        else:
            e = l // 2
            wr_p = jnp.concatenate([moe_router[e], jnp.zeros((D, 128 - N_EXPERTS), F32)], axis=1)
            gi = jnp.concatenate([_router_call(st_p, x1p, l, wr_p), _router_call(st_s, x1s, l, wr_p)], axis=0)
            src, pos, tile_expert, tile_valid = _moe_plan(gi, bm)
            f_sorted = _ffn_call(h_all, tile_expert, tile_valid, moe_w1[e], moe_w3[e], moe_w2[e], bm, src=src)
            f0 = f_sorted[pos[:, 0]]
            f1 = f_sorted[pos[:, 1]]
            xp = _combine_call(st_p, x1p, l, lng, lnb, alpha, [f0, f1], gi)
            xs = _combine_call(st_s, x1s, l, lng, lnb, alpha, [f0, f1], gi, row0=n_p)

    stack = lambda lst: jnp.stack(lst)
    return (xp.reshape(Bp, Tp, D), xs.reshape(Bs, Ts, D),
            *[stack(v) for v in outs_p], *[stack(v) for v in outs_s])
```

```python
import functools

import jax
import jax.numpy as jnp
from jax import lax
from jax.experimental import pallas as pl
from jax.experimental.pallas import tpu as pltpu

F32 = jnp.float32
BF16 = jnp.bfloat16
HIGHEST = lax.Precision.HIGHEST

PAGE_SIZE = 128
MLA_HEADS = 8
MLA_NOPE = 128
MLA_ROPE = 64
MLA_V = 128
MLA_Q_RANK = 512
MLA_KV_RANK = 256
MLA_IN = MLA_Q_RANK + MLA_KV_RANK + MLA_ROPE
MLA_SCALE = (MLA_NOPE + MLA_ROPE) ** -0.5
ROPE_BASE = 10000.0
RWKV_HEADS = 8
RWKV_HEAD = 64
RWKV_WIDTH = RWKV_HEADS * RWKV_HEAD
RWKV_W_LORA = 64
RWKV_A_LORA = 64
RWKV_G_LORA = 128
RWKV_PROJ = 3 * RWKV_WIDTH + RWKV_W_LORA + RWKV_A_LORA + RWKV_G_LORA
RWKV_GN_EPS = 64e-5
GLA_HEADS = 4
GLA_DK = 64
GLA_DV = 128
GLA_WIDTH = GLA_HEADS * GLA_DV
GLA_GATE_LORA = 16
GLA_TAU = 16.0
GLA_PROJ = 2 * GLA_HEADS * GLA_DK + 2 * GLA_WIDTH + GLA_GATE_LORA
N_EXPERTS = 8
LN_EPS = 1e-5
RMS_EPS = 1e-6

Z_WIDTH = 4608
Z_MLA_W = 1024
Z_RWKV_R, Z_RWKV_K, Z_RWKV_V = 2, 3, 4
Z_RWKV_LORA = 10
Z_GLA_GL = 22
Z_GLA_QK, Z_GLA_V, Z_GLA_GR = 6, 7, 8

RWKV_BLOCK = 16
GLA_BLOCK = 16
SCAN_BATCH = 4

VMEM_LIMIT_MB = 56


def _cp(sem, vmem_mb=VMEM_LIMIT_MB):
    return pltpu.CompilerParams(dimension_semantics=sem, vmem_limit_bytes=vmem_mb * 2**20)


def _pick(n, pref):
    if n <= pref:
        return n
    b = pref - pref % 8
    while b >= 8:
        if n % b == 0:
            return b
        b -= 8
    return n


def _silu(x):
    return x * jax.nn.sigmoid(x)


def _softplus(u):
    return jnp.maximum(u, 0.0) + jnp.log(1.0 + jnp.exp(-jnp.abs(u)))


def _lane_group_sum(x, group):
    axis = x.ndim - 1
    width = x.shape[axis]
    lane = lax.broadcasted_iota(jnp.int32, x.shape, axis)
    s = 1
    while s < group:
        partner = jnp.where(jnp.bitwise_and(lane, s) == 0,
                            pltpu.roll(x, width - s, axis), pltpu.roll(x, s, axis))
        x = x + partner
        s *= 2
    return x


def _layernorm(y, g, b):
    mu = jnp.mean(y, -1, keepdims=True)
    d = y - mu
    var = jnp.mean(d * d, -1, keepdims=True)
    return d * lax.rsqrt(var + LN_EPS) * g + b


def _ada_body(c_ref, w_ref, b_ref, o_ref):
    c = c_ref[...]
    s = _silu(c).astype(BF16)
    o_ref[...] = jnp.dot(s, w_ref[...].astype(BF16), preferred_element_type=F32) + b_ref[...]


def _ada_call(c_all, ada_w, ada_b):
    L, D, N6 = ada_w.shape
    Mc = c_all.shape[0]
    bn = 1024
    return pl.pallas_call(
        _ada_body,
        grid=(L, N6 // bn),
        in_specs=[
            pl.BlockSpec((Mc, D), lambda l, j: (0, 0)),
            pl.BlockSpec((None, D, bn), lambda l, j: (l, 0, j)),
            pl.BlockSpec((None, 1, bn), lambda l, j: (l, 0, j)),
        ],
        out_specs=pl.BlockSpec((None, Mc, bn), lambda l, j: (l, 0, j)),
        out_shape=jax.ShapeDtypeStruct((L, Mc, N6), F32),
        compiler_params=_cp(("parallel", "parallel")),
        name="ada_mod",
    )(c_all, ada_w, ada_b.reshape(L, 1, N6))


class _Stream:
    def __init__(self, B, T, mod, per_row, pos_base):
        self.B, self.T, self.N = B, T, B * T
        self.mod = mod
        self.per_row = per_row
        self.pos_base = pos_base

    def mod_spec(self, l, k, bm, D):
        if self.per_row:
            return pl.BlockSpec((None, bm, D), lambda i, *_: (l, i, k))
        nb = self.T // bm
        return pl.BlockSpec((None, None, 1, D), lambda i, *_: (l, i // nb, 0, k))

    def row_block(self, pref):
        return _pick(self.N if self.per_row else self.T, pref)


def _inproj_body(x_ref, sh_ref, sc_ref, w_ref, o_ref, h_s):
    @pl.when(pl.program_id(1) == 0)
    def _():
        h_s[...] = (x_ref[...] * (1.0 + sc_ref[...]) + sh_ref[...]).astype(BF16)

    o_ref[...] = jnp.dot(h_s[...], w_ref[...], preferred_element_type=F32)


def _inproj_call(st, x2d, l, w_in_p):
    N, D = x2d.shape
    Wz = w_in_p.shape[-1]
    bm = st.row_block(1024)
    bn = 512
    return pl.pallas_call(
        _inproj_body,
        grid=(N // bm, Wz // bn),
        in_specs=[
            pl.BlockSpec((bm, D), lambda i, j: (i, 0)),
            st.mod_spec(l, 0, bm, D),
            st.mod_spec(l, 1, bm, D),
            pl.BlockSpec((None, D, bn), lambda i, j: (l, 0, j)),
        ],
        out_specs=pl.BlockSpec((bm, bn), lambda i, j: (i, j)),
        out_shape=jax.ShapeDtypeStruct((N, Wz), F32),
        scratch_shapes=[pltpu.VMEM((bm, D), BF16)],
        compiler_params=_cp(("parallel", "arbitrary")),
        name="in_proj",
    )(x2d, st.mod, st.mod, w_in_p)


def _rope(x, cos, sin_signed, first_half):
    w = x.shape[1]
    swapped = jnp.where(first_half, pltpu.roll(x, w - 32, 1), pltpu.roll(x, 32, 1))
    return x * cos + swapped * sin_signed


def _mla_prep_body(z_ref, gq_ref, gkv_ref, wq_ref, wuk_ref, inv_ref, qlat_ref, qpe_ref, lat_ref, kpe_ref,
                   *, bm, period, pos_base):
    i = pl.program_id(0)
    z = z_ref[...]
    zq = z[:, :MLA_Q_RANK]
    zkv = z[:, MLA_Q_RANK:MLA_Q_RANK + MLA_KV_RANK]
    zpe = z[:, MLA_Q_RANK + MLA_KV_RANK:MLA_Q_RANK + MLA_KV_RANK + 128]
    qn = zq * lax.rsqrt(jnp.mean(zq * zq, -1, keepdims=True) + RMS_EPS) * gq_ref[...]
    q = jnp.dot(qn.astype(BF16), wq_ref[...], preferred_element_type=F32)
    lat_ref[...] = zkv * lax.rsqrt(jnp.mean(zkv * zkv, -1, keepdims=True) + RMS_EPS) * gkv_ref[...]

    row = lax.broadcasted_iota(jnp.int32, (bm, 128), 0) + i * bm
    pos = (pos_base + jnp.bitwise_and(row, period - 1)).astype(F32)
    ang = pos * inv_ref[...]
    cos = jnp.cos(ang)
    sin = jnp.sin(ang)
    first = jnp.bitwise_and(lax.broadcasted_iota(jnp.int32, (bm, 128), 1), 63) < 32
    sin_s = jnp.where(first, -sin, sin)
    kpe_ref[...] = _rope(zpe, cos, sin_s, first)[:, :MLA_ROPE]

    npe = MLA_HEADS * MLA_ROPE // 128
    cos4 = jnp.concatenate([cos] * npe, axis=1)
    sin4 = jnp.concatenate([sin_s] * npe, axis=1)
    first4 = jnp.bitwise_and(lax.broadcasted_iota(jnp.int32, (bm, 128 * npe), 1), 63) < 32
    q_pe = q[:, MLA_HEADS * MLA_NOPE:]
    qpe_ref[...] = (_rope(q_pe, cos4, sin4, first4) * MLA_SCALE).astype(qpe_ref.dtype)
    for h in range(MLA_HEADS):
        qh = q[:, h * MLA_NOPE:(h + 1) * MLA_NOPE].astype(BF16)
        ql = jnp.dot(qh, wuk_ref[h], preferred_element_type=F32) * MLA_SCALE
        qlat_ref[:, h * MLA_KV_RANK:(h + 1) * MLA_KV_RANK] = ql.astype(qlat_ref.dtype)


def _mla_prep_call(st, z, gq, gkv, wq_p, wuk_p, inv128, q_dtype):
    N = z.shape[0]
    bm = _pick(N, 512)
    HR = MLA_HEADS * MLA_KV_RANK
    HP = MLA_HEADS * MLA_ROPE
    assert st.T & (st.T - 1) == 0
    body = functools.partial(_mla_prep_body, bm=bm, period=st.T, pos_base=st.pos_base)
    const = lambda shape: pl.BlockSpec(shape, lambda i: (0,) * len(shape))
    return pl.pallas_call(
        body,
        grid=(N // bm,),
        in_specs=[
            pl.BlockSpec((bm, Z_MLA_W), lambda i: (i, 0)),
            const((1, MLA_Q_RANK)), const((1, MLA_KV_RANK)),
            const(wq_p.shape), const(wuk_p.shape), const((1, 128)),
        ],
        out_specs=[
            pl.BlockSpec((bm, HR), lambda i: (i, 0)),
            pl.BlockSpec((bm, HP), lambda i: (i, 0)),
            pl.BlockSpec((bm, MLA_KV_RANK), lambda i: (i, 0)),
            pl.BlockSpec((bm, MLA_ROPE), lambda i: (i, 0)),
        ],
        out_shape=[
            jax.ShapeDtypeStruct((N, HR), q_dtype),
            jax.ShapeDtypeStruct((N, HP), q_dtype),
            jax.ShapeDtypeStruct((N, MLA_KV_RANK), F32),
            jax.ShapeDtypeStruct((N, MLA_ROPE), F32),
        ],
        compiler_params=_cp(("parallel",)),
        name="mla_prep",
    )(z, gq, gkv, wq_p, wuk_p, inv128)


_NT = (((1,), (1,)), ((), ()))


def _softmax_update(s, h, m_s, l_s, acc_s, values_bf16):
    m_prev = m_s[h]
    m_new = jnp.maximum(m_prev, jnp.max(s, -1, keepdims=True))
    alpha = jnp.exp(m_prev - m_new)
    p = jnp.exp(s - m_new[:, :1])
    l_s[h] = alpha * l_s[h] + jnp.sum(p, -1, keepdims=True)
    acc_s[h] = acc_s[h] * alpha[:, :1] + jnp.dot(p.astype(BF16), values_bf16, preferred_element_type=F32)
    m_s[h] = m_new


def _attn_prompt_body(ql_ref, qp_ref, lat_ref, kpe_ref, wuv_ref, o_ref, m_s, l_s, acc_s, *, bq):
    qi = pl.program_id(1)
    ki = pl.program_id(2)

    @pl.when(ki == 0)
    def _():
        m_s[...] = jnp.full(m_s.shape, -jnp.inf, F32)
        l_s[...] = jnp.zeros(l_s.shape, F32)
        acc_s[...] = jnp.zeros(acc_s.shape, F32)

    def step(masked):
        latb = lat_ref[...].astype(BF16)
        kpb = kpe_ref[...].astype(BF16)
        if masked:
            causal = (lax.broadcasted_iota(jnp.int32, (bq, bq), 0)
                      >= lax.broadcasted_iota(jnp.int32, (bq, bq), 1))
        for h in range(MLA_HEADS):
            s = (lax.dot_general(ql_ref[:, h * MLA_KV_RANK:(h + 1) * MLA_KV_RANK], latb, _NT,
                                 preferred_element_type=F32)
                 + lax.dot_general(qp_ref[:, h * MLA_ROPE:(h + 1) * MLA_ROPE], kpb, _NT,
                                   preferred_element_type=F32))
            if masked:
                s = jnp.where(causal, s, -jnp.inf)
            _softmax_update(s, h, m_s, l_s, acc_s, latb)

    @pl.when(ki < qi)
    def _():
        step(False)

    @pl.when(ki == qi)
    def _():
        step(True)
        for h in range(MLA_HEADS):
            o_lat = (acc_s[h] / l_s[h][:, :1]).astype(BF16)
            o_ref[:, h * MLA_V:(h + 1) * MLA_V] = jnp.dot(
                o_lat, wuv_ref[h], preferred_element_type=F32).astype(o_ref.dtype)


def _attn_prompt_call(st, qlat, qpe, lat, kpe, wuv_p):
    B, T = st.B, st.T
    bq = _pick(T, 512)
    nq = T // bq
    HR = MLA_HEADS * MLA_KV_RANK
    HP = MLA_HEADS * MLA_ROPE
    kv = lambda b, qi, ki: (b * nq + jnp.minimum(ki, qi), 0)
    return pl.pallas_call(
        functools.partial(_attn_prompt_body, bq=bq),
        grid=(B, nq, nq),
        in_specs=[
            pl.BlockSpec((bq, HR), lambda b, qi, ki: (b * nq + qi, 0)),
            pl.BlockSpec((bq, HP), lambda b, qi, ki: (b * nq + qi, 0)),
            pl.BlockSpec((bq, MLA_KV_RANK), kv),
            pl.BlockSpec((bq, MLA_ROPE), kv),
            pl.BlockSpec(wuv_p.shape, lambda b, qi, ki: (0, 0, 0)),
        ],
        out_specs=pl.BlockSpec((bq, MLA_HEADS * MLA_V), lambda b, qi, ki: (b * nq + qi, 0)),
        out_shape=jax.ShapeDtypeStruct((B * T, MLA_HEADS * MLA_V), BF16),
        scratch_shapes=[
            pltpu.VMEM((MLA_HEADS, bq, 128), F32),
            pltpu.VMEM((MLA_HEADS, bq, 128), F32),
            pltpu.VMEM((MLA_HEADS, bq, MLA_KV_RANK), F32),
        ],
        compiler_params=_cp(("parallel", "parallel", "arbitrary")),
        name="attn_prompt",
    )(qlat, qpe, lat, kpe, wuv_p)


def _attn_sample_body(pt_ref, ql_ref, qp_ref, latn_ref, kpen_ref, wuv_ref, cache_lat, cache_pe_t, o_ref,
                      lat_buf, pe_buf, sem, *, layer, npages, ts):
    s_id = pl.program_id(0)
    slot = jnp.bitwise_and(s_id, 1)
    rows = MLA_HEADS * ts

    def start_pages(seq, slot_):
        def body(g, carry):
            page = pt_ref[seq, g]
            pltpu.make_async_copy(cache_lat.at[layer, page], lat_buf.at[slot_, g], sem.at[slot_]).start()
            pltpu.make_async_copy(cache_pe_t.at[layer, page], pe_buf.at[slot_, g], sem.at[slot_]).start()
            return carry

        lax.fori_loop(0, npages, body, 0, unroll=4)

    @pl.when(s_id == 0)
    def _():
        start_pages(0, 0)

    pltpu.make_async_copy(cache_lat.at[layer, pl.ds(0, npages)], lat_buf.at[slot], sem.at[slot]).wait()
    pltpu.make_async_copy(cache_pe_t.at[layer, pl.ds(0, npages)], pe_buf.at[slot], sem.at[slot]).wait()

    @pl.when(s_id + 1 < pl.num_programs(0))
    def _():
        start_pages(s_id + 1, 1 - slot)

    qlb = jnp.concatenate([ql_ref[:, h * MLA_KV_RANK:(h + 1) * MLA_KV_RANK] for h in range(MLA_HEADS)],
                          axis=0).astype(BF16)
    qpb = jnp.concatenate([qp_ref[:, h * MLA_ROPE:(h + 1) * MLA_ROPE] for h in range(MLA_HEADS)],
                          axis=0).astype(BF16)

    def scores(latb, kpe_t):
        return (lax.dot_general(qlb, latb, _NT, preferred_element_type=F32)
                + jnp.dot(qpb, kpe_t, preferred_element_type=F32))

    lats = [lat_buf[slot, g].astype(BF16) for g in range(npages)]
    s_pages = [scores(lats[g], pe_buf[slot, g].astype(BF16)) for g in range(npages)]
    pad = PAGE_SIZE - ts
    lat_new = jnp.concatenate([latn_ref[...], jnp.zeros((pad, MLA_KV_RANK), F32)], axis=0).astype(BF16)
    kpe_new = jnp.concatenate([kpen_ref[...], jnp.zeros((pad, MLA_ROPE), F32)], axis=0).T.astype(BF16)
    t_q = jnp.bitwise_and(lax.broadcasted_iota(jnp.int32, (rows, PAGE_SIZE), 0), ts - 1)
    t_k = lax.broadcasted_iota(jnp.int32, (rows, PAGE_SIZE), 1)
    s_new = jnp.where(t_k <= t_q, scores(lat_new, kpe_new), -jnp.inf)
    s_all = jnp.concatenate(s_pages + [s_new], axis=1)
    p = jnp.exp(s_all - jnp.max(s_all, -1, keepdims=True))
    denom = jnp.sum(p, -1, keepdims=True)
    pv = jnp.zeros((rows, MLA_KV_RANK), F32)
    for g, v in enumerate(lats + [lat_new]):
        pv = pv + jnp.dot(p[:, g * PAGE_SIZE:(g + 1) * PAGE_SIZE].astype(BF16), v, preferred_element_type=F32)
    o_lat = (pv / denom).astype(BF16)
    for h in range(MLA_HEADS):
        o_h = jnp.dot(o_lat, wuv_ref[h], preferred_element_type=F32)
        o_ref[:, h * MLA_V:(h + 1) * MLA_V] = o_h[h * ts:(h + 1) * ts, :].astype(o_ref.dtype)


def _attn_sample_call(st, l, qlat, qpe, lat, kpe, cache_lat, cache_pe, page_table, wuv_p):
    S, ts = st.B, st.T
    P = page_table.shape[1]
    HR = MLA_HEADS * MLA_KV_RANK
    HP = MLA_HEADS * MLA_ROPE
    assert ts & (ts - 1) == 0
    page_bytes = PAGE_SIZE * (MLA_KV_RANK + MLA_ROPE) * 4
    assert 2 * P * page_bytes <= 32 * 2**20, "the whole past of one sequence must fit the two page slots"
    cache_pe_t = jnp.swapaxes(cache_pe, 2, 3)
    grid_spec = pltpu.PrefetchScalarGridSpec(
        num_scalar_prefetch=1,
        grid=(S,),
        in_specs=[
            pl.BlockSpec((ts, HR), lambda s, pt: (s, 0)),
            pl.BlockSpec((ts, HP), lambda s, pt: (s, 0)),
            pl.BlockSpec((ts, MLA_KV_RANK), lambda s, pt: (s, 0)),
            pl.BlockSpec((ts, MLA_ROPE), lambda s, pt: (s, 0)),
            pl.BlockSpec(wuv_p.shape, lambda s, pt: (0, 0, 0)),
            pl.BlockSpec(memory_space=pl.ANY),
            pl.BlockSpec(memory_space=pl.ANY),
        ],
        out_specs=pl.BlockSpec((ts, MLA_HEADS * MLA_V), lambda s, pt: (s, 0)),
        scratch_shapes=[
            pltpu.VMEM((2, P, PAGE_SIZE, MLA_KV_RANK), F32),
            pltpu.VMEM((2, P, MLA_ROPE, PAGE_SIZE), F32),
            pltpu.SemaphoreType.DMA((2,)),
        ],
    )
    return pl.pallas_call(
        functools.partial(_attn_sample_body, layer=l, npages=P, ts=ts),
        grid_spec=grid_spec,
        out_shape=jax.ShapeDtypeStruct((S * ts, MLA_HEADS * MLA_V), F32),
        compiler_params=_cp(("arbitrary",)),
        name="attn_sample",
    )(page_table, qlat, qpe, lat, kpe, wuv_p, cache_lat, cache_pe_t)


def _scan_tiling(B, T, block):
    bb = min(B, SCAN_BATCH)
    sb = block if T % block == 0 else 8
    assert B % bb == 0 and T % sb == 0
    return bb, _pick(T, 256), sb


def _block_cumsum_matrices(tc, sb):
    t = jnp.arange(tc, dtype=jnp.int32)
    same = (t[:, None] // sb) == (t[None, :] // sb)
    return (jnp.logical_and(same, t[None, :] <= t[:, None]).astype(F32), same.astype(F32))


def _rwkv_body(zr_ref, zk_ref, zv_ref, zl_ref, sh0_ref, s0_ref, mu_ref, w0_ref, w2_ref, a0_ref, a2_ref,
               g2_ref, kk_ref, ka_ref, rk_ref, lng_ref, lnb_ref, blk_ref, ltri_ref, bones_ref, o_ref, sout_ref,
               carry_s, st_s, r_s, k_s, v_s, g_s, ae_s, re_s, bq_s, kq_s, bh_s, kh_s, gb_s, y_s, *, bb, tc, sb):
    c = pl.program_id(1)
    W = RWKV_WIDTH
    npair = RWKV_HEADS // 2

    @pl.when(c == 0)
    def _():
        for b in range(bb):
            carry_s[b] = jnp.broadcast_to(sh0_ref[b], (8, RWKV_PROJ))
        st_s[...] = s0_ref[...].reshape(st_s.shape)

    first_row = lax.broadcasted_iota(jnp.int32, (tc, 1), 0) == 0
    def head_sum(x):
        if x.shape[0] <= 64:
            return _lane_group_sum(x, RWKV_HEAD)
        hi = x.astype(BF16)
        lo = (x - hi.astype(F32)).astype(BF16)
        return (jnp.dot(hi, blk_ref[...], preferred_element_type=F32)
                + jnp.dot(lo, blk_ref[...], preferred_element_type=F32))

    def mix(z, off):
        prev = jnp.where(first_row, carry_s[b][0:1, off:off + z.shape[1]], pltpu.roll(z, 1, 0))
        return z + (prev - z) * mu_ref[:, off:off + z.shape[1]]

    for b in range(bb):
        zr, zk, zv, zl = zr_ref[b], zk_ref[b], zv_ref[b], zl_ref[b]
        r = mix(zr, 0)
        k = mix(zk, W)
        v = mix(zv, 2 * W)
        lo = mix(zl, 3 * W)
        carry_s[b] = jnp.concatenate([zr[tc - 8:], zk[tc - 8:], zv[tc - 8:], zl[tc - 8:]], axis=1)[7:8] \
            + jnp.zeros((8, RWKV_PROJ), F32)
        wa = lo[:, :128]
        lw = jnp.dot(jnp.tanh(wa).astype(BF16), w2_ref[...], preferred_element_type=F32)
        la = jnp.dot(wa.astype(BF16), a2_ref[...], preferred_element_type=F32)
        g = jnp.dot(jax.nn.sigmoid(lo[:, 128:]).astype(BF16), g2_ref[...], preferred_element_type=F32)
        log_w = -jnp.exp(-_softplus(-(w0_ref[...] + lw)) - 0.5)
        a = jax.nn.sigmoid(a0_ref[...] + la)
        kk = k * kk_ref[...]
        kk = kk / jnp.maximum(jnp.sqrt(head_sum(kk * kk)), 1e-12)
        k = k * (1.0 + (a - 1.0) * ka_ref[...])
        cw = jnp.dot(ltri_ref[...], log_w, precision=HIGHEST, preferred_element_type=F32)
        cl = jnp.dot(bones_ref[...], log_w, precision=HIGHEST, preferred_element_type=F32)
        inv_g = jnp.exp(-cw)
        to_end = jnp.exp(cl - cw)
        r_s[b], k_s[b], v_s[b], g_s[b] = r, k, v, g
        ae_s[b] = -kk * jnp.exp(cw - log_w)
        re_s[b] = r * jnp.exp(cw)
        bq_s[b] = kk * a * inv_g
        kq_s[b] = k * inv_g
        bh_s[b] = kk * a * to_end
        kh_s[b] = k * to_end
        gb_s[b] = jnp.exp(cl)

    lo_half = lax.broadcasted_iota(jnp.int32, (sb, 128), 1) < RWKV_HEAD
    lo_state = lax.broadcasted_iota(jnp.int32, (RWKV_HEAD, 128), 1) < RWKV_HEAD
    crow = lax.broadcasted_iota(jnp.int32, (4 * sb, 2 * sb), 0)
    ccol = lax.broadcasted_iota(jnp.int32, (4 * sb, 2 * sb), 1)
    t_idx = jnp.bitwise_and(crow, sb - 1)
    u_idx = jnp.bitwise_and(ccol, sb - 1)
    coef_mask = t_idx > u_idx - jnp.where(crow >= 2 * sb, 1, 0)
    lo_of = lambda x: jnp.where(lo_half, x, 0.0)
    hi_of = lambda x: jnp.where(lo_half, 0.0, x)

    pairs = [(b, p) for b in range(bb) for p in range(npair)]

    def block(bi, carry):
        t0 = pl.multiple_of(bi * sb, sb)
        tile = lambda ref, b, p: ref[b, pl.ds(t0, sb), 128 * p:128 * (p + 1)]
        vbs = [tile(v_s, b, p) for b, p in pairs]
        v_splits = [jnp.concatenate([lo_of(v), hi_of(v)], axis=0).astype(BF16) for v in vbs]
        coefs, cols, xs_v = [], [], []
        for i, (b, p) in enumerate(pairs):
            ae, re = tile(ae_s, b, p), tile(re_s, b, p)
            lhs = jnp.concatenate([lo_of(ae), hi_of(ae), lo_of(re), hi_of(re)], axis=0).astype(BF16)
            rhs = jnp.concatenate([tile(bq_s, b, p), tile(kq_s, b, p)], axis=0).astype(BF16)
            coef = jnp.where(coef_mask, lax.dot_general(lhs, rhs, _NT, preferred_element_type=F32), 0.0)
            coefs.append(coef)
            lak = jnp.concatenate([coef[0:sb, sb:], coef[sb:2 * sb, sb:]], axis=1).astype(BF16)
            xs_v.append(jnp.dot(lak, v_splits[i], preferred_element_type=F32))
            cols.append([jnp.where(lo_half, coef[0:sb, u:u + 1], coef[sb:2 * sb, u:u + 1]) for u in range(sb - 1)])
        states, xs, ys = [], [], []
        for i, (b, p) in enumerate(pairs):
            S = st_s[b * npair + p]
            states.append(S)
            s_bd = jnp.concatenate([jnp.where(lo_state, S, 0.0), jnp.where(lo_state, 0.0, S)], axis=0)
            ar = jnp.concatenate([tile(ae_s, b, p), tile(re_s, b, p)], axis=0).astype(BF16)
            xy = lax.dot_general(ar, s_bd.astype(BF16), _NT, preferred_element_type=F32)
            xs.append(xy[0:sb] + xs_v[i])
            ys.append(xy[sb:])
        for u in range(sb - 1):
            xs = [x + cols[i][u] * x[u:u + 1, :] for i, x in enumerate(xs)]
        for i, (b, p) in enumerate(pairs):
            x, vb, coef = xs[i], vbs[i], coefs[i]
            pv = jnp.concatenate([lo_of(x), lo_of(vb), hi_of(x), hi_of(vb)], axis=0).astype(BF16)
            n01 = jnp.concatenate([coef[2 * sb:3 * sb], coef[3 * sb:]], axis=1).astype(BF16)
            y_s[b, pl.ds(t0, sb), 128 * p:128 * (p + 1)] = ys[i] + jnp.dot(n01, pv, preferred_element_type=F32)
            upd = lax.dot_general(jnp.concatenate([x, vb], axis=0).astype(BF16),
                                  jnp.concatenate([tile(bh_s, b, p), tile(kh_s, b, p)], axis=0).astype(BF16),
                                  (((0,), (0,)), ((), ())), preferred_element_type=F32)
            st_s[b * npair + p] = (states[i] * tile(gb_s, b, p)[0:1, :]
                                   + jnp.where(lo_state, upd[0:RWKV_HEAD], upd[RWKV_HEAD:]))
        return carry

    lax.fori_loop(0, tc // sb, block, 0)

    for b in range(bb):
        y = y_s[b]
        mean = head_sum(y) * (1.0 / RWKV_HEAD)
        d = y - mean
        var = head_sum(d * d) * (1.0 / RWKV_HEAD)
        yn = d * lax.rsqrt(var + RWKV_GN_EPS) * lng_ref[...] + lnb_ref[...]
        bonus = head_sum(r_s[b] * k_s[b] * rk_ref[...]) * v_s[b]
        o_ref[b] = ((yn + bonus) * g_s[b]).astype(o_ref.dtype)

    @pl.when(c == pl.num_programs(1) - 1)
    def _():
        sout_ref[...] = st_s[...].reshape(sout_ref.shape)


def _pack_pairs(s):
    B, H, R, C = s.shape
    return s.reshape(B, H // 2, 2, R, C).transpose(0, 1, 3, 2, 4).reshape(B, H // 2, R, 2 * C)


def _unpack_pairs(s):
    B, HP, R, C2 = s.shape
    return s.reshape(B, HP, R, 2, C2 // 2).transpose(0, 1, 3, 2, 4).reshape(B, HP * 2, R, C2 // 2)


def _rwkv_call(st, z, shift0, s0, prm, out_dtype):
    B, T = st.B, st.T
    z3 = z.reshape(B, T, Z_WIDTH)
    bb, tc, sb = _scan_tiling(B, T, RWKV_BLOCK)
    W = RWKV_WIDTH
    npair = RWKV_HEADS // 2
    s0p = _pack_pairs(s0)
    zspec = lambda w, idx: pl.BlockSpec((bb, tc, w), lambda g, c: (g, c, idx))
    const = lambda a: pl.BlockSpec(a.shape, lambda g, c: (0,) * a.ndim)
    consts = [prm[k] for k in ("mu", "w0", "w2", "a0", "a2", "g2", "k_k", "k_a", "r_k", "lnx_g", "lnx_b")]
    head_of = jnp.arange(W, dtype=jnp.int32) // RWKV_HEAD
    consts += [(head_of[:, None] == head_of[None, :]).astype(BF16), *_block_cumsum_matrices(tc, sb)]
    o, sout = pl.pallas_call(
        functools.partial(_rwkv_body, bb=bb, tc=tc, sb=sb),
        grid=(B // bb, T // tc),
        in_specs=[
            zspec(W, Z_RWKV_R), zspec(W, Z_RWKV_K), zspec(W, Z_RWKV_V), zspec(256, Z_RWKV_LORA),
            pl.BlockSpec((bb, 1, RWKV_PROJ), lambda g, c: (g, 0, 0)),
            pl.BlockSpec((bb, npair, RWKV_HEAD, 128), lambda g, c: (g, 0, 0, 0)),
        ] + [const(a) for a in consts],
        out_specs=[
            pl.BlockSpec((bb, tc, W), lambda g, c: (g, c, 0)),
            pl.BlockSpec((bb, npair, RWKV_HEAD, 128), lambda g, c: (g, 0, 0, 0)),
        ],
        out_shape=[
            jax.ShapeDtypeStruct((B, T, W), out_dtype),
            jax.ShapeDtypeStruct((B, npair, RWKV_HEAD, 128), F32),
        ],
        scratch_shapes=[
            pltpu.VMEM((bb, 8, RWKV_PROJ), F32),
            pltpu.VMEM((bb * npair, RWKV_HEAD, 128), F32),
        ] + [pltpu.VMEM((bb, tc, W), F32)] * 12,
        compiler_params=_cp(("parallel", "arbitrary")),
        name="rwkv7",
    )(z3, z3, z3, z3, shift0.reshape(B, 1, RWKV_PROJ), s0p, *consts)
    return o.reshape(B * T, W), _unpack_pairs(sout)


def _gla_body(zqk_ref, zv_ref, zgr_ref, zgl_ref, s0_ref, wg_ref, bg_ref, ng_ref, ltri_ref, bones_ref,
              o_ref, sout_ref, st_s, q_s, k_s, cw_s, qe_s, ke_s, ab_s, v_s, y_s, *, bb, tc, sb):
    c = pl.program_id(1)
    npair = GLA_HEADS // 2
    HK = GLA_HEADS * GLA_DK

    @pl.when(c == 0)
    def _():
        st_s[...] = s0_ref[...].reshape(st_s.shape)

    for b in range(bb):
        zqk = zqk_ref[b]
        gate = jnp.dot(zgl_ref[b].astype(BF16), wg_ref[...], preferred_element_type=F32) + bg_ref[...]
        log_a = -_softplus(-gate) * (1.0 / GLA_TAU)
        cw = jnp.dot(ltri_ref[...], log_a, precision=HIGHEST, preferred_element_type=F32)
        cl = jnp.dot(bones_ref[...], log_a, precision=HIGHEST, preferred_element_type=F32)
        q = zqk[:, :HK] * (GLA_DK ** -0.5)
        k = zqk[:, HK:]
        q_s[b], k_s[b], cw_s[b], v_s[b] = q, k, cw, zv_ref[b]
        qe_s[b] = q * jnp.exp(cw)
        ke_s[b] = k * jnp.exp(cl - cw)
        ab_s[b] = jnp.exp(cl)

    lo_half = lax.broadcasted_iota(jnp.int32, (sb, 128), 1) < GLA_DK
    t_idx = lax.broadcasted_iota(jnp.int32, (sb, 128), 0)
    lo_of = lambda x: jnp.where(lo_half, x, 0.0)
    hi_of = lambda x: jnp.where(lo_half, 0.0, x)

    def block(bi, carry):
        t0 = pl.multiple_of(bi * sb, sb)
        for b in range(bb):
            for p in range(npair):
                cs = slice(128 * p, 128 * (p + 1))
                v0 = slice(2 * GLA_DV * p, 2 * GLA_DV * p + GLA_DV)
                v1 = slice(2 * GLA_DV * p + GLA_DV, 2 * GLA_DV * (p + 1))
                qb, kb, cwb, qeb, keb, abb = (x[b, pl.ds(t0, sb), cs] for x in (q_s, k_s, cw_s, qe_s, ke_s, ab_s))
                v0b = v_s[b, pl.ds(t0, sb), v0]
                v1b = v_s[b, pl.ds(t0, sb), v1]
                S = st_s[b * npair + p]
                inter = lax.dot_general(jnp.concatenate([lo_of(qeb), hi_of(qeb)], axis=0).astype(BF16),
                                        S.astype(BF16), _NT, preferred_element_type=F32)
                o0 = inter[0:sb]
                o1 = inter[sb:]
                for u in range(sb):
                    decay = jnp.exp(jnp.minimum(cwb - cwb[u:u + 1, :], 0.0))
                    w_tu = jnp.where(t_idx >= u, qb * decay * kb[u:u + 1, :], 0.0)
                    o0 = o0 + jnp.sum(lo_of(w_tu), -1, keepdims=True) * v0b[u:u + 1, :]
                    o1 = o1 + jnp.sum(hi_of(w_tu), -1, keepdims=True) * v1b[u:u + 1, :]
                y_s[b, pl.ds(t0, sb), v0] = o0
                y_s[b, pl.ds(t0, sb), v1] = o1
                upd = lax.dot_general(jnp.concatenate([v0b, v1b], axis=0).astype(BF16),
                                      jnp.concatenate([lo_of(keb), hi_of(keb)], axis=0).astype(BF16),
                                      (((0,), (0,)), ((), ())), preferred_element_type=F32)
                st_s[b * npair + p] = S * abb[0:1, :] + upd
        return carry

    lax.fori_loop(0, tc // sb, block, 0)

    for b in range(bb):
        gr = zgr_ref[b]
        for h in range(GLA_HEADS):
            hs = slice(h * GLA_DV, (h + 1) * GLA_DV)
            o = y_s[b, :, hs]
            on = o * lax.rsqrt(jnp.mean(o * o, -1, keepdims=True) + RMS_EPS) * ng_ref[...]
            o_ref[b, :, hs] = (on * _silu(gr[:, hs])).astype(o_ref.dtype)

    @pl.when(c == pl.num_programs(1) - 1)
    def _():
        sout_ref[...] = st_s[...].reshape(sout_ref.shape)


def _gla_call(st, z, s0, wg_p, bg, ng, out_dtype):
    B, T = st.B, st.T
    z3 = z.reshape(B, T, Z_WIDTH)
    bb, tc, sb = _scan_tiling(B, T, GLA_BLOCK)
    npair = GLA_HEADS // 2
    HK = GLA_HEADS * GLA_DK
    s0p = _pack_pairs(s0.transpose(0, 1, 3, 2))
    zspec = lambda w, idx: pl.BlockSpec((bb, tc, w), lambda g, c: (g, c, idx))
    const = lambda a: pl.BlockSpec(a.shape, lambda g, c: (0,) * a.ndim)
    ltri, bones = _block_cumsum_matrices(tc, sb)
    o, sout = pl.pallas_call(
        functools.partial(_gla_body, bb=bb, tc=tc, sb=sb),
        grid=(B // bb, T // tc),
        in_specs=[
            zspec(512, Z_GLA_QK), zspec(512, Z_GLA_V), zspec(512, Z_GLA_GR), zspec(128, Z_GLA_GL),
            pl.BlockSpec((bb, npair, GLA_DV, 128), lambda g, c: (g, 0, 0, 0)),
            const(wg_p), const(bg), const(ng), const(ltri), const(bones),
        ],
        out_specs=[
            pl.BlockSpec((bb, tc, GLA_WIDTH), lambda g, c: (g, c, 0)),
            pl.BlockSpec((bb, npair, GLA_DV, 128), lambda g, c: (g, 0, 0, 0)),
        ],
        out_shape=[
            jax.ShapeDtypeStruct((B, T, GLA_WIDTH), out_dtype),
            jax.ShapeDtypeStruct((B, npair, GLA_DV, 128), F32),
        ],
        scratch_shapes=[
            pltpu.VMEM((bb * npair, GLA_DV, 128), F32),
        ] + [pltpu.VMEM((bb, tc, HK), F32)] * 6 + [pltpu.VMEM((bb, tc, GLA_WIDTH), F32)] * 2,
        compiler_params=_cp(("parallel", "arbitrary")),
        name="gla",
    )(z3, z3, z3, z3, s0p, wg_p, bg, ng, ltri, bones)
    return o.reshape(B * T, GLA_WIDTH), _unpack_pairs(sout).transpose(0, 1, 3, 2)


def _outproj_body(om_ref, or_ref, og_ref, x_ref, g1_ref, sh2_ref, sc2_ref, w_ref, lng_ref, lnb_ref,
                  x1_ref, h2_ref, *, alpha):
    wm = MLA_HEADS * MLA_V
    mix = (jnp.dot(om_ref[...].astype(BF16), w_ref[0:wm, :], preferred_element_type=F32)
           + jnp.dot(or_ref[...].astype(BF16), w_ref[wm:wm + RWKV_WIDTH, :], preferred_element_type=F32)
           + jnp.dot(og_ref[...].astype(BF16), w_ref[wm + RWKV_WIDTH:, :], preferred_element_type=F32))
    x1 = _layernorm(alpha * x_ref[...] + (1.0 + g1_ref[...]) * mix, lng_ref[...], lnb_ref[...])
    x1_ref[...] = x1
    h2_ref[...] = x1 * (1.0 + sc2_ref[...]) + sh2_ref[...]


def _outproj_call(st, o_mla, o_rwkv, o_gla, x2d, l, w_out_b, ln_g, ln_b, alpha):
    N, D = x2d.shape
    bm = st.row_block(256)
    row = lambda w: pl.BlockSpec((bm, w), lambda i: (i, 0))
    return pl.pallas_call(
        functools.partial(_outproj_body, alpha=alpha),
        grid=(N // bm,),
        in_specs=[
            row(o_mla.shape[1]), row(o_rwkv.shape[1]), row(o_gla.shape[1]), row(D),
            st.mod_spec(l, 2, bm, D), st.mod_spec(l, 3, bm, D), st.mod_spec(l, 4, bm, D),
            pl.BlockSpec((None,) + w_out_b.shape[1:], lambda i: (l, 0, 0)),
            pl.BlockSpec((None, 1, D), lambda i: (l, 0, 0)),
            pl.BlockSpec((None, 1, D), lambda i: (l, 0, 0)),
        ],
        out_specs=[row(D), row(D)],
        out_shape=[jax.ShapeDtypeStruct((N, D), F32), jax.ShapeDtypeStruct((N, D), F32)],
        compiler_params=_cp(("parallel",)),
        name="out_proj",
    )(o_mla, o_rwkv, o_gla, x2d, st.mod, st.mod, st.mod, w_out_b, ln_g, ln_b)


def _router_body(x1_ref, sh2_ref, sc2_ref, wr_ref, o_ref):
    h = x1_ref[...] * (1.0 + sc2_ref[...]) + sh2_ref[...]
    logits = jnp.dot(h, wr_ref[...], precision=HIGHEST, preferred_element_type=F32)
    lane = lax.broadcasted_iota(jnp.int32, logits.shape, 1)
    lane_f = lane.astype(F32)
    lg = jnp.where(lane < N_EXPERTS, logits, -jnp.inf)
    v1 = jnp.max(lg, -1, keepdims=True)
    i1 = jnp.min(jnp.where(lg == v1, lane_f, 128.0), -1, keepdims=True)
    lg2 = jnp.where(lane_f == i1, -jnp.inf, lg)
    v2 = jnp.max(lg2, -1, keepdims=True)
    i2 = jnp.min(jnp.where(lg2 == v2, lane_f, 128.0), -1, keepdims=True)
    e = jnp.exp(v2 - v1)
    g1 = 1.0 / (1.0 + e)
    g2 = e * g1
    o_ref[...] = jnp.where(lane == 0, g1, jnp.where(lane == 1, g2, jnp.where(
        lane == 2, i1, jnp.where(lane == 3, i2, 0.0))))


def _router_call(st, x1, l, wr_p):
    N, D = x1.shape
    bm = st.row_block(512)
    return pl.pallas_call(
        _router_body,
        grid=(N // bm,),
        in_specs=[
            pl.BlockSpec((bm, D), lambda i: (i, 0)),
            st.mod_spec(l, 3, bm, D), st.mod_spec(l, 4, bm, D),
            pl.BlockSpec(wr_p.shape, lambda i: (0, 0)),
        ],
        out_specs=pl.BlockSpec((bm, 128), lambda i: (i, 0)),
        out_shape=jax.ShapeDtypeStruct((N, 128), F32),
        compiler_params=_cp(("parallel",)),
        name="moe_router",
    )(x1, st.mod, st.mod, wr_p)


def _ffn_up_body(te_ref, tv_ref, h_ref, w1_ref, w3_ref, o_ref):
    t = pl.program_id(0)

    @pl.when(tv_ref[t] != 0)
    def _():
        h = h_ref[...].astype(BF16)
        a = jnp.dot(h, w1_ref[...].astype(BF16), preferred_element_type=F32)
        b = jnp.dot(h, w3_ref[...].astype(BF16), preferred_element_type=F32)
        o_ref[...] = (_silu(a) * b).astype(BF16)

    @pl.when(tv_ref[t] == 0)
    def _():
        o_ref[...] = jnp.zeros(o_ref.shape, BF16)


def _ffn_down_body(te_ref, tv_ref, g_ref, w2_ref, o_ref):
    t = pl.program_id(0)

    @pl.when(tv_ref[t] != 0)
    def _():
        o_ref[...] = jnp.dot(g_ref[...], w2_ref[...].astype(BF16), preferred_element_type=F32)

    @pl.when(tv_ref[t] == 0)
    def _():
        o_ref[...] = jnp.zeros(o_ref.shape, F32)


def _ffn_up_gather_body(te_ref, tv_ref, src_ref, h_hbm, w1_ref, w3_ref, o_ref, rows_s, hb_s, sem):
    t = pl.program_id(0)
    bm = rows_s.shape[0]

    @pl.when(jnp.logical_and(pl.program_id(1) == 0, tv_ref[t] != 0))
    def _():
        def issue(r, carry):
            pltpu.make_async_copy(h_hbm.at[pl.ds(src_ref[t * bm + r], 1)], rows_s.at[pl.ds(r, 1)], sem).start()
            return carry

        lax.fori_loop(0, bm, issue, 0, unroll=8)
        pltpu.make_async_copy(h_hbm.at[pl.ds(0, bm)], rows_s, sem).wait()
        hb_s[...] = rows_s[...].astype(BF16)

    @pl.when(tv_ref[t] != 0)
    def _():
        h = hb_s[...]
        a = jnp.dot(h, w1_ref[...].astype(BF16), preferred_element_type=F32)
        b = jnp.dot(h, w3_ref[...].astype(BF16), preferred_element_type=F32)
        o_ref[...] = (_silu(a) * b).astype(BF16)

    @pl.when(tv_ref[t] == 0)
    def _():
        o_ref[...] = jnp.zeros(o_ref.shape, BF16)


def _ffn_call(h, tile_expert, tile_valid, w1, w3, w2, bm, src=None):
    D = h.shape[1]
    F = w1.shape[-1]
    R = h.shape[0] if src is None else src.shape[0]
    nt = R // bm
    bf = 512
    if src is None:
        up = pl.pallas_call(
            _ffn_up_body,
            grid_spec=pltpu.PrefetchScalarGridSpec(
                num_scalar_prefetch=2,
                grid=(nt, F // bf),
                in_specs=[
                    pl.BlockSpec((bm, D), lambda t, j, te, tv: (t, 0)),
                    pl.BlockSpec((None, D, bf), lambda t, j, te, tv: (te[t], 0, j)),
                    pl.BlockSpec((None, D, bf), lambda t, j, te, tv: (te[t], 0, j)),
                ],
                out_specs=pl.BlockSpec((bm, bf), lambda t, j, te, tv: (t, j)),
            ),
            out_shape=jax.ShapeDtypeStruct((R, F), BF16),
            compiler_params=_cp(("parallel", "arbitrary")),
            name="ffn_up",
        )(tile_expert, tile_valid, h, w1, w3)
    else:
        up = pl.pallas_call(
            _ffn_up_gather_body,
            grid_spec=pltpu.PrefetchScalarGridSpec(
                num_scalar_prefetch=3,
                grid=(nt, F // bf),
                in_specs=[
                    pl.BlockSpec(memory_space=pl.ANY),
                    pl.BlockSpec((None, D, bf), lambda t, j, te, tv, sr: (te[t], 0, j)),
                    pl.BlockSpec((None, D, bf), lambda t, j, te, tv, sr: (te[t], 0, j)),
                ],
                out_specs=pl.BlockSpec((bm, bf), lambda t, j, te, tv, sr: (t, j)),
                scratch_shapes=[pltpu.VMEM((bm, D), h.dtype), pltpu.VMEM((bm, D), BF16),
                                pltpu.SemaphoreType.DMA(())],
            ),
            out_shape=jax.ShapeDtypeStruct((R, F), BF16),
            compiler_params=_cp(("arbitrary", "arbitrary")),
            name="ffn_up_gather",
        )(tile_expert, tile_valid, src, h, w1, w3)
    bn = 256
    return pl.pallas_call(
        _ffn_down_body,
        grid_spec=pltpu.PrefetchScalarGridSpec(
            num_scalar_prefetch=2,
            grid=(nt, D // bn),
            in_specs=[
                pl.BlockSpec((bm, F), lambda t, n, te, tv: (t, 0)),
                pl.BlockSpec((None, F, bn), lambda t, n, te, tv: (te[t], 0, n)),
            ],
            out_specs=pl.BlockSpec((bm, bn), lambda t, n, te, tv: (t, n)),
        ),
        out_shape=jax.ShapeDtypeStruct((R, D), F32),
        compiler_params=_cp(("parallel", "arbitrary")),
        name="ffn_down",
    )(tile_expert, tile_valid, up, w2)


def _combine_body(*refs, alpha, nterm):
    x1_ref, g2_ref, lng_ref, lnb_ref = refs[:4]
    f_refs = refs[4:4 + nterm]
    o_ref = refs[-1]
    if nterm == 1:
        f = f_refs[0][...]
    else:
        gates = refs[4 + nterm][...]
        f = f_refs[0][...] * gates[:, 0:1] + f_refs[1][...] * gates[:, 1:2]
    o_ref[...] = _layernorm(alpha * x1_ref[...] + (1.0 + g2_ref[...]) * f, lng_ref[...], lnb_ref[...])


def _combine_call(st, x1, l, ln_g, ln_b, alpha, terms, gates=None, row0=0):
    N, D = x1.shape
    bm = st.row_block(512)
    assert row0 % bm == 0
    row = lambda w: pl.BlockSpec((bm, w), lambda i: (i, 0))
    shared = lambda w: pl.BlockSpec((bm, w), lambda i: (i + row0 // bm, 0))
    args = [x1, st.mod, ln_g, ln_b] + list(terms) + ([gates] if gates is not None else [])
    return pl.pallas_call(
        functools.partial(_combine_body, alpha=alpha, nterm=len(terms)),
        grid=(N // bm,),
        in_specs=[
            row(D), st.mod_spec(l, 5, bm, D),
            pl.BlockSpec((None, 1, D), lambda i: (l, 0, 0)),
            pl.BlockSpec((None, 1, D), lambda i: (l, 0, 0)),
        ] + [shared(D)] * len(terms) + ([shared(128)] if gates is not None else []),
        out_specs=row(D),
        out_shape=jax.ShapeDtypeStruct((N, D), F32),
        compiler_params=_cp(("parallel",)),
        name="ffn_residual",
    )(*args)


def _repack_w_in(w_in):
    L, D, _ = w_in.shape
    w = w_in.astype(BF16)
    zeros = lambda n: jnp.zeros((L, D, n), BF16)
    r0 = MLA_IN
    g0 = MLA_IN + RWKV_PROJ
    HK = GLA_HEADS * GLA_DK
    gq, gk, gv = g0, g0 + HK, g0 + 2 * HK
    ggl = gv + GLA_WIDTH
    ggr = ggl + GLA_GATE_LORA
    parts = [
        w[:, :, :MLA_IN], zeros(Z_MLA_W - MLA_IN),
        w[:, :, r0:r0 + RWKV_PROJ],
        w[:, :, ggl:ggr], zeros(256 - GLA_GATE_LORA),
        w[:, :, gq:gv], w[:, :, gv:ggl], w[:, :, ggr:ggr + GLA_WIDTH],
    ]
    out = jnp.concatenate(parts, axis=-1)
    assert out.shape[-1] == Z_WIDTH
    return out


def _moe_plan(gi, bm):
    n = gi.shape[0]
    E = N_EXPERTS
    experts = gi[:, 2:4].astype(jnp.int32).reshape(-1)
    onehot = (experts[:, None] == jnp.arange(E, dtype=jnp.int32)[None, :]).astype(jnp.int32)
    counts = jnp.sum(onehot, axis=0)
    tiles = (counts + bm - 1) // bm
    tile_end = jnp.cumsum(tiles)
    tile_start = tile_end - tiles
    rank = jnp.take_along_axis(jnp.cumsum(onehot, axis=0) - 1, experts[:, None], axis=1)[:, 0]
    pos = tile_start[experts] * bm + rank
    nt = (2 * n + bm - 1) // bm + E
    src = jnp.zeros((nt * bm,), jnp.int32).at[pos].set(jnp.arange(2 * n, dtype=jnp.int32) // 2)
    t_ids = jnp.arange(nt, dtype=jnp.int32)
    tile_expert = jnp.minimum(jnp.sum((t_ids[:, None] >= tile_end[None, :]).astype(jnp.int32), axis=1), E - 1)
    tile_valid = (t_ids < tile_end[-1]).astype(jnp.int32)
    return src, pos.reshape(n, 2), tile_expert, tile_valid


def kernel(x_prompt, x_sample, cache_kv_latent, cache_k_rope, state_rwkv, state_rwkv_shift, state_gla, page_table, c_prompt, c_sample, w_in, w_out, mla_q_norm, mla_kv_norm, mla_w_q_up, mla_w_uk, mla_w_uv, rwkv_mu, rwkv_w0, rwkv_w2, rwkv_a0, rwkv_a2, rwkv_g2, rwkv_k_k, rwkv_k_a, rwkv_r_k, rwkv_lnx_g, rwkv_lnx_b, gla_w_g2, gla_b_g, gla_norm_g, ada_w, ada_b, ln1_g, ln1_b, ln2_g, ln2_b, ffn_w1, ffn_w3, ffn_w2, moe_router, moe_w1, moe_w3, moe_w2):
    Bp, Tp, D = x_prompt.shape
    Bs, Ts, _ = x_sample.shape
    L = w_in.shape[0]
    past_len = page_table.shape[1] * PAGE_SIZE
    alpha = (2 * L) ** 0.25
    dt = x_prompt.dtype

    n_c = Bp + Bs
    c_all = jnp.concatenate([c_prompt, c_sample, jnp.zeros((-n_c % 8, D), F32)], axis=0)
    mod = _ada_call(c_all, ada_w, ada_b)
    st_p = _Stream(Bp, Tp, mod[:, :Bp].reshape(L, Bp, 1, 6 * D), False, 0)
    st_s = _Stream(Bs, Ts, jnp.repeat(mod[:, Bp:Bp + Bs], Ts, axis=1), True, past_len)

    w_in_p = _repack_w_in(w_in)
    w_out_b = w_out.astype(BF16)
    wq = mla_w_q_up.astype(BF16)
    wq_p = jnp.concatenate([wq[..., :MLA_NOPE].reshape(L, MLA_Q_RANK, -1),
                            wq[..., MLA_NOPE:].reshape(L, MLA_Q_RANK, -1)], axis=-1)
    wuk_p = mla_w_uk.astype(BF16).transpose(0, 2, 3, 1)
    wuv_p = mla_w_uv.astype(BF16).transpose(0, 2, 1, 3)
    inv = 1.0 / (ROPE_BASE ** (jnp.arange(0, MLA_ROPE, 2, dtype=F32) / MLA_ROPE))
    inv128 = jnp.tile(inv, 4).reshape(1, 128)
    zpad = lambda a, n: jnp.concatenate([a, jnp.zeros((L, n) + a.shape[2:], a.dtype)], axis=1)
    zpre = lambda a, n: jnp.concatenate([jnp.zeros((L, n) + a.shape[2:], a.dtype), a], axis=1)
    rw_w2 = zpad(rwkv_w2.astype(BF16), RWKV_A_LORA)
    rw_a2 = zpre(rwkv_a2.astype(BF16), RWKV_W_LORA)
    gla_wg = zpad(gla_w_g2.astype(BF16), 128 - GLA_GATE_LORA)
    row1 = lambda a: a.reshape(1, -1)

    xp = x_prompt.reshape(Bp * Tp, D)
    xs = x_sample.reshape(Bs * Ts, D)
    zeros_shift = jnp.zeros((Bp, RWKV_PROJ), dt)
    zeros_rwkv = jnp.zeros((Bp, RWKV_HEADS, RWKV_HEAD, RWKV_HEAD), F32)
    zeros_gla = jnp.zeros((Bp, GLA_HEADS, GLA_DK, GLA_DV), F32)
    outs_p = [[] for _ in range(5)]
    outs_s = [[] for _ in range(5)]

    for l in range(L):
        rw = dict(mu=row1(rwkv_mu[l]), w0=row1(rwkv_w0[l]), w2=rw_w2[l], a0=row1(rwkv_a0[l]), a2=rw_a2[l],
                  g2=rwkv_g2[l].astype(BF16), k_k=row1(rwkv_k_k[l]), k_a=row1(rwkv_k_a[l]),
                  r_k=row1(rwkv_r_k[l]), lnx_g=row1(rwkv_lnx_g[l]), lnx_b=row1(rwkv_lnx_b[l]))
        streams = []
        for st, x2d, sample in ((st_p, xp, False), (st_s, xs, True)):
            z = _inproj_call(st, x2d, l, w_in_p)
            qlat, qpe, lat, kpe = _mla_prep_call(st, z, row1(mla_q_norm[l]), row1(mla_kv_norm[l]),
                                                 wq_p[l], wuk_p[l], inv128, F32 if sample else BF16)
            if sample:
                o_mla = _attn_sample_call(st, l, qlat, qpe, lat, kpe, cache_kv_latent, cache_k_rope,
                                          page_table, wuv_p[l])
                shift0, s_r0, s_g0 = state_rwkv_shift[l], state_rwkv[l], state_gla[l]
            else:
                o_mla = _attn_prompt_call(st, qlat, qpe, lat, kpe, wuv_p[l])
                shift0, s_r0, s_g0 = zeros_shift, zeros_rwkv, zeros_gla
            o_dt = F32 if sample else BF16
            o_rwkv, s_r = _rwkv_call(st, z, shift0, s_r0, rw, o_dt)
            o_gla, s_g = _gla_call(st, z, s_g0, gla_wg[l], row1(gla_b_g[l]), row1(gla_norm_g[l]), o_dt)
            x1, h2 = _outproj_call(st, o_mla, o_rwkv, o_gla, x2d, l, w_out_b, ln1_g.reshape(L, 1, D),
                                   ln1_b.reshape(L, 1, D), alpha)
            shift = z.reshape(st.B, st.T, Z_WIDTH)[:, -1, Z_MLA_W:Z_MLA_W + RWKV_PROJ]
            acc = outs_s if sample else outs_p
            for lst, val in zip(acc, (lat.reshape(st.B, st.T, -1), kpe.reshape(st.B, st.T, -1),
                                      s_r.astype(dt), shift, s_g.astype(dt))):
                lst.append(val)
            streams.append((st, x1, h2))

        (_, x1p, h2p), (_, x1s, h2s) = streams
        h_all = jnp.concatenate([h2p, h2s], axis=0)
        n_all = h_all.shape[0]
        n_p = h2p.shape[0]
        bm = _pick(n_all, 1024)
        lng, lnb = ln2_g.reshape(L, 1, D), ln2_b.reshape(L, 1, D)
        if l % 2 == 0:
            e = l // 2
            nt = n_all // bm
            f = _ffn_call(h_all, jnp.zeros((nt,), jnp.int32), jnp.ones((nt,), jnp.int32),
                          ffn_w1[e:e + 1], ffn_w3[e:e + 1], ffn_w2[e:e + 1], bm)
            xp = _combine_call(st_p, x1p, l, lng, lnb, alpha, [f])
            xs = _combine_call(st_s, x1s, l, lng, lnb, alpha, [f], row0=n_p)
        else:
            e = l // 2
            wr_p = jnp.concatenate([moe_router[e], jnp.zeros((D, 128 - N_EXPERTS), F32)], axis=1)
            gi = jnp.concatenate([_router_call(st_p, x1p, l, wr_p), _router_call(st_s, x1s, l, wr_p)], axis=0)
            src, pos, tile_expert, tile_valid = _moe_plan(gi, bm)
            f_sorted = _ffn_call(h_all, tile_expert, tile_valid, moe_w1[e], moe_w3[e], moe_w2[e], bm, src=src)
            f0 = f_sorted[pos[:, 0]]
            f1 = f_sorted[pos[:, 1]]
            xp = _combine_call(st_p, x1p, l, lng, lnb, alpha, [f0, f1], gi)
            xs = _combine_call(st_s, x1s, l, lng, lnb, alpha, [f0, f1], gi, row0=n_p)

    stack = lambda lst: jnp.stack(lst)
    return (xp.reshape(Bp, Tp, D), xs.reshape(Bs, Ts, D),
            *[stack(v) for v in outs_p], *[stack(v) for v in outs_s])
```

```python
import functools

import jax
import jax.numpy as jnp
from jax import lax
from jax.experimental import pallas as pl
from jax.experimental.pallas import tpu as pltpu

F32 = jnp.float32
BF16 = jnp.bfloat16
HIGHEST = lax.Precision.HIGHEST

PAGE_SIZE = 128
MLA_HEADS = 8
MLA_NOPE = 128
MLA_ROPE = 64
MLA_V = 128
MLA_Q_RANK = 512
MLA_KV_RANK = 256
MLA_IN = MLA_Q_RANK + MLA_KV_RANK + MLA_ROPE
MLA_SCALE = (MLA_NOPE + MLA_ROPE) ** -0.5
ROPE_BASE = 10000.0
RWKV_HEADS = 8
RWKV_HEAD = 64
RWKV_WIDTH = RWKV_HEADS * RWKV_HEAD
RWKV_W_LORA = 64
RWKV_A_LORA = 64
RWKV_G_LORA = 128
RWKV_PROJ = 3 * RWKV_WIDTH + RWKV_W_LORA + RWKV_A_LORA + RWKV_G_LORA
RWKV_GN_EPS = 64e-5
GLA_HEADS = 4
GLA_DK = 64
GLA_DV = 128
GLA_WIDTH = GLA_HEADS * GLA_DV
GLA_GATE_LORA = 16
GLA_TAU = 16.0
GLA_PROJ = 2 * GLA_HEADS * GLA_DK + 2 * GLA_WIDTH + GLA_GATE_LORA
N_EXPERTS = 8
LN_EPS = 1e-5
RMS_EPS = 1e-6

Z_WIDTH = 4608
Z_MLA_W = 1024
Z_RWKV_R, Z_RWKV_K, Z_RWKV_V = 2, 3, 4
Z_RWKV_LORA = 10
Z_GLA_GL = 22
Z_GLA_QK, Z_GLA_V, Z_GLA_GR = 6, 7, 8

RWKV_BLOCK = 16
GLA_BLOCK = 16
SCAN_BATCH = 4

VMEM_LIMIT_MB = 56


def _cp(sem, vmem_mb=VMEM_LIMIT_MB):
    return pltpu.CompilerParams(dimension_semantics=sem, vmem_limit_bytes=vmem_mb * 2**20)


def _pick(n, pref):
    if n <= pref:
        return n
    b = pref - pref % 8
    while b >= 8:
        if n % b == 0:
            return b
        b -= 8
    return n


def _silu(x):
    return x * jax.nn.sigmoid(x)


def _softplus(u):
    return jnp.maximum(u, 0.0) + jnp.log(1.0 + jnp.exp(-jnp.abs(u)))


def _lane_group_sum(x, group):
    axis = x.ndim - 1
    width = x.shape[axis]
    lane = lax.broadcasted_iota(jnp.int32, x.shape, axis)
    s = 1
    while s < group:
        partner = jnp.where(jnp.bitwise_and(lane, s) == 0,
                            pltpu.roll(x, width - s, axis), pltpu.roll(x, s, axis))
        x = x + partner
        s *= 2
    return x


def _layernorm(y, g, b):
    mu = jnp.mean(y, -1, keepdims=True)
    d = y - mu
    var = jnp.mean(d * d, -1, keepdims=True)
    return d * lax.rsqrt(var + LN_EPS) * g + b


def _ada_body(c_ref, w_ref, b_ref, o_ref):
    c = c_ref[...]
    s = _silu(c).astype(BF16)
    o_ref[...] = jnp.dot(s, w_ref[...].astype(BF16), preferred_element_type=F32) + b_ref[...]


def _ada_call(c_all, ada_w, ada_b):
    L, D, N6 = ada_w.shape
    Mc = c_all.shape[0]
    bn = 1024
    return pl.pallas_call(
        _ada_body,
        grid=(L, N6 // bn),
        in_specs=[
            pl.BlockSpec((Mc, D), lambda l, j: (0, 0)),
            pl.BlockSpec((None, D, bn), lambda l, j: (l, 0, j)),
            pl.BlockSpec((None, 1, bn), lambda l, j: (l, 0, j)),
        ],
        out_specs=pl.BlockSpec((None, Mc, bn), lambda l, j: (l, 0, j)),
        out_shape=jax.ShapeDtypeStruct((L, Mc, N6), F32),
        compiler_params=_cp(("parallel", "parallel")),
        name="ada_mod",
    )(c_all, ada_w, ada_b.reshape(L, 1, N6))


class _Stream:
    def __init__(self, B, T, mod, per_row, pos_base):
        self.B, self.T, self.N = B, T, B * T
        self.mod = mod
        self.per_row = per_row
        self.pos_base = pos_base

    def mod_spec(self, l, k, bm, D):
        if self.per_row:
            return pl.BlockSpec((None, bm, D), lambda i, *_: (l, i, k))
        nb = self.T // bm
        return pl.BlockSpec((None, None, 1, D), lambda i, *_: (l, i // nb, 0, k))

    def row_block(self, pref):
        return _pick(self.N if self.per_row else self.T, pref)


def _inproj_body(x_ref, sh_ref, sc_ref, w_ref, o_ref, h_s):
    @pl.when(pl.program_id(1) == 0)
    def _():
        h_s[...] = (x_ref[...] * (1.0 + sc_ref[...]) + sh_ref[...]).astype(BF16)

    o_ref[...] = jnp.dot(h_s[...], w_ref[...], preferred_element_type=F32)


def _inproj_call(st, x2d, l, w_in_p):
    N, D = x2d.shape
    Wz = w_in_p.shape[-1]
    bm = st.row_block(1024)
    bn = 512
    return pl.pallas_call(
        _inproj_body,
        grid=(N // bm, Wz // bn),
        in_specs=[
            pl.BlockSpec((bm, D), lambda i, j: (i, 0)),
            st.mod_spec(l, 0, bm, D),
            st.mod_spec(l, 1, bm, D),
            pl.BlockSpec((None, D, bn), lambda i, j: (l, 0, j)),
        ],
        out_specs=pl.BlockSpec((bm, bn), lambda i, j: (i, j)),
        out_shape=jax.ShapeDtypeStruct((N, Wz), F32),
        scratch_shapes=[pltpu.VMEM((bm, D), BF16)],
        compiler_params=_cp(("parallel", "arbitrary")),
        name="in_proj",
    )(x2d, st.mod, st.mod, w_in_p)


def _rope(x, cos, sin_signed, first_half):
    w = x.shape[1]
    swapped = jnp.where(first_half, pltpu.roll(x, w - 32, 1), pltpu.roll(x, 32, 1))
    return x * cos + swapped * sin_signed


def _mla_prep_body(z_ref, gq_ref, gkv_ref, wq_ref, wuk_ref, inv_ref, qlat_ref, qpe_ref, lat_ref, kpe_ref,
                   *, bm, period, pos_base):
    i = pl.program_id(0)
    z = z_ref[...]
    zq = z[:, :MLA_Q_RANK]
    zkv = z[:, MLA_Q_RANK:MLA_Q_RANK + MLA_KV_RANK]
    zpe = z[:, MLA_Q_RANK + MLA_KV_RANK:MLA_Q_RANK + MLA_KV_RANK + 128]
    qn = zq * lax.rsqrt(jnp.mean(zq * zq, -1, keepdims=True) + RMS_EPS) * gq_ref[...]
    q = jnp.dot(qn.astype(BF16), wq_ref[...], preferred_element_type=F32)
    lat_ref[...] = zkv * lax.rsqrt(jnp.mean(zkv * zkv, -1, keepdims=True) + RMS_EPS) * gkv_ref[...]

    row = lax.broadcasted_iota(jnp.int32, (bm, 128), 0) + i * bm
    pos = (pos_base + jnp.bitwise_and(row, period - 1)).astype(F32)
    ang = pos * inv_ref[...]
    cos = jnp.cos(ang)
    sin = jnp.sin(ang)
    first = jnp.bitwise_and(lax.broadcasted_iota(jnp.int32, (bm, 128), 1), 63) < 32
    sin_s = jnp.where(first, -sin, sin)
    kpe_ref[...] = _rope(zpe, cos, sin_s, first)[:, :MLA_ROPE]

    npe = MLA_HEADS * MLA_ROPE // 128
    cos4 = jnp.concatenate([cos] * npe, axis=1)
    sin4 = jnp.concatenate([sin_s] * npe, axis=1)
    first4 = jnp.bitwise_and(lax.broadcasted_iota(jnp.int32, (bm, 128 * npe), 1), 63) < 32
    q_pe = q[:, MLA_HEADS * MLA_NOPE:]
    qpe_ref[...] = (_rope(q_pe, cos4, sin4, first4) * MLA_SCALE).astype(qpe_ref.dtype)
    for h in range(MLA_HEADS):
        qh = q[:, h * MLA_NOPE:(h + 1) * MLA_NOPE].astype(BF16)
        ql = jnp.dot(qh, wuk_ref[h], preferred_element_type=F32) * MLA_SCALE
        qlat_ref[:, h * MLA_KV_RANK:(h + 1) * MLA_KV_RANK] = ql.astype(qlat_ref.dtype)


def _mla_prep_call(st, z, gq, gkv, wq_p, wuk_p, inv128, q_dtype):
    N = z.shape[0]
    bm = _pick(N, 512)
    HR = MLA_HEADS * MLA_KV_RANK
    HP = MLA_HEADS * MLA_ROPE
    assert st.T & (st.T - 1) == 0
    body = functools.partial(_mla_prep_body, bm=bm, period=st.T, pos_base=st.pos_base)
    const = lambda shape: pl.BlockSpec(shape, lambda i: (0,) * len(shape))
    return pl.pallas_call(
        body,
        grid=(N // bm,),
        in_specs=[
            pl.BlockSpec((bm, Z_MLA_W), lambda i: (i, 0)),
            const((1, MLA_Q_RANK)), const((1, MLA_KV_RANK)),
            const(wq_p.shape), const(wuk_p.shape), const((1, 128)),
        ],
        out_specs=[
            pl.BlockSpec((bm, HR), lambda i: (i, 0)),
            pl.BlockSpec((bm, HP), lambda i: (i, 0)),
            pl.BlockSpec((bm, MLA_KV_RANK), lambda i: (i, 0)),
            pl.BlockSpec((bm, MLA_ROPE), lambda i: (i, 0)),
        ],
        out_shape=[
            jax.ShapeDtypeStruct((N, HR), q_dtype),
            jax.ShapeDtypeStruct((N, HP), q_dtype),
            jax.ShapeDtypeStruct((N, MLA_KV_RANK), F32),
            jax.ShapeDtypeStruct((N, MLA_ROPE), F32),
        ],
        compiler_params=_cp(("parallel",)),
        name="mla_prep",
    )(z, gq, gkv, wq_p, wuk_p, inv128)


_NT = (((1,), (1,)), ((), ()))


def _softmax_update(s, h, m_s, l_s, acc_s, values_bf16):
    m_prev = m_s[h]
    m_new = jnp.maximum(m_prev, jnp.max(s, -1, keepdims=True))
    alpha = jnp.exp(m_prev - m_new)
    p = jnp.exp(s - m_new[:, :1])
    l_s[h] = alpha * l_s[h] + jnp.sum(p, -1, keepdims=True)
    acc_s[h] = acc_s[h] * alpha[:, :1] + jnp.dot(p.astype(BF16), values_bf16, preferred_element_type=F32)
    m_s[h] = m_new


def _attn_prompt_body(ql_ref, qp_ref, lat_ref, kpe_ref, wuv_ref, o_ref, m_s, l_s, acc_s, *, bq):
    qi = pl.program_id(1)
    ki = pl.program_id(2)

    @pl.when(ki == 0)
    def _():
        m_s[...] = jnp.full(m_s.shape, -jnp.inf, F32)
        l_s[...] = jnp.zeros(l_s.shape, F32)
        acc_s[...] = jnp.zeros(acc_s.shape, F32)

    def step(masked):
        latb = lat_ref[...].astype(BF16)
        kpb = kpe_ref[...].astype(BF16)
        if masked:
            causal = (lax.broadcasted_iota(jnp.int32, (bq, bq), 0)
                      >= lax.broadcasted_iota(jnp.int32, (bq, bq), 1))
        for h in range(MLA_HEADS):
            s = (lax.dot_general(ql_ref[:, h * MLA_KV_RANK:(h + 1) * MLA_KV_RANK], latb, _NT,
                                 preferred_element_type=F32)
                 + lax.dot_general(qp_ref[:, h * MLA_ROPE:(h + 1) * MLA_ROPE], kpb, _NT,
                                   preferred_element_type=F32))
            if masked:
                s = jnp.where(causal, s, -jnp.inf)
            _softmax_update(s, h, m_s, l_s, acc_s, latb)

    @pl.when(ki < qi)
    def _():
        step(False)

    @pl.when(ki == qi)
    def _():
        step(True)
        for h in range(MLA_HEADS):
            o_lat = (acc_s[h] / l_s[h][:, :1]).astype(BF16)
            o_ref[:, h * MLA_V:(h + 1) * MLA_V] = jnp.dot(
                o_lat, wuv_ref[h], preferred_element_type=F32).astype(o_ref.dtype)


def _attn_prompt_call(st, qlat, qpe, lat, kpe, wuv_p):
    B, T = st.B, st.T
    bq = _pick(T, 512)
    nq = T // bq
    HR = MLA_HEADS * MLA_KV_RANK
    HP = MLA_HEADS * MLA_ROPE
    kv = lambda b, qi, ki: (b * nq + jnp.minimum(ki, qi), 0)
    return pl.pallas_call(
        functools.partial(_attn_prompt_body, bq=bq),
        grid=(B, nq, nq),
        in_specs=[
            pl.BlockSpec((bq, HR), lambda b, qi, ki: (b * nq + qi, 0)),
            pl.BlockSpec((bq, HP), lambda b, qi, ki: (b * nq + qi, 0)),
            pl.BlockSpec((bq, MLA_KV_RANK), kv),
            pl.BlockSpec((bq, MLA_ROPE), kv),
            pl.BlockSpec(wuv_p.shape, lambda b, qi, ki: (0, 0, 0)),
        ],
        out_specs=pl.BlockSpec((bq, MLA_HEADS * MLA_V), lambda b, qi, ki: (b * nq + qi, 0)),
        out_shape=jax.ShapeDtypeStruct((B * T, MLA_HEADS * MLA_V), BF16),
        scratch_shapes=[
            pltpu.VMEM((MLA_HEADS, bq, 128), F32),
            pltpu.VMEM((MLA_HEADS, bq, 128), F32),
            pltpu.VMEM((MLA_HEADS, bq, MLA_KV_RANK), F32),
        ],
        compiler_params=_cp(("parallel", "parallel", "arbitrary")),
        name="attn_prompt",
    )(qlat, qpe, lat, kpe, wuv_p)


def _attn_sample_body(pt_ref, ql_ref, qp_ref, latn_ref, kpen_ref, wuv_ref, cache_lat, cache_pe_t, o_ref,
                      lat_buf, pe_buf, sem, *, layer, npages, ts):
    s_id = pl.program_id(0)
    slot = jnp.bitwise_and(s_id, 1)
    rows = MLA_HEADS * ts

    def start_pages(seq, slot_):
        def body(g, carry):
            page = pt_ref[seq, g]
            pltpu.make_async_copy(cache_lat.at[layer, page], lat_buf.at[slot_, g], sem.at[slot_]).start()
            pltpu.make_async_copy(cache_pe_t.at[layer, page], pe_buf.at[slot_, g], sem.at[slot_]).start()
            return carry

        lax.fori_loop(0, npages, body, 0, unroll=4)

    @pl.when(s_id == 0)
    def _():
        start_pages(0, 0)

    pltpu.make_async_copy(cache_lat.at[layer, pl.ds(0, npages)], lat_buf.at[slot], sem.at[slot]).wait()
    pltpu.make_async_copy(cache_pe_t.at[layer, pl.ds(0, npages)], pe_buf.at[slot], sem.at[slot]).wait()

    @pl.when(s_id + 1 < pl.num_programs(0))
    def _():
        start_pages(s_id + 1, 1 - slot)

    qlb = jnp.concatenate([ql_ref[:, h * MLA_KV_RANK:(h + 1) * MLA_KV_RANK] for h in range(MLA_HEADS)],
                          axis=0).astype(BF16)
    qpb = jnp.concatenate([qp_ref[:, h * MLA_ROPE:(h + 1) * MLA_ROPE] for h in range(MLA_HEADS)],
                          axis=0).astype(BF16)

    def scores(latb, kpe_t):
        return (lax.dot_general(qlb, latb, _NT, preferred_element_type=F32)
                + jnp.dot(qpb, kpe_t, preferred_element_type=F32))

    lats = [lat_buf[slot, g].astype(BF16) for g in range(npages)]
    s_pages = [scores(lats[g], pe_buf[slot, g].astype(BF16)) for g in range(npages)]
    pad = PAGE_SIZE - ts
    lat_new = jnp.concatenate([latn_ref[...], jnp.zeros((pad, MLA_KV_RANK), F32)], axis=0).astype(BF16)
    kpe_new = jnp.concatenate([kpen_ref[...], jnp.zeros((pad, MLA_ROPE), F32)], axis=0).T.astype(BF16)
    t_q = jnp.bitwise_and(lax.broadcasted_iota(jnp.int32, (rows, PAGE_SIZE), 0), ts - 1)
    t_k = lax.broadcasted_iota(jnp.int32, (rows, PAGE_SIZE), 1)
    s_new = jnp.where(t_k <= t_q, scores(lat_new, kpe_new), -jnp.inf)
    s_all = jnp.concatenate(s_pages + [s_new], axis=1)
    p = jnp.exp(s_all - jnp.max(s_all, -1, keepdims=True))
    denom = jnp.sum(p, -1, keepdims=True)
    pv = jnp.zeros((rows, MLA_KV_RANK), F32)
    for g, v in enumerate(lats + [lat_new]):
        pv = pv + jnp.dot(p[:, g * PAGE_SIZE:(g + 1) * PAGE_SIZE].astype(BF16), v, preferred_element_type=F32)
    o_lat = (pv / denom).astype(BF16)
    for h in range(MLA_HEADS):
        o_h = jnp.dot(o_lat, wuv_ref[h], preferred_element_type=F32)
        o_ref[:, h * MLA_V:(h + 1) * MLA_V] = o_h[h * ts:(h + 1) * ts, :].astype(o_ref.dtype)


def _attn_sample_call(st, l, qlat, qpe, lat, kpe, cache_lat, cache_pe, page_table, wuv_p):
    S, ts = st.B, st.T
    P = page_table.shape[1]
    HR = MLA_HEADS * MLA_KV_RANK
    HP = MLA_HEADS * MLA_ROPE
    assert ts & (ts - 1) == 0
    page_bytes = PAGE_SIZE * (MLA_KV_RANK + MLA_ROPE) * 4
    assert 2 * P * page_bytes <= 32 * 2**20, "the whole past of one sequence must fit the two page slots"
    cache_pe_t = jnp.swapaxes(cache_pe, 2, 3)
    grid_spec = pltpu.PrefetchScalarGridSpec(
        num_scalar_prefetch=1,
        grid=(S,),
        in_specs=[
            pl.BlockSpec((ts, HR), lambda s, pt: (s, 0)),
            pl.BlockSpec((ts, HP), lambda s, pt: (s, 0)),
            pl.BlockSpec((ts, MLA_KV_RANK), lambda s, pt: (s, 0)),
            pl.BlockSpec((ts, MLA_ROPE), lambda s, pt: (s, 0)),
            pl.BlockSpec(wuv_p.shape, lambda s, pt: (0, 0, 0)),
            pl.BlockSpec(memory_space=pl.ANY),
            pl.BlockSpec(memory_space=pl.ANY),
        ],
        out_specs=pl.BlockSpec((ts, MLA_HEADS * MLA_V), lambda s, pt: (s, 0)),
        scratch_shapes=[
            pltpu.VMEM((2, P, PAGE_SIZE, MLA_KV_RANK), F32),
            pltpu.VMEM((2, P, MLA_ROPE, PAGE_SIZE), F32),
            pltpu.SemaphoreType.DMA((2,)),
        ],
    )
    return pl.pallas_call(
        functools.partial(_attn_sample_body, layer=l, npages=P, ts=ts),
        grid_spec=grid_spec,
        out_shape=jax.ShapeDtypeStruct((S * ts, MLA_HEADS * MLA_V), F32),
        compiler_params=_cp(("arbitrary",)),
        name="attn_sample",
    )(page_table, qlat, qpe, lat, kpe, wuv_p, cache_lat, cache_pe_t)


def _scan_tiling(B, T, block):
    bb = min(B, SCAN_BATCH)
    sb = block if T % block == 0 else 8
    assert B % bb == 0 and T % sb == 0
    return bb, _pick(T, 256), sb


def _block_cumsum_matrices(tc, sb):
    t = jnp.arange(tc, dtype=jnp.int32)
    same = (t[:, None] // sb) == (t[None, :] // sb)
    return (jnp.logical_and(same, t[None, :] <= t[:, None]).astype(F32), same.astype(F32))


def _rwkv_body(zr_ref, zk_ref, zv_ref, zl_ref, sh0_ref, s0_ref, mu_ref, w0_ref, w2_ref, a0_ref, a2_ref,
               g2_ref, kk_ref, ka_ref, rk_ref, lng_ref, lnb_ref, blk_ref, ltri_ref, bones_ref, o_ref, sout_ref,
               carry_s, st_s, r_s, k_s, v_s, g_s, ae_s, re_s, bq_s, kq_s, bh_s, kh_s, gb_s, y_s, *, bb, tc, sb):
    c = pl.program_id(1)
    W = RWKV_WIDTH
    npair = RWKV_HEADS // 2

    @pl.when(c == 0)
    def _():
        for b in range(bb):
            carry_s[b] = jnp.broadcast_to(sh0_ref[b], (8, RWKV_PROJ))
        st_s[...] = s0_ref[...].reshape(st_s.shape)

    first_row = lax.broadcasted_iota(jnp.int32, (tc, 1), 0) == 0
    def head_sum(x):
        if x.shape[0] <= 64:
            return _lane_group_sum(x, RWKV_HEAD)
        hi = x.astype(BF16)
        lo = (x - hi.astype(F32)).astype(BF16)
        return (jnp.dot(hi, blk_ref[...], preferred_element_type=F32)
                + jnp.dot(lo, blk_ref[...], preferred_element_type=F32))

    def mix(z, off):
        prev = jnp.where(first_row, carry_s[b][0:1, off:off + z.shape[1]], pltpu.roll(z, 1, 0))
        return z + (prev - z) * mu_ref[:, off:off + z.shape[1]]

    for b in range(bb):
        zr, zk, zv, zl = zr_ref[b], zk_ref[b], zv_ref[b], zl_ref[b]
        r = mix(zr, 0)
        k = mix(zk, W)
        v = mix(zv, 2 * W)
        lo = mix(zl, 3 * W)
        carry_s[b] = jnp.concatenate([zr[tc - 8:], zk[tc - 8:], zv[tc - 8:], zl[tc - 8:]], axis=1)[7:8] \
            + jnp.zeros((8, RWKV_PROJ), F32)
        wa = lo[:, :128]
        lw = jnp.dot(jnp.tanh(wa).astype(BF16), w2_ref[...], preferred_element_type=F32)
        la = jnp.dot(wa.astype(BF16), a2_ref[...], preferred_element_type=F32)
        g = jnp.dot(jax.nn.sigmoid(lo[:, 128:]).astype(BF16), g2_ref[...], preferred_element_type=F32)
        log_w = -jnp.exp(-_softplus(-(w0_ref[...] + lw)) - 0.5)
        a = jax.nn.sigmoid(a0_ref[...] + la)
        kk = k * kk_ref[...]
        kk = kk / jnp.maximum(jnp.sqrt(head_sum(kk * kk)), 1e-12)
        k = k * (1.0 + (a - 1.0) * ka_ref[...])
        cw = jnp.dot(ltri_ref[...], log_w, precision=HIGHEST, preferred_element_type=F32)
        cl = jnp.dot(bones_ref[...], log_w, precision=HIGHEST, preferred_element_type=F32)
        inv_g = jnp.exp(-cw)
        to_end = jnp.exp(cl - cw)
        r_s[b], k_s[b], v_s[b], g_s[b] = r, k, v, g
        ae_s[b] = -kk * jnp.exp(cw - log_w)
        re_s[b] = r * jnp.exp(cw)
        bq_s[b] = kk * a * inv_g
        kq_s[b] = k * inv_g
        bh_s[b] = kk * a * to_end
        kh_s[b] = k * to_end
        gb_s[b] = jnp.exp(cl)

    lo_half = lax.broadcasted_iota(jnp.int32, (sb, 128), 1) < RWKV_HEAD
    lo_state = lax.broadcasted_iota(jnp.int32, (RWKV_HEAD, 128), 1) < RWKV_HEAD
    pairs = [(b, p) for b in range(bb) for p in range(npair)]
    npairs = len(pairs)
    crow = lax.broadcasted_iota(jnp.int32, (npairs * 4 * sb, npairs * 2 * sb), 0)
    ccol = lax.broadcasted_iota(jnp.int32, (npairs * 4 * sb, npairs * 2 * sb), 1)
    same_pair = lax.shift_right_logical(crow, (4 * sb).bit_length() - 1) \
        == lax.shift_right_logical(ccol, (2 * sb).bit_length() - 1)
    r_side = jnp.where(jnp.bitwise_and(crow, 2 * sb) != 0, 1, 0)
    causal = jnp.bitwise_and(crow, sb - 1) > jnp.bitwise_and(ccol, sb - 1) - r_side
    coef_mask = jnp.logical_and(same_pair, causal)
    lo_of = lambda x: jnp.where(lo_half, x, 0.0)
    hi_of = lambda x: jnp.where(lo_half, 0.0, x)
    zeros_sb = jnp.zeros((sb, 128), F32)
    zero_tile = jnp.zeros((2 * sb, 128), BF16)
    cat = jnp.concatenate

    def block(bi, carry):
        t0 = pl.multiple_of(bi * sb, sb)
        tile = lambda ref, i: ref[pairs[i][0], pl.ds(t0, sb), 128 * pairs[i][1]:128 * (pairs[i][1] + 1)]
        rng = range(npairs)
        aes, res, vbs = ([tile(ref, i) for i in rng] for ref in (ae_s, re_s, v_s))
        lhs = cat([cat([lo_of(aes[i]), hi_of(aes[i]), lo_of(res[i]), hi_of(res[i])], axis=0) for i in rng], axis=0)
        rhs = cat([cat([tile(bq_s, i), tile(kq_s, i)], axis=0) for i in rng], axis=0)
        coef = jnp.where(coef_mask, lax.dot_general(lhs.astype(BF16), rhs.astype(BF16), _NT,
                                                    preferred_element_type=F32), 0.0)
        rows_of = lambda k: cat([coef[(4 * i + k) * sb:(4 * i + k + 1) * sb] for i in rng], axis=0)
        v_rows = cat([cat([zeros_sb, lo_of(v)], axis=0) for v in vbs] + [cat([zeros_sb, hi_of(v)], axis=0) for v in vbs],
                     axis=0).astype(BF16)
        x_v = jnp.dot(cat([rows_of(0), rows_of(1)], axis=1).astype(BF16), v_rows, preferred_element_type=F32)
        states = [st_s[b * npair + p] for b, p in pairs]
        s_bd = cat([cat([jnp.where(lo_state, S, 0.0), jnp.where(lo_state, 0.0, S)], axis=0).astype(BF16)
                    for S in states], axis=1)
        ar_bd = cat([cat([zero_tile] * i + [cat([aes[i], res[i]], axis=0).astype(BF16)]
                         + [zero_tile] * (npairs - 1 - i), axis=1) for i in rng], axis=0)
        xy = lax.dot_general(ar_bd, s_bd, _NT, preferred_element_type=F32)
        xs = [xy[2 * sb * i:2 * sb * i + sb] + x_v[sb * i:sb * (i + 1)] for i in rng]
        cols = [[jnp.where(lo_half,
                           coef[4 * sb * i:4 * sb * i + sb, 2 * sb * i + u:2 * sb * i + u + 1],
                           coef[4 * sb * i + sb:4 * sb * i + 2 * sb, 2 * sb * i + u:2 * sb * i + u + 1])
                 for u in range(sb - 1)] for i in rng]
        for u in range(sb - 1):
            xs = [x + cols[i][u] * x[u:u + 1, :] for i, x in enumerate(xs)]
        pv_rows = cat([cat([lo_of(xs[i]), lo_of(vbs[i])], axis=0) for i in rng]
                      + [cat([hi_of(xs[i]), hi_of(vbs[i])], axis=0) for i in rng], axis=0).astype(BF16)
        y_add = jnp.dot(cat([rows_of(2), rows_of(3)], axis=1).astype(BF16), pv_rows, preferred_element_type=F32)
        for i, (b, p) in enumerate(pairs):
            x, vb = xs[i], vbs[i]
            y_s[b, pl.ds(t0, sb), 128 * p:128 * (p + 1)] = (xy[2 * sb * i + sb:2 * sb * (i + 1)]
                                                            + y_add[sb * i:sb * (i + 1)])
            upd = lax.dot_general(jnp.concatenate([x, vb], axis=0).astype(BF16),
                                  jnp.concatenate([tile(bh_s, i), tile(kh_s, i)], axis=0).astype(BF16),
                                  (((0,), (0,)), ((), ())), preferred_element_type=F32)
            st_s[b * npair + p] = (states[i] * tile(gb_s, i)[0:1, :]
                                   + jnp.where(lo_state, upd[0:RWKV_HEAD], upd[RWKV_HEAD:]))
        return carry

    lax.fori_loop(0, tc // sb, block, 0)

    for b in range(bb):
        y = y_s[b]
        mean = head_sum(y) * (1.0 / RWKV_HEAD)
        d = y - mean
        var = head_sum(d * d) * (1.0 / RWKV_HEAD)
        yn = d * lax.rsqrt(var + RWKV_GN_EPS) * lng_ref[...] + lnb_ref[...]
        bonus = head_sum(r_s[b] * k_s[b] * rk_ref[...]) * v_s[b]
        o_ref[b] = ((yn + bonus) * g_s[b]).astype(o_ref.dtype)

    @pl.when(c == pl.num_programs(1) - 1)
    def _():
        sout_ref[...] = st_s[...].reshape(sout_ref.shape)


def _pack_pairs(s):
    B, H, R, C = s.shape
    return s.reshape(B, H // 2, 2, R, C).transpose(0, 1, 3, 2, 4).reshape(B, H // 2, R, 2 * C)


def _unpack_pairs(s):
    B, HP, R, C2 = s.shape
    return s.reshape(B, HP, R, 2, C2 // 2).transpose(0, 1, 3, 2, 4).reshape(B, HP * 2, R, C2 // 2)


def _rwkv_call(st, z, shift0, s0, prm, out_dtype):
    B, T = st.B, st.T
    z3 = z.reshape(B, T, Z_WIDTH)
    bb, tc, sb = _scan_tiling(B, T, RWKV_BLOCK)
    W = RWKV_WIDTH
    npair = RWKV_HEADS // 2
    s0p = _pack_pairs(s0)
    zspec = lambda w, idx: pl.BlockSpec((bb, tc, w), lambda g, c: (g, c, idx))
    const = lambda a: pl.BlockSpec(a.shape, lambda g, c: (0,) * a.ndim)
    consts = [prm[k] for k in ("mu", "w0", "w2", "a0", "a2", "g2", "k_k", "k_a", "r_k", "lnx_g", "lnx_b")]
    head_of = jnp.arange(W, dtype=jnp.int32) // RWKV_HEAD
    consts += [(head_of[:, None] == head_of[None, :]).astype(BF16), *_block_cumsum_matrices(tc, sb)]
    o, sout = pl.pallas_call(
        functools.partial(_rwkv_body, bb=bb, tc=tc, sb=sb),
        grid=(B // bb, T // tc),
        in_specs=[
            zspec(W, Z_RWKV_R), zspec(W, Z_RWKV_K), zspec(W, Z_RWKV_V), zspec(256, Z_RWKV_LORA),
            pl.BlockSpec((bb, 1, RWKV_PROJ), lambda g, c: (g, 0, 0)),
            pl.BlockSpec((bb, npair, RWKV_HEAD, 128), lambda g, c: (g, 0, 0, 0)),
        ] + [const(a) for a in consts],
        out_specs=[
            pl.BlockSpec((bb, tc, W), lambda g, c: (g, c, 0)),
            pl.BlockSpec((bb, npair, RWKV_HEAD, 128), lambda g, c: (g, 0, 0, 0)),
        ],
        out_shape=[
            jax.ShapeDtypeStruct((B, T, W), out_dtype),
            jax.ShapeDtypeStruct((B, npair, RWKV_HEAD, 128), F32),
        ],
        scratch_shapes=[
            pltpu.VMEM((bb, 8, RWKV_PROJ), F32),
            pltpu.VMEM((bb * npair, RWKV_HEAD, 128), F32),
        ] + [pltpu.VMEM((bb, tc, W), F32)] * 12,
        compiler_params=_cp(("parallel", "arbitrary")),
        name="rwkv7",
    )(z3, z3, z3, z3, shift0.reshape(B, 1, RWKV_PROJ), s0p, *consts)
    return o.reshape(B * T, W), _unpack_pairs(sout)


def _gla_body(zqk_ref, zv_ref, zgr_ref, zgl_ref, s0_ref, wg_ref, bg_ref, ng_ref, ltri_ref, bones_ref,
              o_ref, sout_ref, st_s, q_s, k_s, cw_s, qe_s, ke_s, ab_s, v_s, y_s, *, bb, tc, sb):
    c = pl.program_id(1)
    npair = GLA_HEADS // 2
    HK = GLA_HEADS * GLA_DK

    @pl.when(c == 0)
    def _():
        st_s[...] = s0_ref[...].reshape(st_s.shape)

    for b in range(bb):
        zqk = zqk_ref[b]
        gate = jnp.dot(zgl_ref[b].astype(BF16), wg_ref[...], preferred_element_type=F32) + bg_ref[...]
        log_a = -_softplus(-gate) * (1.0 / GLA_TAU)
        cw = jnp.dot(ltri_ref[...], log_a, precision=HIGHEST, preferred_element_type=F32)
        cl = jnp.dot(bones_ref[...], log_a, precision=HIGHEST, preferred_element_type=F32)
        q = zqk[:, :HK] * (GLA_DK ** -0.5)
        k = zqk[:, HK:]
        q_s[b], k_s[b], cw_s[b], v_s[b] = q, k, cw, zv_ref[b]
        qe_s[b] = q * jnp.exp(cw)
        ke_s[b] = k * jnp.exp(cl - cw)
        ab_s[b] = jnp.exp(cl)

    lo_half = lax.broadcasted_iota(jnp.int32, (sb, 128), 1) < GLA_DK
    t_idx = lax.broadcasted_iota(jnp.int32, (sb, 128), 0)
    lo_of = lambda x: jnp.where(lo_half, x, 0.0)
    hi_of = lambda x: jnp.where(lo_half, 0.0, x)

    def block(bi, carry):
        t0 = pl.multiple_of(bi * sb, sb)
        for b in range(bb):
            for p in range(npair):
                cs = slice(128 * p, 128 * (p + 1))
                v0 = slice(2 * GLA_DV * p, 2 * GLA_DV * p + GLA_DV)
                v1 = slice(2 * GLA_DV * p + GLA_DV, 2 * GLA_DV * (p + 1))
                qb, kb, cwb, qeb, keb, abb = (x[b, pl.ds(t0, sb), cs] for x in (q_s, k_s, cw_s, qe_s, ke_s, ab_s))
                v0b = v_s[b, pl.ds(t0, sb), v0]
                v1b = v_s[b, pl.ds(t0, sb), v1]
                S = st_s[b * npair + p]
                inter = lax.dot_general(jnp.concatenate([lo_of(qeb), hi_of(qeb)], axis=0).astype(BF16),
                                        S.astype(BF16), _NT, preferred_element_type=F32)
                o0 = inter[0:sb]
                o1 = inter[sb:]
                for u in range(sb):
                    decay = jnp.exp(jnp.minimum(cwb - cwb[u:u + 1, :], 0.0))
                    w_tu = jnp.where(t_idx >= u, qb * decay * kb[u:u + 1, :], 0.0)
                    o0 = o0 + jnp.sum(lo_of(w_tu), -1, keepdims=True) * v0b[u:u + 1, :]
                    o1 = o1 + jnp.sum(hi_of(w_tu), -1, keepdims=True) * v1b[u:u + 1, :]
                y_s[b, pl.ds(t0, sb), v0] = o0
                y_s[b, pl.ds(t0, sb), v1] = o1
                upd = lax.dot_general(jnp.concatenate([v0b, v1b], axis=0).astype(BF16),
                                      jnp.concatenate([lo_of(keb), hi_of(keb)], axis=0).astype(BF16),
                                      (((0,), (0,)), ((), ())), preferred_element_type=F32)
                st_s[b * npair + p] = S * abb[0:1, :] + upd
        return carry

    lax.fori_loop(0, tc // sb, block, 0)

    for b in range(bb):
        gr = zgr_ref[b]
        for h in range(GLA_HEADS):
            hs = slice(h * GLA_DV, (h + 1) * GLA_DV)
            o = y_s[b, :, hs]
            on = o * lax.rsqrt(jnp.mean(o * o, -1, keepdims=True) + RMS_EPS) * ng_ref[...]
            o_ref[b, :, hs] = (on * _silu(gr[:, hs])).astype(o_ref.dtype)

    @pl.when(c == pl.num_programs(1) - 1)
    def _():
        sout_ref[...] = st_s[...].reshape(sout_ref.shape)


def _gla_call(st, z, s0, wg_p, bg, ng, out_dtype):
    B, T = st.B, st.T
    z3 = z.reshape(B, T, Z_WIDTH)
    bb, tc, sb = _scan_tiling(B, T, GLA_BLOCK)
    npair = GLA_HEADS // 2
    HK = GLA_HEADS * GLA_DK
    s0p = _pack_pairs(s0.transpose(0, 1, 3, 2))
    zspec = lambda w, idx: pl.BlockSpec((bb, tc, w), lambda g, c: (g, c, idx))
    const = lambda a: pl.BlockSpec(a.shape, lambda g, c: (0,) * a.ndim)
    ltri, bones = _block_cumsum_matrices(tc, sb)
    o, sout = pl.pallas_call(
        functools.partial(_gla_body, bb=bb, tc=tc, sb=sb),
        grid=(B // bb, T // tc),
        in_specs=[
            zspec(512, Z_GLA_QK), zspec(512, Z_GLA_V), zspec(512, Z_GLA_GR), zspec(128, Z_GLA_GL),
            pl.BlockSpec((bb, npair, GLA_DV, 128), lambda g, c: (g, 0, 0, 0)),
            const(wg_p), const(bg), const(ng), const(ltri), const(bones),
        ],
        out_specs=[
            pl.BlockSpec((bb, tc, GLA_WIDTH), lambda g, c: (g, c, 0)),
            pl.BlockSpec((bb, npair, GLA_DV, 128), lambda g, c: (g, 0, 0, 0)),
        ],
        out_shape=[
            jax.ShapeDtypeStruct((B, T, GLA_WIDTH), out_dtype),
            jax.ShapeDtypeStruct((B, npair, GLA_DV, 128), F32),
        ],
        scratch_shapes=[
            pltpu.VMEM((bb * npair, GLA_DV, 128), F32),
        ] + [pltpu.VMEM((bb, tc, HK), F32)] * 6 + [pltpu.VMEM((bb, tc, GLA_WIDTH), F32)] * 2,
        compiler_params=_cp(("parallel", "arbitrary")),
        name="gla",
    )(z3, z3, z3, z3, s0p, wg_p, bg, ng, ltri, bones)
    return o.reshape(B * T, GLA_WIDTH), _unpack_pairs(sout).transpose(0, 1, 3, 2)


def _outproj_body(om_ref, or_ref, og_ref, x_ref, g1_ref, sh2_ref, sc2_ref, w_ref, lng_ref, lnb_ref,
                  x1_ref, h2_ref, *, alpha):
    wm = MLA_HEADS * MLA_V
    mix = (jnp.dot(om_ref[...].astype(BF16), w_ref[0:wm, :], preferred_element_type=F32)
           + jnp.dot(or_ref[...].astype(BF16), w_ref[wm:wm + RWKV_WIDTH, :], preferred_element_type=F32)
           + jnp.dot(og_ref[...].astype(BF16), w_ref[wm + RWKV_WIDTH:, :], preferred_element_type=F32))
    x1 = _layernorm(alpha * x_ref[...] + (1.0 + g1_ref[...]) * mix, lng_ref[...], lnb_ref[...])
    x1_ref[...] = x1
    h2_ref[...] = x1 * (1.0 + sc2_ref[...]) + sh2_ref[...]


def _outproj_call(st, o_mla, o_rwkv, o_gla, x2d, l, w_out_b, ln_g, ln_b, alpha):
    N, D = x2d.shape
    bm = st.row_block(256)
    row = lambda w: pl.BlockSpec((bm, w), lambda i: (i, 0))
    return pl.pallas_call(
        functools.partial(_outproj_body, alpha=alpha),
        grid=(N // bm,),
        in_specs=[
            row(o_mla.shape[1]), row(o_rwkv.shape[1]), row(o_gla.shape[1]), row(D),
            st.mod_spec(l, 2, bm, D), st.mod_spec(l, 3, bm, D), st.mod_spec(l, 4, bm, D),
            pl.BlockSpec((None,) + w_out_b.shape[1:], lambda i: (l, 0, 0)),
            pl.BlockSpec((None, 1, D), lambda i: (l, 0, 0)),
            pl.BlockSpec((None, 1, D), lambda i: (l, 0, 0)),
        ],
        out_specs=[row(D), row(D)],
        out_shape=[jax.ShapeDtypeStruct((N, D), F32), jax.ShapeDtypeStruct((N, D), F32)],
        compiler_params=_cp(("parallel",)),
        name="out_proj",
    )(o_mla, o_rwkv, o_gla, x2d, st.mod, st.mod, st.mod, w_out_b, ln_g, ln_b)


def _router_body(x1_ref, sh2_ref, sc2_ref, wr_ref, o_ref):
    h = x1_ref[...] * (1.0 + sc2_ref[...]) + sh2_ref[...]
    logits = jnp.dot(h, wr_ref[...], precision=HIGHEST, preferred_element_type=F32)
    lane = lax.broadcasted_iota(jnp.int32, logits.shape, 1)
    lane_f = lane.astype(F32)
    lg = jnp.where(lane < N_EXPERTS, logits, -jnp.inf)
    v1 = jnp.max(lg, -1, keepdims=True)
    i1 = jnp.min(jnp.where(lg == v1, lane_f, 128.0), -1, keepdims=True)
    lg2 = jnp.where(lane_f == i1, -jnp.inf, lg)
    v2 = jnp.max(lg2, -1, keepdims=True)
    i2 = jnp.min(jnp.where(lg2 == v2, lane_f, 128.0), -1, keepdims=True)
    e = jnp.exp(v2 - v1)
    g1 = 1.0 / (1.0 + e)
    g2 = e * g1
    o_ref[...] = jnp.where(lane == 0, g1, jnp.where(lane == 1, g2, jnp.where(
        lane == 2, i1, jnp.where(lane == 3, i2, 0.0))))


def _router_call(st, x1, l, wr_p):
    N, D = x1.shape
    bm = st.row_block(512)
    return pl.pallas_call(
        _router_body,
        grid=(N // bm,),
        in_specs=[
            pl.BlockSpec((bm, D), lambda i: (i, 0)),
            st.mod_spec(l, 3, bm, D), st.mod_spec(l, 4, bm, D),
            pl.BlockSpec(wr_p.shape, lambda i: (0, 0)),
        ],
        out_specs=pl.BlockSpec((bm, 128), lambda i: (i, 0)),
        out_shape=jax.ShapeDtypeStruct((N, 128), F32),
        compiler_params=_cp(("parallel",)),
        name="moe_router",
    )(x1, st.mod, st.mod, wr_p)


def _ffn_up_body(te_ref, tv_ref, h_ref, w1_ref, w3_ref, o_ref):
    t = pl.program_id(0)

    @pl.when(tv_ref[t] != 0)
    def _():
        h = h_ref[...].astype(BF16)
        a = jnp.dot(h, w1_ref[...].astype(BF16), preferred_element_type=F32)
        b = jnp.dot(h, w3_ref[...].astype(BF16), preferred_element_type=F32)
        o_ref[...] = (_silu(a) * b).astype(BF16)

    @pl.when(tv_ref[t] == 0)
    def _():
        o_ref[...] = jnp.zeros(o_ref.shape, BF16)


def _ffn_down_body(te_ref, tv_ref, g_ref, w2_ref, o_ref):
    t = pl.program_id(0)

    @pl.when(tv_ref[t] != 0)
    def _():
        o_ref[...] = jnp.dot(g_ref[...], w2_ref[...].astype(BF16), preferred_element_type=F32)

    @pl.when(tv_ref[t] == 0)
    def _():
        o_ref[...] = jnp.zeros(o_ref.shape, F32)


def _ffn_up_gather_body(te_ref, tv_ref, src_ref, h_hbm, w1_ref, w3_ref, o_ref, rows_s, hb_s, sem):
    t = pl.program_id(0)
    bm = rows_s.shape[0]

    @pl.when(jnp.logical_and(pl.program_id(1) == 0, tv_ref[t] != 0))
    def _():
        def issue(r, carry):
            pltpu.make_async_copy(h_hbm.at[pl.ds(src_ref[t * bm + r], 1)], rows_s.at[pl.ds(r, 1)], sem).start()
            return carry

        lax.fori_loop(0, bm, issue, 0, unroll=8)
        pltpu.make_async_copy(h_hbm.at[pl.ds(0, bm)], rows_s, sem).wait()
        hb_s[...] = rows_s[...].astype(BF16)

    @pl.when(tv_ref[t] != 0)
    def _():
        h = hb_s[...]
        a = jnp.dot(h, w1_ref[...].astype(BF16), preferred_element_type=F32)
        b = jnp.dot(h, w3_ref[...].astype(BF16), preferred_element_type=F32)
        o_ref[...] = (_silu(a) * b).astype(BF16)

    @pl.when(tv_ref[t] == 0)
    def _():
        o_ref[...] = jnp.zeros(o_ref.shape, BF16)


def _ffn_call(h, tile_expert, tile_valid, w1, w3, w2, bm, src=None):
    D = h.shape[1]
    F = w1.shape[-1]
    R = h.shape[0] if src is None else src.shape[0]
    nt = R // bm
    bf = 512
    if src is None:
        up = pl.pallas_call(
            _ffn_up_body,
            grid_spec=pltpu.PrefetchScalarGridSpec(
                num_scalar_prefetch=2,
                grid=(nt, F // bf),
                in_specs=[
                    pl.BlockSpec((bm, D), lambda t, j, te, tv: (t, 0)),
                    pl.BlockSpec((None, D, bf), lambda t, j, te, tv: (te[t], 0, j)),
                    pl.BlockSpec((None, D, bf), lambda t, j, te, tv: (te[t], 0, j)),
                ],
                out_specs=pl.BlockSpec((bm, bf), lambda t, j, te, tv: (t, j)),
            ),
            out_shape=jax.ShapeDtypeStruct((R, F), BF16),
            compiler_params=_cp(("parallel", "arbitrary")),
            name="ffn_up",
        )(tile_expert, tile_valid, h, w1, w3)
    else:
        nj = F // bf
        up = pl.pallas_call(
            _ffn_up_gather_body,
            grid_spec=pltpu.PrefetchScalarGridSpec(
                num_scalar_prefetch=3,
                grid=(nt, F // bf),
                in_specs=[
                    pl.BlockSpec(memory_space=pl.ANY),
                    pl.BlockSpec((None, D, bf), lambda t, j, te, tv, sr: (te[t], 0, jnp.where(tv[t] != 0, j, nj - 1))),
                    pl.BlockSpec((None, D, bf), lambda t, j, te, tv, sr: (te[t], 0, jnp.where(tv[t] != 0, j, nj - 1))),
                ],
                out_specs=pl.BlockSpec((bm, bf), lambda t, j, te, tv, sr: (t, j)),
                scratch_shapes=[pltpu.VMEM((bm, D), h.dtype), pltpu.VMEM((bm, D), BF16),
                                pltpu.SemaphoreType.DMA(())],
            ),
            out_shape=jax.ShapeDtypeStruct((R, F), BF16),
            compiler_params=_cp(("arbitrary", "arbitrary")),
            name="ffn_up_gather",
        )(tile_expert, tile_valid, src, h, w1, w3)
    bn = 256
    return pl.pallas_call(
        _ffn_down_body,
        grid_spec=pltpu.PrefetchScalarGridSpec(
            num_scalar_prefetch=2,
            grid=(nt, D // bn),
            in_specs=[
                pl.BlockSpec((bm, F), lambda t, n, te, tv: (t, 0)),
                pl.BlockSpec((None, F, bn), lambda t, n, te, tv: (te[t], 0, jnp.where(tv[t] != 0, n, D // bn - 1))),
            ],
            out_specs=pl.BlockSpec((bm, bn), lambda t, n, te, tv: (t, n)),
        ),
        out_shape=jax.ShapeDtypeStruct((R, D), F32),
        compiler_params=_cp(("parallel", "arbitrary")),
        name="ffn_down",
    )(tile_expert, tile_valid, up, w2)


def _combine_body(*refs, alpha, nterm):
    x1_ref, g2_ref, lng_ref, lnb_ref = refs[:4]
    f_refs = refs[4:4 + nterm]
    o_ref = refs[-1]
    if nterm == 1:
        f = f_refs[0][...]
    else:
        gates = refs[4 + nterm][...]
        f = f_refs[0][...] * gates[:, 0:1] + f_refs[1][...] * gates[:, 1:2]
    o_ref[...] = _layernorm(alpha * x1_ref[...] + (1.0 + g2_ref[...]) * f, lng_ref[...], lnb_ref[...])


def _combine_call(st, x1, l, ln_g, ln_b, alpha, terms, gates=None, row0=0):
    N, D = x1.shape
    bm = st.row_block(512)
    assert row0 % bm == 0
    row = lambda w: pl.BlockSpec((bm, w), lambda i: (i, 0))
    shared = lambda w: pl.BlockSpec((bm, w), lambda i: (i + row0 // bm, 0))
    args = [x1, st.mod, ln_g, ln_b] + list(terms) + ([gates] if gates is not None else [])
    return pl.pallas_call(
        functools.partial(_combine_body, alpha=alpha, nterm=len(terms)),
        grid=(N // bm,),
        in_specs=[
            row(D), st.mod_spec(l, 5, bm, D),
            pl.BlockSpec((None, 1, D), lambda i: (l, 0, 0)),
            pl.BlockSpec((None, 1, D), lambda i: (l, 0, 0)),
        ] + [shared(D)] * len(terms) + ([shared(128)] if gates is not None else []),
        out_specs=row(D),
        out_shape=jax.ShapeDtypeStruct((N, D), F32),
        compiler_params=_cp(("parallel",)),
        name="ffn_residual",
    )(*args)


def _repack_w_in(w_in):
    L, D, _ = w_in.shape
    w = w_in.astype(BF16)
    zeros = lambda n: jnp.zeros((L, D, n), BF16)
    r0 = MLA_IN
    g0 = MLA_IN + RWKV_PROJ
    HK = GLA_HEADS * GLA_DK
    gq, gk, gv = g0, g0 + HK, g0 + 2 * HK
    ggl = gv + GLA_WIDTH
    ggr = ggl + GLA_GATE_LORA
    parts = [
        w[:, :, :MLA_IN], zeros(Z_MLA_W - MLA_IN),
        w[:, :, r0:r0 + RWKV_PROJ],
        w[:, :, ggl:ggr], zeros(256 - GLA_GATE_LORA),
        w[:, :, gq:gv], w[:, :, gv:ggl], w[:, :, ggr:ggr + GLA_WIDTH],
    ]
    out = jnp.concatenate(parts, axis=-1)
    assert out.shape[-1] == Z_WIDTH
    return out


def _moe_plan(gi, bm):
    n = gi.shape[0]
    E = N_EXPERTS
    experts = gi[:, 2:4].astype(jnp.int32).reshape(-1)
    onehot = (experts[:, None] == jnp.arange(E, dtype=jnp.int32)[None, :]).astype(jnp.int32)
    counts = jnp.sum(onehot, axis=0)
    tiles = (counts + bm - 1) // bm
    tile_end = jnp.cumsum(tiles)
    tile_start = tile_end - tiles
    rank = jnp.take_along_axis(jnp.cumsum(onehot, axis=0) - 1, experts[:, None], axis=1)[:, 0]
    pos = tile_start[experts] * bm + rank
    nt = (2 * n + bm - 1) // bm + E
    src = jnp.zeros((nt * bm,), jnp.int32).at[pos].set(jnp.arange(2 * n, dtype=jnp.int32) // 2)
    t_ids = jnp.arange(nt, dtype=jnp.int32)
    tile_expert = jnp.minimum(jnp.sum((t_ids[:, None] >= tile_end[None, :]).astype(jnp.int32), axis=1), E - 1)
    tile_valid = (t_ids < tile_end[-1]).astype(jnp.int32)
    tile_expert = jnp.where(tile_valid != 0, tile_expert, tile_expert[jnp.maximum(tile_end[-1] - 1, 0)])
    return src, pos.reshape(n, 2), tile_expert, tile_valid


def kernel(x_prompt, x_sample, cache_kv_latent, cache_k_rope, state_rwkv, state_rwkv_shift, state_gla, page_table, c_prompt, c_sample, w_in, w_out, mla_q_norm, mla_kv_norm, mla_w_q_up, mla_w_uk, mla_w_uv, rwkv_mu, rwkv_w0, rwkv_w2, rwkv_a0, rwkv_a2, rwkv_g2, rwkv_k_k, rwkv_k_a, rwkv_r_k, rwkv_lnx_g, rwkv_lnx_b, gla_w_g2, gla_b_g, gla_norm_g, ada_w, ada_b, ln1_g, ln1_b, ln2_g, ln2_b, ffn_w1, ffn_w3, ffn_w2, moe_router, moe_w1, moe_w3, moe_w2):
    Bp, Tp, D = x_prompt.shape
    Bs, Ts, _ = x_sample.shape
    L = w_in.shape[0]
    past_len = page_table.shape[1] * PAGE_SIZE
    alpha = (2 * L) ** 0.25
    dt = x_prompt.dtype

    n_c = Bp + Bs
    c_all = jnp.concatenate([c_prompt, c_sample, jnp.zeros((-n_c % 8, D), F32)], axis=0)
    mod = _ada_call(c_all, ada_w, ada_b)
    st_p = _Stream(Bp, Tp, mod[:, :Bp].reshape(L, Bp, 1, 6 * D), False, 0)
    st_s = _Stream(Bs, Ts, jnp.repeat(mod[:, Bp:Bp + Bs], Ts, axis=1), True, past_len)

    w_in_p = _repack_w_in(w_in)
    w_out_b = w_out.astype(BF16)
    wq = mla_w_q_up.astype(BF16)
    wq_p = jnp.concatenate([wq[..., :MLA_NOPE].reshape(L, MLA_Q_RANK, -1),
                            wq[..., MLA_NOPE:].reshape(L, MLA_Q_RANK, -1)], axis=-1)
    wuk_p = mla_w_uk.astype(BF16).transpose(0, 2, 3, 1)
    wuv_p = mla_w_uv.astype(BF16).transpose(0, 2, 1, 3)
    inv = 1.0 / (ROPE_BASE ** (jnp.arange(0, MLA_ROPE, 2, dtype=F32) / MLA_ROPE))
    inv128 = jnp.tile(inv, 4).reshape(1, 128)
    zpad = lambda a, n: jnp.concatenate([a, jnp.zeros((L, n) + a.shape[2:], a.dtype)], axis=1)
    zpre = lambda a, n: jnp.concatenate([jnp.zeros((L, n) + a.shape[2:], a.dtype), a], axis=1)
    rw_w2 = zpad(rwkv_w2.astype(BF16), RWKV_A_LORA)
    rw_a2 = zpre(rwkv_a2.astype(BF16), RWKV_W_LORA)
    gla_wg = zpad(gla_w_g2.astype(BF16), 128 - GLA_GATE_LORA)
    row1 = lambda a: a.reshape(1, -1)

    xp = x_prompt.reshape(Bp * Tp, D)
    xs = x_sample.reshape(Bs * Ts, D)
    zeros_shift = jnp.zeros((Bp, RWKV_PROJ), dt)
    zeros_rwkv = jnp.zeros((Bp, RWKV_HEADS, RWKV_HEAD, RWKV_HEAD), F32)
    zeros_gla = jnp.zeros((Bp, GLA_HEADS, GLA_DK, GLA_DV), F32)
    outs_p = [[] for _ in range(5)]
    outs_s = [[] for _ in range(5)]

    for l in range(L):
        rw = dict(mu=row1(rwkv_mu[l]), w0=row1(rwkv_w0[l]), w2=rw_w2[l], a0=row1(rwkv_a0[l]), a2=rw_a2[l],
                  g2=rwkv_g2[l].astype(BF16), k_k=row1(rwkv_k_k[l]), k_a=row1(rwkv_k_a[l]),
                  r_k=row1(rwkv_r_k[l]), lnx_g=row1(rwkv_lnx_g[l]), lnx_b=row1(rwkv_lnx_b[l]))
        streams = []
        for st, x2d, sample in ((st_p, xp, False), (st_s, xs, True)):
            z = _inproj_call(st, x2d, l, w_in_p)
            qlat, qpe, lat, kpe = _mla_prep_call(st, z, row1(mla_q_norm[l]), row1(mla_kv_norm[l]),
                                                 wq_p[l], wuk_p[l], inv128, F32 if sample else BF16)
            if sample:
                o_mla = _attn_sample_call(st, l, qlat, qpe, lat, kpe, cache_kv_latent, cache_k_rope,
                                          page_table, wuv_p[l])
                shift0, s_r0, s_g0 = state_rwkv_shift[l], state_rwkv[l], state_gla[l]
            else:
                o_mla = _attn_prompt_call(st, qlat, qpe, lat, kpe, wuv_p[l])
                shift0, s_r0, s_g0 = zeros_shift, zeros_rwkv, zeros_gla
            o_dt = F32 if sample else BF16
            o_rwkv, s_r = _rwkv_call(st, z, shift0, s_r0, rw, o_dt)
            o_gla, s_g = _gla_call(st, z, s_g0, gla_wg[l], row1(gla_b_g[l]), row1(gla_norm_g[l]), o_dt)
            x1, h2 = _outproj_call(st, o_mla, o_rwkv, o_gla, x2d, l, w_out_b, ln1_g.reshape(L, 1, D),
                                   ln1_b.reshape(L, 1, D), alpha)
            shift = z.reshape(st.B, st.T, Z_WIDTH)[:, -1, Z_MLA_W:Z_MLA_W + RWKV_PROJ]
            acc = outs_s if sample else outs_p
            for lst, val in zip(acc, (lat.reshape(st.B, st.T, -1), kpe.reshape(st.B, st.T, -1),
                                      s_r.astype(dt), shift, s_g.astype(dt))):
                lst.append(val)
            streams.append((st, x1, h2))

        (_, x1p, h2p), (_, x1s, h2s) = streams
        h_all = jnp.concatenate([h2p, h2s], axis=0)
        n_all = h_all.shape[0]
        n_p = h2p.shape[0]
        bm = _pick(n_all, 1024)
        lng, lnb = ln2_g.reshape(L, 1, D), ln2_b.reshape(L, 1, D)
        if l % 2 == 0:
            e = l // 2
            nt = n_all // bm
            f = _ffn_call(h_all, jnp.zeros((nt,), jnp.int32), jnp.ones((nt,), jnp.int32),
                          ffn_w1[e:e + 1], ffn_w3[e:e + 1], ffn_w2[e:e + 1], bm)
            xp = _combine_call(st_p, x1p, l, lng, lnb, alpha, [f])
            xs = _combine_call(st_s, x1s, l, lng, lnb, alpha, [f], row0=n_p)
        else:
            e = l // 2
            wr_p = jnp.concatenate([moe_router[e], jnp.zeros((D, 128 - N_EXPERTS), F32)], axis=1)
            gi = jnp.concatenate([_router_call(st_p, x1p, l, wr_p), _router_call(st_s, x1s, l, wr_p)], axis=0)
            src, pos, tile_expert, tile_valid = _moe_plan(gi, bm)
            f_sorted = _ffn_call(h_all, tile_expert, tile_valid, moe_w1[e], moe_w3[e], moe_w2[e], bm, src=src)
            f0 = f_sorted[pos[:, 0]]
            f1 = f_sorted[pos[:, 1]]
            xp = _combine_call(st_p, x1p, l, lng, lnb, alpha, [f0, f1], gi)
            xs = _combine_call(st_s, x1s, l, lng, lnb, alpha, [f0, f1], gi, row0=n_p)

    stack = lambda lst: jnp.stack(lst)
    return (xp.reshape(Bp, Tp, D), xs.reshape(Bs, Ts, D),
            *[stack(v) for v in outs_p], *[stack(v) for v in outs_s])
```

```python
import functools

import jax
import jax.numpy as jnp
from jax import lax
from jax.experimental import pallas as pl
from jax.experimental.pallas import tpu as pltpu

F32 = jnp.float32
BF16 = jnp.bfloat16
HIGHEST = lax.Precision.HIGHEST

PAGE_SIZE = 128
MLA_HEADS = 8
MLA_NOPE = 128
MLA_ROPE = 64
MLA_V = 128
MLA_Q_RANK = 512
MLA_KV_RANK = 256
MLA_IN = MLA_Q_RANK + MLA_KV_RANK + MLA_ROPE
MLA_SCALE = (MLA_NOPE + MLA_ROPE) ** -0.5
ROPE_BASE = 10000.0
RWKV_HEADS = 8
RWKV_HEAD = 64
RWKV_WIDTH = RWKV_HEADS * RWKV_HEAD
RWKV_W_LORA = 64
RWKV_A_LORA = 64
RWKV_G_LORA = 128
RWKV_PROJ = 3 * RWKV_WIDTH + RWKV_W_LORA + RWKV_A_LORA + RWKV_G_LORA
RWKV_GN_EPS = 64e-5
GLA_HEADS = 4
GLA_DK = 64
GLA_DV = 128
GLA_WIDTH = GLA_HEADS * GLA_DV
GLA_GATE_LORA = 16
GLA_TAU = 16.0
GLA_PROJ = 2 * GLA_HEADS * GLA_DK + 2 * GLA_WIDTH + GLA_GATE_LORA
N_EXPERTS = 8
LN_EPS = 1e-5
RMS_EPS = 1e-6

Z_WIDTH = 4608
Z_MLA_W = 1024
Z_RWKV_R, Z_RWKV_K, Z_RWKV_V = 2, 3, 4
Z_RWKV_LORA = 10
Z_GLA_GL = 22
Z_GLA_QK, Z_GLA_V, Z_GLA_GR = 6, 7, 8

MOE_TILE_ROWS = 768
RWKV_BLOCK = 16
GLA_BLOCK = 16
SCAN_BATCH = 4

VMEM_LIMIT_MB = 56


def _cp(sem, vmem_mb=VMEM_LIMIT_MB):
    return pltpu.CompilerParams(dimension_semantics=sem, vmem_limit_bytes=vmem_mb * 2**20)


def _pick(n, pref):
    if n <= pref:
        return n
    b = pref - pref % 8
    while b >= 8:
        if n % b == 0:
            return b
        b -= 8
    return n


def _silu(x):
    return x * jax.nn.sigmoid(x)


def _softplus(u):
    return jnp.maximum(u, 0.0) + jnp.log(1.0 + jnp.exp(-jnp.abs(u)))


def _lane_group_sum(x, group):
    axis = x.ndim - 1
    width = x.shape[axis]
    lane = lax.broadcasted_iota(jnp.int32, x.shape, axis)
    s = 1
    while s < group:
        partner = jnp.where(jnp.bitwise_and(lane, s) == 0,
                            pltpu.roll(x, width - s, axis), pltpu.roll(x, s, axis))
        x = x + partner
        s *= 2
    return x


def _layernorm(y, g, b):
    mu = jnp.mean(y, -1, keepdims=True)
    d = y - mu
    var = jnp.mean(d * d, -1, keepdims=True)
    return d * lax.rsqrt(var + LN_EPS) * g + b


def _ada_body(c_ref, w_ref, b_ref, o_ref):
    c = c_ref[...]
    s = _silu(c).astype(BF16)
    o_ref[...] = jnp.dot(s, w_ref[...].astype(BF16), preferred_element_type=F32) + b_ref[...]


def _ada_call(c_all, ada_w, ada_b):
    L, D, N6 = ada_w.shape
    Mc = c_all.shape[0]
    bn = 1024
    return pl.pallas_call(
        _ada_body,
        grid=(L, N6 // bn),
        in_specs=[
            pl.BlockSpec((Mc, D), lambda l, j: (0, 0)),
            pl.BlockSpec((None, D, bn), lambda l, j: (l, 0, j)),
            pl.BlockSpec((None, 1, bn), lambda l, j: (l, 0, j)),
        ],
        out_specs=pl.BlockSpec((None, Mc, bn), lambda l, j: (l, 0, j)),
        out_shape=jax.ShapeDtypeStruct((L, Mc, N6), F32),
        compiler_params=_cp(("parallel", "parallel")),
        name="ada_mod",
    )(c_all, ada_w, ada_b.reshape(L, 1, N6))


class _Stream:
    def __init__(self, B, T, mod, per_row, pos_base):
        self.B, self.T, self.N = B, T, B * T
        self.mod = mod
        self.per_row = per_row
        self.pos_base = pos_base

    def mod_spec(self, l, k, bm, D):
        if self.per_row:
            return pl.BlockSpec((None, bm, D), lambda i, *_: (l, i, k))
        nb = self.T // bm
        return pl.BlockSpec((None, None, 1, D), lambda i, *_: (l, i // nb, 0, k))

    def row_block(self, pref):
        return _pick(self.N if self.per_row else self.T, pref)


def _inproj_body(x_ref, sh_ref, sc_ref, w_ref, o_ref, h_s):
    @pl.when(pl.program_id(1) == 0)
    def _():
        h_s[...] = (x_ref[...] * (1.0 + sc_ref[...]) + sh_ref[...]).astype(BF16)

    o_ref[...] = jnp.dot(h_s[...], w_ref[...], preferred_element_type=F32)


def _inproj_call(st, x2d, l, w_in_p):
    N, D = x2d.shape
    Wz = w_in_p.shape[-1]
    bm = st.row_block(1024)
    bn = 512
    return pl.pallas_call(
        _inproj_body,
        grid=(N // bm, Wz // bn),
        in_specs=[
            pl.BlockSpec((bm, D), lambda i, j: (i, 0)),
            st.mod_spec(l, 0, bm, D),
            st.mod_spec(l, 1, bm, D),
            pl.BlockSpec((None, D, bn), lambda i, j: (l, 0, j)),
        ],
        out_specs=pl.BlockSpec((bm, bn), lambda i, j: (i, j)),
        out_shape=jax.ShapeDtypeStruct((N, Wz), F32),
        scratch_shapes=[pltpu.VMEM((bm, D), BF16)],
        compiler_params=_cp(("parallel", "arbitrary")),
        name="in_proj",
    )(x2d, st.mod, st.mod, w_in_p)


def _rope(x, cos, sin_signed, first_half):
    w = x.shape[1]
    swapped = jnp.where(first_half, pltpu.roll(x, w - 32, 1), pltpu.roll(x, 32, 1))
    return x * cos + swapped * sin_signed


def _mla_prep_body(z_ref, gq_ref, gkv_ref, wq_ref, wuk_ref, inv_ref, qlat_ref, qpe_ref, lat_ref, kpe_ref,
                   *, bm, period, pos_base):
    i = pl.program_id(0)
    z = z_ref[...]
    zq = z[:, :MLA_Q_RANK]
    zkv = z[:, MLA_Q_RANK:MLA_Q_RANK + MLA_KV_RANK]
    zpe = z[:, MLA_Q_RANK + MLA_KV_RANK:MLA_Q_RANK + MLA_KV_RANK + 128]
    qn = zq * lax.rsqrt(jnp.mean(zq * zq, -1, keepdims=True) + RMS_EPS) * gq_ref[...]
    q = jnp.dot(qn.astype(BF16), wq_ref[...], preferred_element_type=F32)
    lat_ref[...] = zkv * lax.rsqrt(jnp.mean(zkv * zkv, -1, keepdims=True) + RMS_EPS) * gkv_ref[...]

    row = lax.broadcasted_iota(jnp.int32, (bm, 128), 0) + i * bm
    pos = (pos_base + jnp.bitwise_and(row, period - 1)).astype(F32)
    ang = pos * inv_ref[...]
    cos = jnp.cos(ang)
    sin = jnp.sin(ang)
    first = jnp.bitwise_and(lax.broadcasted_iota(jnp.int32, (bm, 128), 1), 63) < 32
    sin_s = jnp.where(first, -sin, sin)
    kpe_ref[...] = _rope(zpe, cos, sin_s, first)[:, :MLA_ROPE]

    npe = MLA_HEADS * MLA_ROPE // 128
    cos4 = jnp.concatenate([cos] * npe, axis=1)
    sin4 = jnp.concatenate([sin_s] * npe, axis=1)
    first4 = jnp.bitwise_and(lax.broadcasted_iota(jnp.int32, (bm, 128 * npe), 1), 63) < 32
    q_pe = q[:, MLA_HEADS * MLA_NOPE:]
    qpe_ref[...] = (_rope(q_pe, cos4, sin4, first4) * MLA_SCALE).astype(qpe_ref.dtype)
    for h in range(MLA_HEADS):
        qh = q[:, h * MLA_NOPE:(h + 1) * MLA_NOPE].astype(BF16)
        ql = jnp.dot(qh, wuk_ref[h], preferred_element_type=F32) * MLA_SCALE
        qlat_ref[:, h * MLA_KV_RANK:(h + 1) * MLA_KV_RANK] = ql.astype(qlat_ref.dtype)


def _mla_prep_call(st, z, gq, gkv, wq_p, wuk_p, inv128, q_dtype):
    N = z.shape[0]
    bm = _pick(N, 512)
    HR = MLA_HEADS * MLA_KV_RANK
    HP = MLA_HEADS * MLA_ROPE
    assert st.T & (st.T - 1) == 0
    body = functools.partial(_mla_prep_body, bm=bm, period=st.T, pos_base=st.pos_base)
    const = lambda shape: pl.BlockSpec(shape, lambda i: (0,) * len(shape))
    return pl.pallas_call(
        body,
        grid=(N // bm,),
        in_specs=[
            pl.BlockSpec((bm, Z_MLA_W), lambda i: (i, 0)),
            const((1, MLA_Q_RANK)), const((1, MLA_KV_RANK)),
            const(wq_p.shape), const(wuk_p.shape), const((1, 128)),
        ],
        out_specs=[
            pl.BlockSpec((bm, HR), lambda i: (i, 0)),
            pl.BlockSpec((bm, HP), lambda i: (i, 0)),
            pl.BlockSpec((bm, MLA_KV_RANK), lambda i: (i, 0)),
            pl.BlockSpec((bm, MLA_ROPE), lambda i: (i, 0)),
        ],
        out_shape=[
            jax.ShapeDtypeStruct((N, HR), q_dtype),
            jax.ShapeDtypeStruct((N, HP), q_dtype),
            jax.ShapeDtypeStruct((N, MLA_KV_RANK), F32),
            jax.ShapeDtypeStruct((N, MLA_ROPE), F32),
        ],
        compiler_params=_cp(("parallel",)),
        name="mla_prep",
    )(z, gq, gkv, wq_p, wuk_p, inv128)


_NT = (((1,), (1,)), ((), ()))


def _softmax_update(s, h, m_s, l_s, acc_s, values_bf16):
    m_prev = m_s[h]
    m_new = jnp.maximum(m_prev, jnp.max(s, -1, keepdims=True))
    alpha = jnp.exp(m_prev - m_new)
    p = jnp.exp(s - m_new[:, :1])
    l_s[h] = alpha * l_s[h] + jnp.sum(p, -1, keepdims=True)
    acc_s[h] = acc_s[h] * alpha[:, :1] + jnp.dot(p.astype(BF16), values_bf16, preferred_element_type=F32)
    m_s[h] = m_new


def _attn_prompt_body(ql_ref, qp_ref, lat_ref, kpe_ref, wuv_ref, o_ref, m_s, l_s, acc_s, *, bq):
    qi = pl.program_id(1)
    ki = pl.program_id(2)

    @pl.when(ki == 0)
    def _():
        m_s[...] = jnp.full(m_s.shape, -jnp.inf, F32)
        l_s[...] = jnp.zeros(l_s.shape, F32)
        acc_s[...] = jnp.zeros(acc_s.shape, F32)

    def step(masked):
        latb = lat_ref[...].astype(BF16)
        kpb = kpe_ref[...].astype(BF16)
        if masked:
            causal = (lax.broadcasted_iota(jnp.int32, (bq, bq), 0)
                      >= lax.broadcasted_iota(jnp.int32, (bq, bq), 1))
        for h in range(MLA_HEADS):
            s = (lax.dot_general(ql_ref[:, h * MLA_KV_RANK:(h + 1) * MLA_KV_RANK], latb, _NT,
                                 preferred_element_type=F32)
                 + lax.dot_general(qp_ref[:, h * MLA_ROPE:(h + 1) * MLA_ROPE], kpb, _NT,
                                   preferred_element_type=F32))
            if masked:
                s = jnp.where(causal, s, -jnp.inf)
            _softmax_update(s, h, m_s, l_s, acc_s, latb)

    @pl.when(ki < qi)
    def _():
        step(False)

    @pl.when(ki == qi)
    def _():
        step(True)
        for h in range(MLA_HEADS):
            o_lat = (acc_s[h] / l_s[h][:, :1]).astype(BF16)
            o_ref[:, h * MLA_V:(h + 1) * MLA_V] = jnp.dot(
                o_lat, wuv_ref[h], preferred_element_type=F32).astype(o_ref.dtype)


def _attn_prompt_call(st, qlat, qpe, lat, kpe, wuv_p):
    B, T = st.B, st.T
    bq = _pick(T, 512)
    nq = T // bq
    HR = MLA_HEADS * MLA_KV_RANK
    HP = MLA_HEADS * MLA_ROPE
    kv = lambda b, qi, ki: (b * nq + jnp.minimum(ki, qi), 0)
    return pl.pallas_call(
        functools.partial(_attn_prompt_body, bq=bq),
        grid=(B, nq, nq),
        in_specs=[
            pl.BlockSpec((bq, HR), lambda b, qi, ki: (b * nq + qi, 0)),
            pl.BlockSpec((bq, HP), lambda b, qi, ki: (b * nq + qi, 0)),
            pl.BlockSpec((bq, MLA_KV_RANK), kv),
            pl.BlockSpec((bq, MLA_ROPE), kv),
            pl.BlockSpec(wuv_p.shape, lambda b, qi, ki: (0, 0, 0)),
        ],
        out_specs=pl.BlockSpec((bq, MLA_HEADS * MLA_V), lambda b, qi, ki: (b * nq + qi, 0)),
        out_shape=jax.ShapeDtypeStruct((B * T, MLA_HEADS * MLA_V), BF16),
        scratch_shapes=[
            pltpu.VMEM((MLA_HEADS, bq, 128), F32),
            pltpu.VMEM((MLA_HEADS, bq, 128), F32),
            pltpu.VMEM((MLA_HEADS, bq, MLA_KV_RANK), F32),
        ],
        compiler_params=_cp(("parallel", "parallel", "arbitrary")),
        name="attn_prompt",
    )(qlat, qpe, lat, kpe, wuv_p)


def _attn_sample_body(pt_ref, ql_ref, qp_ref, latn_ref, kpen_ref, wuv_ref, cache_lat, cache_pe_t, o_ref,
                      lat_buf, pe_buf, sem, *, layer, npages, ts):
    s_id = pl.program_id(0)
    slot = jnp.bitwise_and(s_id, 1)
    rows = MLA_HEADS * ts

    def start_pages(seq, slot_):
        def body(g, carry):
            page = pt_ref[seq, g]
            pltpu.make_async_copy(cache_lat.at[layer, page], lat_buf.at[slot_, g], sem.at[slot_]).start()
            pltpu.make_async_copy(cache_pe_t.at[layer, page], pe_buf.at[slot_, g], sem.at[slot_]).start()
            return carry

        lax.fori_loop(0, npages, body, 0, unroll=4)

    @pl.when(s_id == 0)
    def _():
        start_pages(0, 0)

    pltpu.make_async_copy(cache_lat.at[layer, pl.ds(0, npages)], lat_buf.at[slot], sem.at[slot]).wait()
    pltpu.make_async_copy(cache_pe_t.at[layer, pl.ds(0, npages)], pe_buf.at[slot], sem.at[slot]).wait()

    @pl.when(s_id + 1 < pl.num_programs(0))
    def _():
        start_pages(s_id + 1, 1 - slot)

    qlb = jnp.concatenate([ql_ref[:, h * MLA_KV_RANK:(h + 1) * MLA_KV_RANK] for h in range(MLA_HEADS)],
                          axis=0).astype(BF16)
    qpb = jnp.concatenate([qp_ref[:, h * MLA_ROPE:(h + 1) * MLA_ROPE] for h in range(MLA_HEADS)],
                          axis=0).astype(BF16)

    def scores(latb, kpe_t):
        return (lax.dot_general(qlb, latb, _NT, preferred_element_type=F32)
                + jnp.dot(qpb, kpe_t, preferred_element_type=F32))

    lats = [lat_buf[slot, g].astype(BF16) for g in range(npages)]
    s_pages = [scores(lats[g], pe_buf[slot, g].astype(BF16)) for g in range(npages)]
    pad = PAGE_SIZE - ts
    lat_new = jnp.concatenate([latn_ref[...], jnp.zeros((pad, MLA_KV_RANK), F32)], axis=0).astype(BF16)
    kpe_new = jnp.concatenate([kpen_ref[...], jnp.zeros((pad, MLA_ROPE), F32)], axis=0).T.astype(BF16)
    t_q = jnp.bitwise_and(lax.broadcasted_iota(jnp.int32, (rows, PAGE_SIZE), 0), ts - 1)
    t_k = lax.broadcasted_iota(jnp.int32, (rows, PAGE_SIZE), 1)
    s_new = jnp.where(t_k <= t_q, scores(lat_new, kpe_new), -jnp.inf)
    s_all = jnp.concatenate(s_pages + [s_new], axis=1)
    p = jnp.exp(s_all - jnp.max(s_all, -1, keepdims=True))
    denom = jnp.sum(p, -1, keepdims=True)
    pv = jnp.zeros((rows, MLA_KV_RANK), F32)
    for g, v in enumerate(lats + [lat_new]):
        pv = pv + jnp.dot(p[:, g * PAGE_SIZE:(g + 1) * PAGE_SIZE].astype(BF16), v, preferred_element_type=F32)
    o_lat = (pv / denom).astype(BF16)
    for h in range(MLA_HEADS):
        o_h = jnp.dot(o_lat, wuv_ref[h], preferred_element_type=F32)
        o_ref[:, h * MLA_V:(h + 1) * MLA_V] = o_h[h * ts:(h + 1) * ts, :].astype(o_ref.dtype)


def _attn_sample_call(st, l, qlat, qpe, lat, kpe, cache_lat, cache_pe, page_table, wuv_p):
    S, ts = st.B, st.T
    P = page_table.shape[1]
    HR = MLA_HEADS * MLA_KV_RANK
    HP = MLA_HEADS * MLA_ROPE
    assert ts & (ts - 1) == 0
    page_bytes = PAGE_SIZE * (MLA_KV_RANK + MLA_ROPE) * 4
    assert 2 * P * page_bytes <= 32 * 2**20, "the whole past of one sequence must fit the two page slots"
    cache_pe_t = jnp.swapaxes(cache_pe, 2, 3)
    grid_spec = pltpu.PrefetchScalarGridSpec(
        num_scalar_prefetch=1,
        grid=(S,),
        in_specs=[
            pl.BlockSpec((ts, HR), lambda s, pt: (s, 0)),
            pl.BlockSpec((ts, HP), lambda s, pt: (s, 0)),
            pl.BlockSpec((ts, MLA_KV_RANK), lambda s, pt: (s, 0)),
            pl.BlockSpec((ts, MLA_ROPE), lambda s, pt: (s, 0)),
            pl.BlockSpec(wuv_p.shape, lambda s, pt: (0, 0, 0)),
            pl.BlockSpec(memory_space=pl.ANY),
            pl.BlockSpec(memory_space=pl.ANY),
        ],
        out_specs=pl.BlockSpec((ts, MLA_HEADS * MLA_V), lambda s, pt: (s, 0)),
        scratch_shapes=[
            pltpu.VMEM((2, P, PAGE_SIZE, MLA_KV_RANK), F32),
            pltpu.VMEM((2, P, MLA_ROPE, PAGE_SIZE), F32),
            pltpu.SemaphoreType.DMA((2,)),
        ],
    )
    return pl.pallas_call(
        functools.partial(_attn_sample_body, layer=l, npages=P, ts=ts),
        grid_spec=grid_spec,
        out_shape=jax.ShapeDtypeStruct((S * ts, MLA_HEADS * MLA_V), F32),
        compiler_params=_cp(("arbitrary",)),
        name="attn_sample",
    )(page_table, qlat, qpe, lat, kpe, wuv_p, cache_lat, cache_pe_t)


def _scan_tiling(B, T, block):
    bb = min(B, SCAN_BATCH)
    sb = block if T % block == 0 else 8
    assert B % bb == 0 and T % sb == 0
    return bb, _pick(T, 256), sb


def _block_cumsum_matrices(tc, sb):
    t = jnp.arange(tc, dtype=jnp.int32)
    same = (t[:, None] // sb) == (t[None, :] // sb)
    return (jnp.logical_and(same, t[None, :] <= t[:, None]).astype(F32), same.astype(F32))


def _rwkv_body(zr_ref, zk_ref, zv_ref, zl_ref, sh0_ref, s0_ref, mu_ref, w0_ref, w2_ref, a0_ref, a2_ref,
               g2_ref, kk_ref, ka_ref, rk_ref, lng_ref, lnb_ref, blk_ref, ltri_ref, bones_ref, o_ref, sout_ref,
               carry_s, st_s, r_s, k_s, v_s, g_s, ae_s, re_s, bq_s, kq_s, bh_s, kh_s, gb_s, y_s, *, bb, tc, sb):
    c = pl.program_id(1)
    W = RWKV_WIDTH
    npair = RWKV_HEADS // 2

    @pl.when(c == 0)
    def _():
        for b in range(bb):
            carry_s[b] = jnp.broadcast_to(sh0_ref[b], (8, RWKV_PROJ))
        st_s[...] = s0_ref[...].reshape(st_s.shape)

    first_row = lax.broadcasted_iota(jnp.int32, (tc, 1), 0) == 0
    def head_sum(x):
        if x.shape[0] <= 64:
            return _lane_group_sum(x, RWKV_HEAD)
        hi = x.astype(BF16)
        lo = (x - hi.astype(F32)).astype(BF16)
        return (jnp.dot(hi, blk_ref[...], preferred_element_type=F32)
                + jnp.dot(lo, blk_ref[...], preferred_element_type=F32))

    def mix(z, off):
        prev = jnp.where(first_row, carry_s[b][0:1, off:off + z.shape[1]], pltpu.roll(z, 1, 0))
        return z + (prev - z) * mu_ref[:, off:off + z.shape[1]]

    for b in range(bb):
        zr, zk, zv, zl = zr_ref[b], zk_ref[b], zv_ref[b], zl_ref[b]
        r = mix(zr, 0)
        k = mix(zk, W)
        v = mix(zv, 2 * W)
        lo = mix(zl, 3 * W)
        carry_s[b] = jnp.concatenate([zr[tc - 8:], zk[tc - 8:], zv[tc - 8:], zl[tc - 8:]], axis=1)[7:8] \
            + jnp.zeros((8, RWKV_PROJ), F32)
        wa = lo[:, :128]
        lw = jnp.dot(jnp.tanh(wa).astype(BF16), w2_ref[...], preferred_element_type=F32)
        la = jnp.dot(wa.astype(BF16), a2_ref[...], preferred_element_type=F32)
        g = jnp.dot(jax.nn.sigmoid(lo[:, 128:]).astype(BF16), g2_ref[...], preferred_element_type=F32)
        log_w = -jnp.exp(-_softplus(-(w0_ref[...] + lw)) - 0.5)
        a = jax.nn.sigmoid(a0_ref[...] + la)
        kk = k * kk_ref[...]
        kk = kk / jnp.maximum(jnp.sqrt(head_sum(kk * kk)), 1e-12)
        k = k * (1.0 + (a - 1.0) * ka_ref[...])
        cw = jnp.dot(ltri_ref[...], log_w, precision=HIGHEST, preferred_element_type=F32)
        cl = jnp.dot(bones_ref[...], log_w, precision=HIGHEST, preferred_element_type=F32)
        inv_g = jnp.exp(-cw)
        to_end = jnp.exp(cl - cw)
        r_s[b], k_s[b], v_s[b], g_s[b] = r, k, v, g
        ae_s[b] = -kk * jnp.exp(cw - log_w)
        re_s[b] = r * jnp.exp(cw)
        bq_s[b] = kk * a * inv_g
        kq_s[b] = k * inv_g
        bh_s[b] = kk * a * to_end
        kh_s[b] = k * to_end
        gb_s[b] = jnp.exp(cl)

    lo_half = lax.broadcasted_iota(jnp.int32, (sb, 128), 1) < RWKV_HEAD
    lo_state = lax.broadcasted_iota(jnp.int32, (RWKV_HEAD, 128), 1) < RWKV_HEAD
    pairs = [(b, p) for b in range(bb) for p in range(npair)]
    npairs = len(pairs)
    crow = lax.broadcasted_iota(jnp.int32, (npairs * 4 * sb, npairs * 2 * sb), 0)
    ccol = lax.broadcasted_iota(jnp.int32, (npairs * 4 * sb, npairs * 2 * sb), 1)
    same_pair = lax.shift_right_logical(crow, (4 * sb).bit_length() - 1) \
        == lax.shift_right_logical(ccol, (2 * sb).bit_length() - 1)
    r_side = jnp.where(jnp.bitwise_and(crow, 2 * sb) != 0, 1, 0)
    causal = jnp.bitwise_and(crow, sb - 1) > jnp.bitwise_and(ccol, sb - 1) - r_side
    coef_mask = jnp.logical_and(same_pair, causal)
    lo_of = lambda x: jnp.where(lo_half, x, 0.0)
    hi_of = lambda x: jnp.where(lo_half, 0.0, x)
    zeros_sb = jnp.zeros((sb, 128), F32)
    zero_tile = jnp.zeros((2 * sb, 128), BF16)
    cat = jnp.concatenate

    def block(bi, carry):
        t0 = pl.multiple_of(bi * sb, sb)
        tile = lambda ref, i: ref[pairs[i][0], pl.ds(t0, sb), 128 * pairs[i][1]:128 * (pairs[i][1] + 1)]
        rng = range(npairs)
        aes, res, vbs = ([tile(ref, i) for i in rng] for ref in (ae_s, re_s, v_s))
        lhs = cat([cat([lo_of(aes[i]), hi_of(aes[i]), lo_of(res[i]), hi_of(res[i])], axis=0) for i in rng], axis=0)
        rhs = cat([cat([tile(bq_s, i), tile(kq_s, i)], axis=0) for i in rng], axis=0)
        coef = jnp.where(coef_mask, lax.dot_general(lhs.astype(BF16), rhs.astype(BF16), _NT,
                                                    preferred_element_type=F32), 0.0)
        rows_of = lambda k: cat([coef[(4 * i + k) * sb:(4 * i + k + 1) * sb] for i in rng], axis=0)
        v_rows = cat([cat([zeros_sb, lo_of(v)], axis=0) for v in vbs] + [cat([zeros_sb, hi_of(v)], axis=0) for v in vbs],
                     axis=0).astype(BF16)
        x_v = jnp.dot(cat([rows_of(0), rows_of(1)], axis=1).astype(BF16), v_rows, preferred_element_type=F32)
        states = [st_s[b * npair + p] for b, p in pairs]
        s_bd = cat([cat([jnp.where(lo_state, S, 0.0), jnp.where(lo_state, 0.0, S)], axis=0).astype(BF16)
                    for S in states], axis=1)
        ar_bd = cat([cat([zero_tile] * i + [cat([aes[i], res[i]], axis=0).astype(BF16)]
                         + [zero_tile] * (npairs - 1 - i), axis=1) for i in rng], axis=0)
        xy = lax.dot_general(ar_bd, s_bd, _NT, preferred_element_type=F32)
        xs = [xy[2 * sb * i:2 * sb * i + sb] + x_v[sb * i:sb * (i + 1)] for i in rng]
        cols = [[jnp.where(lo_half,
                           coef[4 * sb * i:4 * sb * i + sb, 2 * sb * i + u:2 * sb * i + u + 1],
                           coef[4 * sb * i + sb:4 * sb * i + 2 * sb, 2 * sb * i + u:2 * sb * i + u + 1])
                 for u in range(sb - 1)] for i in rng]
        for u in range(sb - 1):
            xs = [x + cols[i][u] * x[u:u + 1, :] for i, x in enumerate(xs)]
        pv_rows = cat([cat([lo_of(xs[i]), lo_of(vbs[i])], axis=0) for i in rng]
                      + [cat([hi_of(xs[i]), hi_of(vbs[i])], axis=0) for i in rng], axis=0).astype(BF16)
        y_add = jnp.dot(cat([rows_of(2), rows_of(3)], axis=1).astype(BF16), pv_rows, preferred_element_type=F32)
        for i, (b, p) in enumerate(pairs):
            x, vb = xs[i], vbs[i]
            y_s[b, pl.ds(t0, sb), 128 * p:128 * (p + 1)] = (xy[2 * sb * i + sb:2 * sb * (i + 1)]
                                                            + y_add[sb * i:sb * (i + 1)])
            upd = lax.dot_general(jnp.concatenate([x, vb], axis=0).astype(BF16),
                                  jnp.concatenate([tile(bh_s, i), tile(kh_s, i)], axis=0).astype(BF16),
                                  (((0,), (0,)), ((), ())), preferred_element_type=F32)
            st_s[b * npair + p] = (states[i] * tile(gb_s, i)[0:1, :]
                                   + jnp.where(lo_state, upd[0:RWKV_HEAD], upd[RWKV_HEAD:]))
        return carry

    lax.fori_loop(0, tc // sb, block, 0)

    for b in range(bb):
        y = y_s[b]
        mean = head_sum(y) * (1.0 / RWKV_HEAD)
        d = y - mean
        var = head_sum(d * d) * (1.0 / RWKV_HEAD)
        yn = d * lax.rsqrt(var + RWKV_GN_EPS) * lng_ref[...] + lnb_ref[...]
        bonus = head_sum(r_s[b] * k_s[b] * rk_ref[...]) * v_s[b]
        o_ref[b] = ((yn + bonus) * g_s[b]).astype(o_ref.dtype)

    @pl.when(c == pl.num_programs(1) - 1)
    def _():
        sout_ref[...] = st_s[...].reshape(sout_ref.shape)


def _pack_pairs(s):
    B, H, R, C = s.shape
    return s.reshape(B, H // 2, 2, R, C).transpose(0, 1, 3, 2, 4).reshape(B, H // 2, R, 2 * C)


def _unpack_pairs(s):
    B, HP, R, C2 = s.shape
    return s.reshape(B, HP, R, 2, C2 // 2).transpose(0, 1, 3, 2, 4).reshape(B, HP * 2, R, C2 // 2)


def _rwkv_call(st, z, shift0, s0, prm, out_dtype):
    B, T = st.B, st.T
    z3 = z.reshape(B, T, Z_WIDTH)
    bb, tc, sb = _scan_tiling(B, T, RWKV_BLOCK)
    W = RWKV_WIDTH
    npair = RWKV_HEADS // 2
    s0p = _pack_pairs(s0)
    zspec = lambda w, idx: pl.BlockSpec((bb, tc, w), lambda g, c: (g, c, idx))
    const = lambda a: pl.BlockSpec(a.shape, lambda g, c: (0,) * a.ndim)
    consts = [prm[k] for k in ("mu", "w0", "w2", "a0", "a2", "g2", "k_k", "k_a", "r_k", "lnx_g", "lnx_b")]
    head_of = jnp.arange(W, dtype=jnp.int32) // RWKV_HEAD
    consts += [(head_of[:, None] == head_of[None, :]).astype(BF16), *_block_cumsum_matrices(tc, sb)]
    o, sout = pl.pallas_call(
        functools.partial(_rwkv_body, bb=bb, tc=tc, sb=sb),
        grid=(B // bb, T // tc),
        in_specs=[
            zspec(W, Z_RWKV_R), zspec(W, Z_RWKV_K), zspec(W, Z_RWKV_V), zspec(256, Z_RWKV_LORA),
            pl.BlockSpec((bb, 1, RWKV_PROJ), lambda g, c: (g, 0, 0)),
            pl.BlockSpec((bb, npair, RWKV_HEAD, 128), lambda g, c: (g, 0, 0, 0)),
        ] + [const(a) for a in consts],
        out_specs=[
            pl.BlockSpec((bb, tc, W), lambda g, c: (g, c, 0)),
            pl.BlockSpec((bb, npair, RWKV_HEAD, 128), lambda g, c: (g, 0, 0, 0)),
        ],
        out_shape=[
            jax.ShapeDtypeStruct((B, T, W), out_dtype),
            jax.ShapeDtypeStruct((B, npair, RWKV_HEAD, 128), F32),
        ],
        scratch_shapes=[
            pltpu.VMEM((bb, 8, RWKV_PROJ), F32),
            pltpu.VMEM((bb * npair, RWKV_HEAD, 128), F32),
        ] + [pltpu.VMEM((bb, tc, W), F32)] * 12,
        compiler_params=_cp(("parallel", "arbitrary")),
        name="rwkv7",
    )(z3, z3, z3, z3, shift0.reshape(B, 1, RWKV_PROJ), s0p, *consts)
    return o.reshape(B * T, W), _unpack_pairs(sout)


def _gla_body(zqk_ref, zv_ref, zgr_ref, zgl_ref, s0_ref, wg_ref, bg_ref, ng_ref, ltri_ref, bones_ref,
              o_ref, sout_ref, st_s, q_s, k_s, cw_s, qe_s, ke_s, ab_s, v_s, y_s, *, bb, tc, sb):
    c = pl.program_id(1)
    npair = GLA_HEADS // 2
    HK = GLA_HEADS * GLA_DK

    @pl.when(c == 0)
    def _():
        st_s[...] = s0_ref[...].reshape(st_s.shape)

    for b in range(bb):
        zqk = zqk_ref[b]
        gate = jnp.dot(zgl_ref[b].astype(BF16), wg_ref[...], preferred_element_type=F32) + bg_ref[...]
        log_a = -_softplus(-gate) * (1.0 / GLA_TAU)
        cw = jnp.dot(ltri_ref[...], log_a, precision=HIGHEST, preferred_element_type=F32)
        cl = jnp.dot(bones_ref[...], log_a, precision=HIGHEST, preferred_element_type=F32)
        q = zqk[:, :HK] * (GLA_DK ** -0.5)
        k = zqk[:, HK:]
        q_s[b], k_s[b], cw_s[b], v_s[b] = q, k, cw, zv_ref[b]
        qe_s[b] = q * jnp.exp(cw)
        ke_s[b] = k * jnp.exp(cl - cw)
        ab_s[b] = jnp.exp(cl)

    lo_half = lax.broadcasted_iota(jnp.int32, (sb, 128), 1) < GLA_DK
    t_idx = lax.broadcasted_iota(jnp.int32, (sb, 128), 0)
    lo_of = lambda x: jnp.where(lo_half, x, 0.0)
    hi_of = lambda x: jnp.where(lo_half, 0.0, x)

    def block(bi, carry):
        t0 = pl.multiple_of(bi * sb, sb)
        for b in range(bb):
            for p in range(npair):
                cs = slice(128 * p, 128 * (p + 1))
                v0 = slice(2 * GLA_DV * p, 2 * GLA_DV * p + GLA_DV)
                v1 = slice(2 * GLA_DV * p + GLA_DV, 2 * GLA_DV * (p + 1))
                qb, kb, cwb, qeb, keb, abb = (x[b, pl.ds(t0, sb), cs] for x in (q_s, k_s, cw_s, qe_s, ke_s, ab_s))
                v0b = v_s[b, pl.ds(t0, sb), v0]
                v1b = v_s[b, pl.ds(t0, sb), v1]
                S = st_s[b * npair + p]
                inter = lax.dot_general(jnp.concatenate([lo_of(qeb), hi_of(qeb)], axis=0).astype(BF16),
                                        S.astype(BF16), _NT, preferred_element_type=F32)
                o0 = inter[0:sb]
                o1 = inter[sb:]
                for u in range(sb):
                    decay = jnp.exp(jnp.minimum(cwb - cwb[u:u + 1, :], 0.0))
                    w_tu = jnp.where(t_idx >= u, qb * decay * kb[u:u + 1, :], 0.0)
                    o0 = o0 + jnp.sum(lo_of(w_tu), -1, keepdims=True) * v0b[u:u + 1, :]
                    o1 = o1 + jnp.sum(hi_of(w_tu), -1, keepdims=True) * v1b[u:u + 1, :]
                y_s[b, pl.ds(t0, sb), v0] = o0
                y_s[b, pl.ds(t0, sb), v1] = o1
                upd = lax.dot_general(jnp.concatenate([v0b, v1b], axis=0).astype(BF16),
                                      jnp.concatenate([lo_of(keb), hi_of(keb)], axis=0).astype(BF16),
                                      (((0,), (0,)), ((), ())), preferred_element_type=F32)
                st_s[b * npair + p] = S * abb[0:1, :] + upd
        return carry

    lax.fori_loop(0, tc // sb, block, 0)

    for b in range(bb):
        gr = zgr_ref[b]
        for h in range(GLA_HEADS):
            hs = slice(h * GLA_DV, (h + 1) * GLA_DV)
            o = y_s[b, :, hs]
            on = o * lax.rsqrt(jnp.mean(o * o, -1, keepdims=True) + RMS_EPS) * ng_ref[...]
            o_ref[b, :, hs] = (on * _silu(gr[:, hs])).astype(o_ref.dtype)

    @pl.when(c == pl.num_programs(1) - 1)
    def _():
        sout_ref[...] = st_s[...].reshape(sout_ref.shape)


def _gla_call(st, z, s0, wg_p, bg, ng, out_dtype):
    B, T = st.B, st.T
    z3 = z.reshape(B, T, Z_WIDTH)
    bb, tc, sb = _scan_tiling(B, T, GLA_BLOCK)
    npair = GLA_HEADS // 2
    HK = GLA_HEADS * GLA_DK
    s0p = _pack_pairs(s0.transpose(0, 1, 3, 2))
    zspec = lambda w, idx: pl.BlockSpec((bb, tc, w), lambda g, c: (g, c, idx))
    const = lambda a: pl.BlockSpec(a.shape, lambda g, c: (0,) * a.ndim)
    ltri, bones = _block_cumsum_matrices(tc, sb)
    o, sout = pl.pallas_call(
        functools.partial(_gla_body, bb=bb, tc=tc, sb=sb),
        grid=(B // bb, T // tc),
        in_specs=[
            zspec(512, Z_GLA_QK), zspec(512, Z_GLA_V), zspec(512, Z_GLA_GR), zspec(128, Z_GLA_GL),
            pl.BlockSpec((bb, npair, GLA_DV, 128), lambda g, c: (g, 0, 0, 0)),
            const(wg_p), const(bg), const(ng), const(ltri), const(bones),
        ],
        out_specs=[
            pl.BlockSpec((bb, tc, GLA_WIDTH), lambda g, c: (g, c, 0)),
            pl.BlockSpec((bb, npair, GLA_DV, 128), lambda g, c: (g, 0, 0, 0)),
        ],
        out_shape=[
            jax.ShapeDtypeStruct((B, T, GLA_WIDTH), out_dtype),
            jax.ShapeDtypeStruct((B, npair, GLA_DV, 128), F32),
        ],
        scratch_shapes=[
            pltpu.VMEM((bb * npair, GLA_DV, 128), F32),
        ] + [pltpu.VMEM((bb, tc, HK), F32)] * 6 + [pltpu.VMEM((bb, tc, GLA_WIDTH), F32)] * 2,
        compiler_params=_cp(("parallel", "arbitrary")),
        name="gla",
    )(z3, z3, z3, z3, s0p, wg_p, bg, ng, ltri, bones)
    return o.reshape(B * T, GLA_WIDTH), _unpack_pairs(sout).transpose(0, 1, 3, 2)


def _outproj_body(om_ref, or_ref, og_ref, x_ref, g1_ref, sh2_ref, sc2_ref, w_ref, lng_ref, lnb_ref,
                  x1_ref, h2_ref, *, alpha):
    wm = MLA_HEADS * MLA_V
    mix = (jnp.dot(om_ref[...].astype(BF16), w_ref[0:wm, :], preferred_element_type=F32)
           + jnp.dot(or_ref[...].astype(BF16), w_ref[wm:wm + RWKV_WIDTH, :], preferred_element_type=F32)
           + jnp.dot(og_ref[...].astype(BF16), w_ref[wm + RWKV_WIDTH:, :], preferred_element_type=F32))
    x1 = _layernorm(alpha * x_ref[...] + (1.0 + g1_ref[...]) * mix, lng_ref[...], lnb_ref[...])
    x1_ref[...] = x1
    h2_ref[...] = x1 * (1.0 + sc2_ref[...]) + sh2_ref[...]


def _outproj_call(st, o_mla, o_rwkv, o_gla, x2d, l, w_out_b, ln_g, ln_b, alpha):
    N, D = x2d.shape
    bm = st.row_block(256)
    row = lambda w: pl.BlockSpec((bm, w), lambda i: (i, 0))
    return pl.pallas_call(
        functools.partial(_outproj_body, alpha=alpha),
        grid=(N // bm,),
        in_specs=[
            row(o_mla.shape[1]), row(o_rwkv.shape[1]), row(o_gla.shape[1]), row(D),
            st.mod_spec(l, 2, bm, D), st.mod_spec(l, 3, bm, D), st.mod_spec(l, 4, bm, D),
            pl.BlockSpec((None,) + w_out_b.shape[1:], lambda i: (l, 0, 0)),
            pl.BlockSpec((None, 1, D), lambda i: (l, 0, 0)),
            pl.BlockSpec((None, 1, D), lambda i: (l, 0, 0)),
        ],
        out_specs=[row(D), row(D)],
        out_shape=[jax.ShapeDtypeStruct((N, D), F32), jax.ShapeDtypeStruct((N, D), F32)],
        compiler_params=_cp(("parallel",)),
        name="out_proj",
    )(o_mla, o_rwkv, o_gla, x2d, st.mod, st.mod, st.mod, w_out_b, ln_g, ln_b)


def _router_body(x1_ref, sh2_ref, sc2_ref, wr_ref, o_ref):
    h = x1_ref[...] * (1.0 + sc2_ref[...]) + sh2_ref[...]
    logits = jnp.dot(h, wr_ref[...], precision=HIGHEST, preferred_element_type=F32)
    lane = lax.broadcasted_iota(jnp.int32, logits.shape, 1)
    lane_f = lane.astype(F32)
    lg = jnp.where(lane < N_EXPERTS, logits, -jnp.inf)
    v1 = jnp.max(lg, -1, keepdims=True)
    i1 = jnp.min(jnp.where(lg == v1, lane_f, 128.0), -1, keepdims=True)
    lg2 = jnp.where(lane_f == i1, -jnp.inf, lg)
    v2 = jnp.max(lg2, -1, keepdims=True)
    i2 = jnp.min(jnp.where(lg2 == v2, lane_f, 128.0), -1, keepdims=True)
    e = jnp.exp(v2 - v1)
    g1 = 1.0 / (1.0 + e)
    g2 = e * g1
    o_ref[...] = jnp.where(lane == 0, g1, jnp.where(lane == 1, g2, jnp.where(
        lane == 2, i1, jnp.where(lane == 3, i2, 0.0))))


def _router_call(st, x1, l, wr_p):
    N, D = x1.shape
    bm = st.row_block(512)
    return pl.pallas_call(
        _router_body,
        grid=(N // bm,),
        in_specs=[
            pl.BlockSpec((bm, D), lambda i: (i, 0)),
            st.mod_spec(l, 3, bm, D), st.mod_spec(l, 4, bm, D),
            pl.BlockSpec(wr_p.shape, lambda i: (0, 0)),
        ],
        out_specs=pl.BlockSpec((bm, 128), lambda i: (i, 0)),
        out_shape=jax.ShapeDtypeStruct((N, 128), F32),
        compiler_params=_cp(("parallel",)),
        name="moe_router",
    )(x1, st.mod, st.mod, wr_p)


def _ffn_up_body(te_ref, tv_ref, h_ref, w1_ref, w3_ref, o_ref):
    t = pl.program_id(0)

    @pl.when(tv_ref[t] != 0)
    def _():
        h = h_ref[...].astype(BF16)
        a = jnp.dot(h, w1_ref[...].astype(BF16), preferred_element_type=F32)
        b = jnp.dot(h, w3_ref[...].astype(BF16), preferred_element_type=F32)
        o_ref[...] = (_silu(a) * b).astype(BF16)

    @pl.when(tv_ref[t] == 0)
    def _():
        o_ref[...] = jnp.zeros(o_ref.shape, BF16)


def _ffn_down_body(te_ref, tv_ref, g_ref, w2_ref, o_ref):
    t = pl.program_id(0)

    @pl.when(tv_ref[t] != 0)
    def _():
        o_ref[...] = jnp.dot(g_ref[...], w2_ref[...].astype(BF16), preferred_element_type=F32)

    @pl.when(tv_ref[t] == 0)
    def _():
        o_ref[...] = jnp.zeros(o_ref.shape, F32)


def _ffn_up_gather_body(te_ref, tv_ref, src_ref, h_hbm, w1_ref, w3_ref, o_ref, rows_s, hb_s, sem):
    t = pl.program_id(0)
    bm = rows_s.shape[0]

    @pl.when(jnp.logical_and(pl.program_id(1) == 0, tv_ref[t] != 0))
    def _():
        def issue(r, carry):
            pltpu.make_async_copy(h_hbm.at[pl.ds(src_ref[t * bm + r], 1)], rows_s.at[pl.ds(r, 1)], sem).start()
            return carry

        lax.fori_loop(0, bm, issue, 0, unroll=8)
        pltpu.make_async_copy(h_hbm.at[pl.ds(0, bm)], rows_s, sem).wait()
        hb_s[...] = rows_s[...].astype(BF16)

    @pl.when(tv_ref[t] != 0)
    def _():
        h = hb_s[...]
        a = jnp.dot(h, w1_ref[...].astype(BF16), preferred_element_type=F32)
        b = jnp.dot(h, w3_ref[...].astype(BF16), preferred_element_type=F32)
        o_ref[...] = (_silu(a) * b).astype(BF16)

    @pl.when(tv_ref[t] == 0)
    def _():
        o_ref[...] = jnp.zeros(o_ref.shape, BF16)


def _ffn_call(h, tile_expert, tile_valid, w1, w3, w2, bm, src=None):
    D = h.shape[1]
    F = w1.shape[-1]
    R = h.shape[0] if src is None else src.shape[0]
    nt = R // bm
    bf = 512
    if src is None:
        up = pl.pallas_call(
            _ffn_up_body,
            grid_spec=pltpu.PrefetchScalarGridSpec(
                num_scalar_prefetch=2,
                grid=(nt, F // bf),
                in_specs=[
                    pl.BlockSpec((bm, D), lambda t, j, te, tv: (t, 0)),
                    pl.BlockSpec((None, D, bf), lambda t, j, te, tv: (te[t], 0, j)),
                    pl.BlockSpec((None, D, bf), lambda t, j, te, tv: (te[t], 0, j)),
                ],
                out_specs=pl.BlockSpec((bm, bf), lambda t, j, te, tv: (t, j)),
            ),
            out_shape=jax.ShapeDtypeStruct((R, F), BF16),
            compiler_params=_cp(("parallel", "arbitrary")),
            name="ffn_up",
        )(tile_expert, tile_valid, h, w1, w3)
    else:
        nj = F // bf
        up = pl.pallas_call(
            _ffn_up_gather_body,
            grid_spec=pltpu.PrefetchScalarGridSpec(
                num_scalar_prefetch=3,
                grid=(nt, F // bf),
                in_specs=[
                    pl.BlockSpec(memory_space=pl.ANY),
                    pl.BlockSpec((None, D, bf), lambda t, j, te, tv, sr: (te[t], 0, jnp.where(tv[t] != 0, j, nj - 1))),
                    pl.BlockSpec((None, D, bf), lambda t, j, te, tv, sr: (te[t], 0, jnp.where(tv[t] != 0, j, nj - 1))),
                ],
                out_specs=pl.BlockSpec((bm, bf), lambda t, j, te, tv, sr: (t, j)),
                scratch_shapes=[pltpu.VMEM((bm, D), h.dtype), pltpu.VMEM((bm, D), BF16),
                                pltpu.SemaphoreType.DMA(())],
            ),
            out_shape=jax.ShapeDtypeStruct((R, F), BF16),
            compiler_params=_cp(("arbitrary", "arbitrary")),
            name="ffn_up_gather",
        )(tile_expert, tile_valid, src, h, w1, w3)
    bn = min(512 if bm <= 768 else 256, D)
    assert D % bn == 0 and F % bf == 0
    return pl.pallas_call(
        _ffn_down_body,
        grid_spec=pltpu.PrefetchScalarGridSpec(
            num_scalar_prefetch=2,
            grid=(nt, D // bn),
            in_specs=[
                pl.BlockSpec((bm, F), lambda t, n, te, tv: (t, 0)),
                pl.BlockSpec((None, F, bn), lambda t, n, te, tv: (te[t], 0, jnp.where(tv[t] != 0, n, D // bn - 1))),
            ],
            out_specs=pl.BlockSpec((bm, bn), lambda t, n, te, tv: (t, n)),
        ),
        out_shape=jax.ShapeDtypeStruct((R, D), F32),
        compiler_params=_cp(("parallel", "arbitrary")),
        name="ffn_down",
    )(tile_expert, tile_valid, up, w2)


def _combine_body(*refs, alpha, nterm):
    x1_ref, g2_ref, lng_ref, lnb_ref = refs[:4]
    f_refs = refs[4:4 + nterm]
    o_ref = refs[-1]
    if nterm == 1:
        f = f_refs[0][...]
    else:
        gates = refs[4 + nterm][...]
        f = f_refs[0][...] * gates[:, 0:1] + f_refs[1][...] * gates[:, 1:2]
    o_ref[...] = _layernorm(alpha * x1_ref[...] + (1.0 + g2_ref[...]) * f, lng_ref[...], lnb_ref[...])


def _combine_call(st, x1, l, ln_g, ln_b, alpha, terms, gates=None, row0=0):
    N, D = x1.shape
    bm = st.row_block(512)
    assert row0 % bm == 0
    row = lambda w: pl.BlockSpec((bm, w), lambda i: (i, 0))
    shared = lambda w: pl.BlockSpec((bm, w), lambda i: (i + row0 // bm, 0))
    args = [x1, st.mod, ln_g, ln_b] + list(terms) + ([gates] if gates is not None else [])
    return pl.pallas_call(
        functools.partial(_combine_body, alpha=alpha, nterm=len(terms)),
        grid=(N // bm,),
        in_specs=[
            row(D), st.mod_spec(l, 5, bm, D),
            pl.BlockSpec((None, 1, D), lambda i: (l, 0, 0)),
            pl.BlockSpec((None, 1, D), lambda i: (l, 0, 0)),
        ] + [shared(D)] * len(terms) + ([shared(128)] if gates is not None else []),
        out_specs=row(D),
        out_shape=jax.ShapeDtypeStruct((N, D), F32),
        compiler_params=_cp(("parallel",)),
        name="ffn_residual",
    )(*args)


def _repack_w_in(w_in):
    L, D, _ = w_in.shape
    w = w_in.astype(BF16)
    zeros = lambda n: jnp.zeros((L, D, n), BF16)
    r0 = MLA_IN
    g0 = MLA_IN + RWKV_PROJ
    HK = GLA_HEADS * GLA_DK
    gq, gk, gv = g0, g0 + HK, g0 + 2 * HK
    ggl = gv + GLA_WIDTH
    ggr = ggl + GLA_GATE_LORA
    parts = [
        w[:, :, :MLA_IN], zeros(Z_MLA_W - MLA_IN),
        w[:, :, r0:r0 + RWKV_PROJ],
        w[:, :, ggl:ggr], zeros(256 - GLA_GATE_LORA),
        w[:, :, gq:gv], w[:, :, gv:ggl], w[:, :, ggr:ggr + GLA_WIDTH],
    ]
    out = jnp.concatenate(parts, axis=-1)
    assert out.shape[-1] == Z_WIDTH
    return out


def _moe_plan(gi, bm):
    n = gi.shape[0]
    E = N_EXPERTS
    experts = gi[:, 2:4].astype(jnp.int32).reshape(-1)
    onehot = (experts[:, None] == jnp.arange(E, dtype=jnp.int32)[None, :]).astype(jnp.int32)
    counts = jnp.sum(onehot, axis=0)
    tiles = (counts + bm - 1) // bm
    tile_end = jnp.cumsum(tiles)
    tile_start = tile_end - tiles
    rank = jnp.take_along_axis(jnp.cumsum(onehot, axis=0) - 1, experts[:, None], axis=1)[:, 0]
    pos = tile_start[experts] * bm + rank
    nt = (2 * n + bm - 1) // bm + E
    src = jnp.zeros((nt * bm,), jnp.int32).at[pos].set(jnp.arange(2 * n, dtype=jnp.int32) // 2)
    t_ids = jnp.arange(nt, dtype=jnp.int32)
    tile_expert = jnp.minimum(jnp.sum((t_ids[:, None] >= tile_end[None, :]).astype(jnp.int32), axis=1), E - 1)
    tile_valid = (t_ids < tile_end[-1]).astype(jnp.int32)
    tile_expert = jnp.where(tile_valid != 0, tile_expert, tile_expert[jnp.maximum(tile_end[-1] - 1, 0)])
    return src, pos.reshape(n, 2), tile_expert, tile_valid


def kernel(x_prompt, x_sample, cache_kv_latent, cache_k_rope, state_rwkv, state_rwkv_shift, state_gla, page_table, c_prompt, c_sample, w_in, w_out, mla_q_norm, mla_kv_norm, mla_w_q_up, mla_w_uk, mla_w_uv, rwkv_mu, rwkv_w0, rwkv_w2, rwkv_a0, rwkv_a2, rwkv_g2, rwkv_k_k, rwkv_k_a, rwkv_r_k, rwkv_lnx_g, rwkv_lnx_b, gla_w_g2, gla_b_g, gla_norm_g, ada_w, ada_b, ln1_g, ln1_b, ln2_g, ln2_b, ffn_w1, ffn_w3, ffn_w2, moe_router, moe_w1, moe_w3, moe_w2):
    Bp, Tp, D = x_prompt.shape
    Bs, Ts, _ = x_sample.shape
    L = w_in.shape[0]
    past_len = page_table.shape[1] * PAGE_SIZE
    alpha = (2 * L) ** 0.25
    dt = x_prompt.dtype

    n_c = Bp + Bs
    c_all = jnp.concatenate([c_prompt, c_sample, jnp.zeros((-n_c % 8, D), F32)], axis=0)
    mod = _ada_call(c_all, ada_w, ada_b)
    st_p = _Stream(Bp, Tp, mod[:, :Bp].reshape(L, Bp, 1, 6 * D), False, 0)
    st_s = _Stream(Bs, Ts, jnp.repeat(mod[:, Bp:Bp + Bs], Ts, axis=1), True, past_len)

    w_in_p = _repack_w_in(w_in)
    w_out_b = w_out.astype(BF16)
    wq = mla_w_q_up.astype(BF16)
    wq_p = jnp.concatenate([wq[..., :MLA_NOPE].reshape(L, MLA_Q_RANK, -1),
                            wq[..., MLA_NOPE:].reshape(L, MLA_Q_RANK, -1)], axis=-1)
    wuk_p = mla_w_uk.astype(BF16).transpose(0, 2, 3, 1)
    wuv_p = mla_w_uv.astype(BF16).transpose(0, 2, 1, 3)
    inv = 1.0 / (ROPE_BASE ** (jnp.arange(0, MLA_ROPE, 2, dtype=F32) / MLA_ROPE))
    inv128 = jnp.tile(inv, 4).reshape(1, 128)
    zpad = lambda a, n: jnp.concatenate([a, jnp.zeros((L, n) + a.shape[2:], a.dtype)], axis=1)
    zpre = lambda a, n: jnp.concatenate([jnp.zeros((L, n) + a.shape[2:], a.dtype), a], axis=1)
    rw_w2 = zpad(rwkv_w2.astype(BF16), RWKV_A_LORA)
    rw_a2 = zpre(rwkv_a2.astype(BF16), RWKV_W_LORA)
    gla_wg = zpad(gla_w_g2.astype(BF16), 128 - GLA_GATE_LORA)
    row1 = lambda a: a.reshape(1, -1)

    xp = x_prompt.reshape(Bp * Tp, D)
    xs = x_sample.reshape(Bs * Ts, D)
    zeros_shift = jnp.zeros((Bp, RWKV_PROJ), dt)
    zeros_rwkv = jnp.zeros((Bp, RWKV_HEADS, RWKV_HEAD, RWKV_HEAD), F32)
    zeros_gla = jnp.zeros((Bp, GLA_HEADS, GLA_DK, GLA_DV), F32)
    outs_p = [[] for _ in range(5)]
    outs_s = [[] for _ in range(5)]

    for l in range(L):
        rw = dict(mu=row1(rwkv_mu[l]), w0=row1(rwkv_w0[l]), w2=rw_w2[l], a0=row1(rwkv_a0[l]), a2=rw_a2[l],
                  g2=rwkv_g2[l].astype(BF16), k_k=row1(rwkv_k_k[l]), k_a=row1(rwkv_k_a[l]),
                  r_k=row1(rwkv_r_k[l]), lnx_g=row1(rwkv_lnx_g[l]), lnx_b=row1(rwkv_lnx_b[l]))
        streams = []
        for st, x2d, sample in ((st_p, xp, False), (st_s, xs, True)):
            z = _inproj_call(st, x2d, l, w_in_p)
            qlat, qpe, lat, kpe = _mla_prep_call(st, z, row1(mla_q_norm[l]), row1(mla_kv_norm[l]),
                                                 wq_p[l], wuk_p[l], inv128, F32 if sample else BF16)
            if sample:
                o_mla = _attn_sample_call(st, l, qlat, qpe, lat, kpe, cache_kv_latent, cache_k_rope,
                                          page_table, wuv_p[l])
                shift0, s_r0, s_g0 = state_rwkv_shift[l], state_rwkv[l], state_gla[l]
            else:
                o_mla = _attn_prompt_call(st, qlat, qpe, lat, kpe, wuv_p[l])
                shift0, s_r0, s_g0 = zeros_shift, zeros_rwkv, zeros_gla
            o_dt = F32 if sample else BF16
            o_rwkv, s_r = _rwkv_call(st, z, shift0, s_r0, rw, o_dt)
            o_gla, s_g = _gla_call(st, z, s_g0, gla_wg[l], row1(gla_b_g[l]), row1(gla_norm_g[l]), o_dt)
            x1, h2 = _outproj_call(st, o_mla, o_rwkv, o_gla, x2d, l, w_out_b, ln1_g.reshape(L, 1, D),
                                   ln1_b.reshape(L, 1, D), alpha)
            shift = z.reshape(st.B, st.T, Z_WIDTH)[:, -1, Z_MLA_W:Z_MLA_W + RWKV_PROJ]
            acc = outs_s if sample else outs_p
            for lst, val in zip(acc, (lat.reshape(st.B, st.T, -1), kpe.reshape(st.B, st.T, -1),
                                      s_r.astype(dt), shift, s_g.astype(dt))):
                lst.append(val)
            streams.append((st, x1, h2))

        (_, x1p, h2p), (_, x1s, h2s) = streams
        h_all = jnp.concatenate([h2p, h2s], axis=0)
        n_all = h_all.shape[0]
        n_p = h2p.shape[0]
        bm = _pick(n_all, 1024)
        lng, lnb = ln2_g.reshape(L, 1, D), ln2_b.reshape(L, 1, D)
        if l % 2 == 0:
            e = l // 2
            nt = n_all // bm
            f = _ffn_call(h_all, jnp.zeros((nt,), jnp.int32), jnp.ones((nt,), jnp.int32),
                          ffn_w1[e:e + 1], ffn_w3[e:e + 1], ffn_w2[e:e + 1], bm)
            xp = _combine_call(st_p, x1p, l, lng, lnb, alpha, [f])
            xs = _combine_call(st_s, x1s, l, lng, lnb, alpha, [f], row0=n_p)
        else:
            e = l // 2
            wr_p = jnp.concatenate([moe_router[e], jnp.zeros((D, 128 - N_EXPERTS), F32)], axis=1)
            gi = jnp.concatenate([_router_call(st_p, x1p, l, wr_p), _router_call(st_s, x1s, l, wr_p)], axis=0)
            src, pos, tile_expert, tile_valid = _moe_plan(gi, MOE_TILE_ROWS)
            f_sorted = _ffn_call(h_all, tile_expert, tile_valid, moe_w1[e], moe_w3[e], moe_w2[e], MOE_TILE_ROWS,
                                 src=src)
            f0 = f_sorted[pos[:, 0]]
            f1 = f_sorted[pos[:, 1]]
            xp = _combine_call(st_p, x1p, l, lng, lnb, alpha, [f0, f1], gi)
            xs = _combine_call(st_s, x1s, l, lng, lnb, alpha, [f0, f1], gi, row0=n_p)

    stack = lambda lst: jnp.stack(lst)
    return (xp.reshape(Bp, Tp, D), xs.reshape(Bs, Ts, D),
            *[stack(v) for v in outs_p], *[stack(v) for v in outs_s])
```

```python
import functools

import jax
import jax.numpy as jnp
from jax import lax
from jax.experimental import pallas as pl
from jax.experimental.pallas import tpu as pltpu

F32 = jnp.float32
BF16 = jnp.bfloat16
HIGHEST = lax.Precision.HIGHEST

PAGE_SIZE = 128
MLA_HEADS = 8
MLA_NOPE = 128
MLA_ROPE = 64
MLA_V = 128
MLA_Q_RANK = 512
MLA_KV_RANK = 256
MLA_IN = MLA_Q_RANK + MLA_KV_RANK + MLA_ROPE
MLA_SCALE = (MLA_NOPE + MLA_ROPE) ** -0.5
ROPE_BASE = 10000.0
RWKV_HEADS = 8
RWKV_HEAD = 64
RWKV_WIDTH = RWKV_HEADS * RWKV_HEAD
RWKV_W_LORA = 64
RWKV_A_LORA = 64
RWKV_G_LORA = 128
RWKV_PROJ = 3 * RWKV_WIDTH + RWKV_W_LORA + RWKV_A_LORA + RWKV_G_LORA
RWKV_GN_EPS = 64e-5
GLA_HEADS = 4
GLA_DK = 64
GLA_DV = 128
GLA_WIDTH = GLA_HEADS * GLA_DV
GLA_GATE_LORA = 16
GLA_TAU = 16.0
GLA_PROJ = 2 * GLA_HEADS * GLA_DK + 2 * GLA_WIDTH + GLA_GATE_LORA
N_EXPERTS = 8
LN_EPS = 1e-5
RMS_EPS = 1e-6

Z_WIDTH = 4608
Z_MLA_W = 1024
Z_RWKV_R, Z_RWKV_K, Z_RWKV_V = 2, 3, 4
Z_RWKV_LORA = 10
Z_GLA_GL = 22
Z_GLA_QK, Z_GLA_V, Z_GLA_GR = 6, 7, 8

MOE_TILE_ROWS = 768
RWKV_BLOCK = 16
GLA_BLOCK = 16
SCAN_BATCH = 4

VMEM_LIMIT_MB = 56


def _cp(sem, vmem_mb=VMEM_LIMIT_MB):
    return pltpu.CompilerParams(dimension_semantics=sem, vmem_limit_bytes=vmem_mb * 2**20)


def _pick(n, pref):
    if n <= pref:
        return n
    b = pref - pref % 8
    while b >= 8:
        if n % b == 0:
            return b
        b -= 8
    return n


def _silu(x):
    return x * jax.nn.sigmoid(x)


def _softplus(u):
    return jnp.maximum(u, 0.0) + jnp.log(1.0 + jnp.exp(-jnp.abs(u)))


def _lane_group_sum(x, group):
    axis = x.ndim - 1
    width = x.shape[axis]
    lane = lax.broadcasted_iota(jnp.int32, x.shape, axis)
    s = 1
    while s < group:
        partner = jnp.where(jnp.bitwise_and(lane, s) == 0,
                            pltpu.roll(x, width - s, axis), pltpu.roll(x, s, axis))
        x = x + partner
        s *= 2
    return x


def _layernorm(y, g, b):
    mu = jnp.mean(y, -1, keepdims=True)
    d = y - mu
    var = jnp.mean(d * d, -1, keepdims=True)
    return d * lax.rsqrt(var + LN_EPS) * g + b


def _ada_body(c_ref, w_ref, b_ref, o_ref):
    c = c_ref[...]
    s = _silu(c).astype(BF16)
    o_ref[...] = jnp.dot(s, w_ref[...].astype(BF16), preferred_element_type=F32) + b_ref[...]


def _ada_call(c_all, ada_w, ada_b):
    L, D, N6 = ada_w.shape
    Mc = c_all.shape[0]
    bn = 1024
    return pl.pallas_call(
        _ada_body,
        grid=(L, N6 // bn),
        in_specs=[
            pl.BlockSpec((Mc, D), lambda l, j: (0, 0)),
            pl.BlockSpec((None, D, bn), lambda l, j: (l, 0, j)),
            pl.BlockSpec((None, 1, bn), lambda l, j: (l, 0, j)),
        ],
        out_specs=pl.BlockSpec((None, Mc, bn), lambda l, j: (l, 0, j)),
        out_shape=jax.ShapeDtypeStruct((L, Mc, N6), F32),
        compiler_params=_cp(("parallel", "parallel")),
        name="ada_mod",
    )(c_all, ada_w, ada_b.reshape(L, 1, N6))


class _Stream:
    def __init__(self, B, T, mod, per_row, pos_base):
        self.B, self.T, self.N = B, T, B * T
        self.mod = mod
        self.per_row = per_row
        self.pos_base = pos_base

    def mod_spec(self, l, k, bm, D):
        if self.per_row:
            return pl.BlockSpec((None, bm, D), lambda i, *_: (l, i, k))
        nb = self.T // bm
        return pl.BlockSpec((None, None, 1, D), lambda i, *_: (l, i // nb, 0, k))

    def row_block(self, pref):
        return _pick(self.N if self.per_row else self.T, pref)


def _inproj_body(x_ref, sh_ref, sc_ref, w_ref, o_ref, h_s):
    @pl.when(pl.program_id(1) == 0)
    def _():
        h_s[...] = (x_ref[...] * (1.0 + sc_ref[...]) + sh_ref[...]).astype(BF16)

    o_ref[...] = jnp.dot(h_s[...], w_ref[...], preferred_element_type=F32)


def _inproj_call(st, x2d, l, w_in_p):
    N, D = x2d.shape
    Wz = w_in_p.shape[-1]
    bm = st.row_block(1024)
    bn = 1536 if Wz % 1536 == 0 else 512
    return pl.pallas_call(
        _inproj_body,
        grid=(N // bm, Wz // bn),
        in_specs=[
            pl.BlockSpec((bm, D), lambda i, j: (i, 0)),
            st.mod_spec(l, 0, bm, D),
            st.mod_spec(l, 1, bm, D),
            pl.BlockSpec((None, D, bn), lambda i, j: (l, 0, j)),
        ],
        out_specs=pl.BlockSpec((bm, bn), lambda i, j: (i, j)),
        out_shape=jax.ShapeDtypeStruct((N, Wz), F32),
        scratch_shapes=[pltpu.VMEM((bm, D), BF16)],
        compiler_params=_cp(("parallel", "arbitrary")),
        name="in_proj",
    )(x2d, st.mod, st.mod, w_in_p)


def _rope(x, cos, sin_signed, first_half):
    w = x.shape[1]
    swapped = jnp.where(first_half, pltpu.roll(x, w - 32, 1), pltpu.roll(x, 32, 1))
    return x * cos + swapped * sin_signed


def _mla_prep_body(z_ref, gq_ref, gkv_ref, wq_ref, wuk_ref, inv_ref, qlat_ref, qpe_ref, lat_ref, kpe_ref,
                   *, bm, period, pos_base):
    i = pl.program_id(0)
    z = z_ref[...]
    zq = z[:, :MLA_Q_RANK]
    zkv = z[:, MLA_Q_RANK:MLA_Q_RANK + MLA_KV_RANK]
    zpe = z[:, MLA_Q_RANK + MLA_KV_RANK:MLA_Q_RANK + MLA_KV_RANK + 128]
    qn = zq * lax.rsqrt(jnp.mean(zq * zq, -1, keepdims=True) + RMS_EPS) * gq_ref[...]
    q = jnp.dot(qn.astype(BF16), wq_ref[...], preferred_element_type=F32)
    lat_ref[...] = zkv * lax.rsqrt(jnp.mean(zkv * zkv, -1, keepdims=True) + RMS_EPS) * gkv_ref[...]

    row = lax.broadcasted_iota(jnp.int32, (bm, 128), 0) + i * bm
    pos = (pos_base + jnp.bitwise_and(row, period - 1)).astype(F32)
    ang = pos * inv_ref[...]
    cos = jnp.cos(ang)
    sin = jnp.sin(ang)
    first = jnp.bitwise_and(lax.broadcasted_iota(jnp.int32, (bm, 128), 1), 63) < 32
    sin_s = jnp.where(first, -sin, sin)
    kpe_ref[...] = _rope(zpe, cos, sin_s, first)[:, :MLA_ROPE]

    npe = MLA_HEADS * MLA_ROPE // 128
    cos4 = jnp.concatenate([cos] * npe, axis=1)
    sin4 = jnp.concatenate([sin_s] * npe, axis=1)
    first4 = jnp.bitwise_and(lax.broadcasted_iota(jnp.int32, (bm, 128 * npe), 1), 63) < 32
    q_pe = q[:, MLA_HEADS * MLA_NOPE:]
    qpe_ref[...] = (_rope(q_pe, cos4, sin4, first4) * MLA_SCALE).astype(qpe_ref.dtype)
    for h in range(MLA_HEADS):
        qh = q[:, h * MLA_NOPE:(h + 1) * MLA_NOPE].astype(BF16)
        ql = jnp.dot(qh, wuk_ref[h], preferred_element_type=F32) * MLA_SCALE
        qlat_ref[:, h * MLA_KV_RANK:(h + 1) * MLA_KV_RANK] = ql.astype(qlat_ref.dtype)


def _mla_prep_call(st, z, gq, gkv, wq_p, wuk_p, inv128, q_dtype):
    N = z.shape[0]
    bm = _pick(N, 512)
    HR = MLA_HEADS * MLA_KV_RANK
    HP = MLA_HEADS * MLA_ROPE
    assert st.T & (st.T - 1) == 0
    body = functools.partial(_mla_prep_body, bm=bm, period=st.T, pos_base=st.pos_base)
    const = lambda shape: pl.BlockSpec(shape, lambda i: (0,) * len(shape))
    return pl.pallas_call(
        body,
        grid=(N // bm,),
        in_specs=[
            pl.BlockSpec((bm, Z_MLA_W), lambda i: (i, 0)),
            const((1, MLA_Q_RANK)), const((1, MLA_KV_RANK)),
            const(wq_p.shape), const(wuk_p.shape), const((1, 128)),
        ],
        out_specs=[
            pl.BlockSpec((bm, HR), lambda i: (i, 0)),
            pl.BlockSpec((bm, HP), lambda i: (i, 0)),
            pl.BlockSpec((bm, MLA_KV_RANK), lambda i: (i, 0)),
            pl.BlockSpec((bm, MLA_ROPE), lambda i: (i, 0)),
        ],
        out_shape=[
            jax.ShapeDtypeStruct((N, HR), q_dtype),
            jax.ShapeDtypeStruct((N, HP), q_dtype),
            jax.ShapeDtypeStruct((N, MLA_KV_RANK), F32),
            jax.ShapeDtypeStruct((N, MLA_ROPE), F32),
        ],
        compiler_params=_cp(("parallel",)),
        name="mla_prep",
    )(z, gq, gkv, wq_p, wuk_p, inv128)


_NT = (((1,), (1,)), ((), ()))


def _softmax_update(s, h, m_s, l_s, acc_s, values_bf16):
    m_prev = m_s[h]
    m_new = jnp.maximum(m_prev, jnp.max(s, -1, keepdims=True))
    alpha = jnp.exp(m_prev - m_new)
    p = jnp.exp(s - m_new[:, :1])
    l_s[h] = alpha * l_s[h] + jnp.sum(p, -1, keepdims=True)
    acc_s[h] = acc_s[h] * alpha[:, :1] + jnp.dot(p.astype(BF16), values_bf16, preferred_element_type=F32)
    m_s[h] = m_new


def _attn_prompt_body(ql_ref, qp_ref, lat_ref, kpe_ref, wuv_ref, o_ref, m_s, l_s, acc_s, *, bq):
    qi = pl.program_id(1)
    ki = pl.program_id(2)

    @pl.when(ki == 0)
    def _():
        m_s[...] = jnp.full(m_s.shape, -jnp.inf, F32)
        l_s[...] = jnp.zeros(l_s.shape, F32)
        acc_s[...] = jnp.zeros(acc_s.shape, F32)

    def step(masked):
        latb = lat_ref[...].astype(BF16)
        kpb = kpe_ref[...].astype(BF16)
        if masked:
            causal = (lax.broadcasted_iota(jnp.int32, (bq, bq), 0)
                      >= lax.broadcasted_iota(jnp.int32, (bq, bq), 1))
        for h in range(MLA_HEADS):
            s = (lax.dot_general(ql_ref[:, h * MLA_KV_RANK:(h + 1) * MLA_KV_RANK], latb, _NT,
                                 preferred_element_type=F32)
                 + lax.dot_general(qp_ref[:, h * MLA_ROPE:(h + 1) * MLA_ROPE], kpb, _NT,
                                   preferred_element_type=F32))
            if masked:
                s = jnp.where(causal, s, -jnp.inf)
            _softmax_update(s, h, m_s, l_s, acc_s, latb)

    @pl.when(ki < qi)
    def _():
        step(False)

    @pl.when(ki == qi)
    def _():
        step(True)
        for h in range(MLA_HEADS):
            o_lat = (acc_s[h] / l_s[h][:, :1]).astype(BF16)
            o_ref[:, h * MLA_V:(h + 1) * MLA_V] = jnp.dot(
                o_lat, wuv_ref[h], preferred_element_type=F32).astype(o_ref.dtype)


def _attn_prompt_call(st, qlat, qpe, lat, kpe, wuv_p):
    B, T = st.B, st.T
    bq = _pick(T, 512)
    nq = T // bq
    HR = MLA_HEADS * MLA_KV_RANK
    HP = MLA_HEADS * MLA_ROPE
    kv = lambda b, qi, ki: (b * nq + jnp.minimum(ki, qi), 0)
    return pl.pallas_call(
        functools.partial(_attn_prompt_body, bq=bq),
        grid=(B, nq, nq),
        in_specs=[
            pl.BlockSpec((bq, HR), lambda b, qi, ki: (b * nq + qi, 0)),
            pl.BlockSpec((bq, HP), lambda b, qi, ki: (b * nq + qi, 0)),
            pl.BlockSpec((bq, MLA_KV_RANK), kv),
            pl.BlockSpec((bq, MLA_ROPE), kv),
            pl.BlockSpec(wuv_p.shape, lambda b, qi, ki: (0, 0, 0)),
        ],
        out_specs=pl.BlockSpec((bq, MLA_HEADS * MLA_V), lambda b, qi, ki: (b * nq + qi, 0)),
        out_shape=jax.ShapeDtypeStruct((B * T, MLA_HEADS * MLA_V), BF16),
        scratch_shapes=[
            pltpu.VMEM((MLA_HEADS, bq, 128), F32),
            pltpu.VMEM((MLA_HEADS, bq, 128), F32),
            pltpu.VMEM((MLA_HEADS, bq, MLA_KV_RANK), F32),
        ],
        compiler_params=_cp(("parallel", "parallel", "arbitrary")),
        name="attn_prompt",
    )(qlat, qpe, lat, kpe, wuv_p)


def _attn_sample_body(pt_ref, ql_ref, qp_ref, latn_ref, kpen_ref, wuv_ref, cache_lat, cache_pe_t, o_ref,
                      lat_buf, pe_buf, sem, *, layer, npages, ts):
    s_id = pl.program_id(0)
    slot = jnp.bitwise_and(s_id, 1)
    rows = MLA_HEADS * ts

    def start_pages(seq, slot_):
        def body(g, carry):
            page = pt_ref[seq, g]
            pltpu.make_async_copy(cache_lat.at[layer, page], lat_buf.at[slot_, g], sem.at[slot_]).start()
            pltpu.make_async_copy(cache_pe_t.at[layer, page], pe_buf.at[slot_, g], sem.at[slot_]).start()
            return carry

        lax.fori_loop(0, npages, body, 0, unroll=4)

    @pl.when(s_id == 0)
    def _():
        start_pages(0, 0)

    pltpu.make_async_copy(cache_lat.at[layer, pl.ds(0, npages)], lat_buf.at[slot], sem.at[slot]).wait()
    pltpu.make_async_copy(cache_pe_t.at[layer, pl.ds(0, npages)], pe_buf.at[slot], sem.at[slot]).wait()

    @pl.when(s_id + 1 < pl.num_programs(0))
    def _():
        start_pages(s_id + 1, 1 - slot)

    qlb = jnp.concatenate([ql_ref[:, h * MLA_KV_RANK:(h + 1) * MLA_KV_RANK] for h in range(MLA_HEADS)],
                          axis=0).astype(BF16)
    qpb = jnp.concatenate([qp_ref[:, h * MLA_ROPE:(h + 1) * MLA_ROPE] for h in range(MLA_HEADS)],
                          axis=0).astype(BF16)

    def scores(latb, kpe_t):
        return (lax.dot_general(qlb, latb, _NT, preferred_element_type=F32)
                + jnp.dot(qpb, kpe_t, preferred_element_type=F32))

    lats = [lat_buf[slot, g].astype(BF16) for g in range(npages)]
    s_pages = [scores(lats[g], pe_buf[slot, g].astype(BF16)) for g in range(npages)]
    pad = PAGE_SIZE - ts
    lat_new = jnp.concatenate([latn_ref[...], jnp.zeros((pad, MLA_KV_RANK), F32)], axis=0).astype(BF16)
    kpe_new = jnp.concatenate([kpen_ref[...], jnp.zeros((pad, MLA_ROPE), F32)], axis=0).T.astype(BF16)
    t_q = jnp.bitwise_and(lax.broadcasted_iota(jnp.int32, (rows, PAGE_SIZE), 0), ts - 1)
    t_k = lax.broadcasted_iota(jnp.int32, (rows, PAGE_SIZE), 1)
    s_new = jnp.where(t_k <= t_q, scores(lat_new, kpe_new), -jnp.inf)
    s_all = jnp.concatenate(s_pages + [s_new], axis=1)
    p = jnp.exp(s_all - jnp.max(s_all, -1, keepdims=True))
    denom = jnp.sum(p, -1, keepdims=True)
    pv = jnp.zeros((rows, MLA_KV_RANK), F32)
    for g, v in enumerate(lats + [lat_new]):
        pv = pv + jnp.dot(p[:, g * PAGE_SIZE:(g + 1) * PAGE_SIZE].astype(BF16), v, preferred_element_type=F32)
    o_lat = (pv / denom).astype(BF16)
    for h in range(MLA_HEADS):
        o_h = jnp.dot(o_lat, wuv_ref[h], preferred_element_type=F32)
        o_ref[:, h * MLA_V:(h + 1) * MLA_V] = o_h[h * ts:(h + 1) * ts, :].astype(o_ref.dtype)


def _attn_sample_call(st, l, qlat, qpe, lat, kpe, cache_lat, cache_pe, page_table, wuv_p):
    S, ts = st.B, st.T
    P = page_table.shape[1]
    HR = MLA_HEADS * MLA_KV_RANK
    HP = MLA_HEADS * MLA_ROPE
    assert ts & (ts - 1) == 0
    page_bytes = PAGE_SIZE * (MLA_KV_RANK + MLA_ROPE) * 4
    assert 2 * P * page_bytes <= 32 * 2**20, "the whole past of one sequence must fit the two page slots"
    cache_pe_t = jnp.swapaxes(cache_pe, 2, 3)
    grid_spec = pltpu.PrefetchScalarGridSpec(
        num_scalar_prefetch=1,
        grid=(S,),
        in_specs=[
            pl.BlockSpec((ts, HR), lambda s, pt: (s, 0)),
            pl.BlockSpec((ts, HP), lambda s, pt: (s, 0)),
            pl.BlockSpec((ts, MLA_KV_RANK), lambda s, pt: (s, 0)),
            pl.BlockSpec((ts, MLA_ROPE), lambda s, pt: (s, 0)),
            pl.BlockSpec(wuv_p.shape, lambda s, pt: (0, 0, 0)),
            pl.BlockSpec(memory_space=pl.ANY),
            pl.BlockSpec(memory_space=pl.ANY),
        ],
        out_specs=pl.BlockSpec((ts, MLA_HEADS * MLA_V), lambda s, pt: (s, 0)),
        scratch_shapes=[
            pltpu.VMEM((2, P, PAGE_SIZE, MLA_KV_RANK), F32),
            pltpu.VMEM((2, P, MLA_ROPE, PAGE_SIZE), F32),
            pltpu.SemaphoreType.DMA((2,)),
        ],
    )
    return pl.pallas_call(
        functools.partial(_attn_sample_body, layer=l, npages=P, ts=ts),
        grid_spec=grid_spec,
        out_shape=jax.ShapeDtypeStruct((S * ts, MLA_HEADS * MLA_V), F32),
        compiler_params=_cp(("arbitrary",)),
        name="attn_sample",
    )(page_table, qlat, qpe, lat, kpe, wuv_p, cache_lat, cache_pe_t)


def _scan_tiling(B, T, block):
    bb = min(B, SCAN_BATCH)
    sb = block if T % block == 0 else 8
    assert B % bb == 0 and T % sb == 0
    return bb, _pick(T, 256), sb


def _block_cumsum_matrices(tc, sb):
    t = jnp.arange(tc, dtype=jnp.int32)
    same = (t[:, None] // sb) == (t[None, :] // sb)
    return (jnp.logical_and(same, t[None, :] <= t[:, None]).astype(F32), same.astype(F32))


def _rwkv_body(zr_ref, zk_ref, zv_ref, zl_ref, sh0_ref, s0_ref, mu_ref, w0_ref, w2_ref, a0_ref, a2_ref,
               g2_ref, kk_ref, ka_ref, rk_ref, lng_ref, lnb_ref, blk_ref, ltri_ref, bones_ref, o_ref, sout_ref,
               carry_s, st_s, r_s, k_s, v_s, g_s, ae_s, re_s, bq_s, kq_s, bh_s, kh_s, gb_s, y_s, *, bb, tc, sb):
    c = pl.program_id(1)
    W = RWKV_WIDTH
    npair = RWKV_HEADS // 2

    @pl.when(c == 0)
    def _():
        for b in range(bb):
            carry_s[b] = jnp.broadcast_to(sh0_ref[b], (8, RWKV_PROJ))
        st_s[...] = s0_ref[...].reshape(st_s.shape)

    first_row = lax.broadcasted_iota(jnp.int32, (tc, 1), 0) == 0
    def head_sum(x):
        if x.shape[0] <= 64:
            return _lane_group_sum(x, RWKV_HEAD)
        hi = x.astype(BF16)
        lo = (x - hi.astype(F32)).astype(BF16)
        return (jnp.dot(hi, blk_ref[...], preferred_element_type=F32)
                + jnp.dot(lo, blk_ref[...], preferred_element_type=F32))

    def mix(z, off):
        prev = jnp.where(first_row, carry_s[b][0:1, off:off + z.shape[1]], pltpu.roll(z, 1, 0))
        return z + (prev - z) * mu_ref[:, off:off + z.shape[1]]

    for b in range(bb):
        zr, zk, zv, zl = zr_ref[b], zk_ref[b], zv_ref[b], zl_ref[b]
        r = mix(zr, 0)
        k = mix(zk, W)
        v = mix(zv, 2 * W)
        lo = mix(zl, 3 * W)
        carry_s[b] = jnp.concatenate([zr[tc - 8:], zk[tc - 8:], zv[tc - 8:], zl[tc - 8:]], axis=1)[7:8] \
            + jnp.zeros((8, RWKV_PROJ), F32)
        wa = lo[:, :128]
        lw = jnp.dot(jnp.tanh(wa).astype(BF16), w2_ref[...], preferred_element_type=F32)
        la = jnp.dot(wa.astype(BF16), a2_ref[...], preferred_element_type=F32)
        g = jnp.dot(jax.nn.sigmoid(lo[:, 128:]).astype(BF16), g2_ref[...], preferred_element_type=F32)
        log_w = -jnp.exp(-_softplus(-(w0_ref[...] + lw)) - 0.5)
        a = jax.nn.sigmoid(a0_ref[...] + la)
        kk = k * kk_ref[...]
        kk = kk / jnp.maximum(jnp.sqrt(head_sum(kk * kk)), 1e-12)
        k = k * (1.0 + (a - 1.0) * ka_ref[...])
        cw = jnp.dot(ltri_ref[...], log_w, precision=HIGHEST, preferred_element_type=F32)
        cl = jnp.dot(bones_ref[...], log_w, precision=HIGHEST, preferred_element_type=F32)
        inv_g = jnp.exp(-cw)
        to_end = jnp.exp(cl - cw)
        r_s[b], k_s[b], v_s[b], g_s[b] = r, k, v, g
        ae_s[b] = -kk * jnp.exp(cw - log_w)
        re_s[b] = r * jnp.exp(cw)
        bq_s[b] = kk * a * inv_g
        kq_s[b] = k * inv_g
        bh_s[b] = kk * a * to_end
        kh_s[b] = k * to_end
        gb_s[b] = jnp.exp(cl)

    lo_half = lax.broadcasted_iota(jnp.int32, (sb, 128), 1) < RWKV_HEAD
    lo_state = lax.broadcasted_iota(jnp.int32, (RWKV_HEAD, 128), 1) < RWKV_HEAD
    pairs = [(b, p) for b in range(bb) for p in range(npair)]
    npairs = len(pairs)
    crow = lax.broadcasted_iota(jnp.int32, (npairs * 4 * sb, npairs * 2 * sb), 0)
    ccol = lax.broadcasted_iota(jnp.int32, (npairs * 4 * sb, npairs * 2 * sb), 1)
    same_pair = lax.shift_right_logical(crow, (4 * sb).bit_length() - 1) \
        == lax.shift_right_logical(ccol, (2 * sb).bit_length() - 1)
    r_side = jnp.where(jnp.bitwise_and(crow, 2 * sb) != 0, 1, 0)
    causal = jnp.bitwise_and(crow, sb - 1) > jnp.bitwise_and(ccol, sb - 1) - r_side
    coef_mask = jnp.logical_and(same_pair, causal)
    lo_of = lambda x: jnp.where(lo_half, x, 0.0)
    hi_of = lambda x: jnp.where(lo_half, 0.0, x)
    zeros_sb = jnp.zeros((sb, 128), F32)
    zero_tile = jnp.zeros((2 * sb, 128), BF16)
    cat = jnp.concatenate

    def block(bi, carry):
        t0 = pl.multiple_of(bi * sb, sb)
        tile = lambda ref, i: ref[pairs[i][0], pl.ds(t0, sb), 128 * pairs[i][1]:128 * (pairs[i][1] + 1)]
        rng = range(npairs)
        aes, res, vbs = ([tile(ref, i) for i in rng] for ref in (ae_s, re_s, v_s))
        lhs = cat([cat([lo_of(aes[i]), hi_of(aes[i]), lo_of(res[i]), hi_of(res[i])], axis=0) for i in rng], axis=0)
        rhs = cat([cat([tile(bq_s, i), tile(kq_s, i)], axis=0) for i in rng], axis=0)
        coef = jnp.where(coef_mask, lax.dot_general(lhs.astype(BF16), rhs.astype(BF16), _NT,
                                                    preferred_element_type=F32), 0.0)
        rows_of = lambda k: cat([coef[(4 * i + k) * sb:(4 * i + k + 1) * sb] for i in rng], axis=0)
        v_rows = cat([cat([zeros_sb, lo_of(v)], axis=0) for v in vbs] + [cat([zeros_sb, hi_of(v)], axis=0) for v in vbs],
                     axis=0).astype(BF16)
        x_v = jnp.dot(cat([rows_of(0), rows_of(1)], axis=1).astype(BF16), v_rows, preferred_element_type=F32)
        states = [st_s[b * npair + p] for b, p in pairs]
        s_bd = cat([cat([jnp.where(lo_state, S, 0.0), jnp.where(lo_state, 0.0, S)], axis=0).astype(BF16)
                    for S in states], axis=1)
        ar_bd = cat([cat([zero_tile] * i + [cat([aes[i], res[i]], axis=0).astype(BF16)]
                         + [zero_tile] * (npairs - 1 - i), axis=1) for i in rng], axis=0)
        xy = lax.dot_general(ar_bd, s_bd, _NT, preferred_element_type=F32)
        xs = [xy[2 * sb * i:2 * sb * i + sb] + x_v[sb * i:sb * (i + 1)] for i in rng]
        cols = [[jnp.where(lo_half,
                           coef[4 * sb * i:4 * sb * i + sb, 2 * sb * i + u:2 * sb * i + u + 1],
                           coef[4 * sb * i + sb:4 * sb * i + 2 * sb, 2 * sb * i + u:2 * sb * i + u + 1])
                 for u in range(sb - 1)] for i in rng]
        for u in range(sb - 1):
            xs = [x + cols[i][u] * x[u:u + 1, :] for i, x in enumerate(xs)]
        pv_rows = cat([cat([lo_of(xs[i]), lo_of(vbs[i])], axis=0) for i in rng]
                      + [cat([hi_of(xs[i]), hi_of(vbs[i])], axis=0) for i in rng], axis=0).astype(BF16)
        y_add = jnp.dot(cat([rows_of(2), rows_of(3)], axis=1).astype(BF16), pv_rows, preferred_element_type=F32)
        for i, (b, p) in enumerate(pairs):
            x, vb = xs[i], vbs[i]
            y_s[b, pl.ds(t0, sb), 128 * p:128 * (p + 1)] = (xy[2 * sb * i + sb:2 * sb * (i + 1)]
                                                            + y_add[sb * i:sb * (i + 1)])
            upd = lax.dot_general(jnp.concatenate([x, vb], axis=0).astype(BF16),
                                  jnp.concatenate([tile(bh_s, i), tile(kh_s, i)], axis=0).astype(BF16),
                                  (((0,), (0,)), ((), ())), preferred_element_type=F32)
            st_s[b * npair + p] = (states[i] * tile(gb_s, i)[0:1, :]
                                   + jnp.where(lo_state, upd[0:RWKV_HEAD], upd[RWKV_HEAD:]))
        return carry

    lax.fori_loop(0, tc // sb, block, 0)

    for b in range(bb):
        y = y_s[b]
        mean = head_sum(y) * (1.0 / RWKV_HEAD)
        d = y - mean
        var = head_sum(d * d) * (1.0 / RWKV_HEAD)
        yn = d * lax.rsqrt(var + RWKV_GN_EPS) * lng_ref[...] + lnb_ref[...]
        bonus = head_sum(r_s[b] * k_s[b] * rk_ref[...]) * v_s[b]
        o_ref[b] = ((yn + bonus) * g_s[b]).astype(o_ref.dtype)

    @pl.when(c == pl.num_programs(1) - 1)
    def _():
        sout_ref[...] = st_s[...].reshape(sout_ref.shape)


def _pack_pairs(s):
    B, H, R, C = s.shape
    return s.reshape(B, H // 2, 2, R, C).transpose(0, 1, 3, 2, 4).reshape(B, H // 2, R, 2 * C)


def _unpack_pairs(s):
    B, HP, R, C2 = s.shape
    return s.reshape(B, HP, R, 2, C2 // 2).transpose(0, 1, 3, 2, 4).reshape(B, HP * 2, R, C2 // 2)


def _rwkv_call(st, z, shift0, s0, prm, out_dtype):
    B, T = st.B, st.T
    z3 = z.reshape(B, T, Z_WIDTH)
    bb, tc, sb = _scan_tiling(B, T, RWKV_BLOCK)
    W = RWKV_WIDTH
    npair = RWKV_HEADS // 2
    s0p = _pack_pairs(s0)
    zspec = lambda w, idx: pl.BlockSpec((bb, tc, w), lambda g, c: (g, c, idx))
    const = lambda a: pl.BlockSpec(a.shape, lambda g, c: (0,) * a.ndim)
    consts = [prm[k] for k in ("mu", "w0", "w2", "a0", "a2", "g2", "k_k", "k_a", "r_k", "lnx_g", "lnx_b")]
    head_of = jnp.arange(W, dtype=jnp.int32) // RWKV_HEAD
    consts += [(head_of[:, None] == head_of[None, :]).astype(BF16), *_block_cumsum_matrices(tc, sb)]
    o, sout = pl.pallas_call(
        functools.partial(_rwkv_body, bb=bb, tc=tc, sb=sb),
        grid=(B // bb, T // tc),
        in_specs=[
            zspec(W, Z_RWKV_R), zspec(W, Z_RWKV_K), zspec(W, Z_RWKV_V), zspec(256, Z_RWKV_LORA),
            pl.BlockSpec((bb, 1, RWKV_PROJ), lambda g, c: (g, 0, 0)),
            pl.BlockSpec((bb, npair, RWKV_HEAD, 128), lambda g, c: (g, 0, 0, 0)),
        ] + [const(a) for a in consts],
        out_specs=[
            pl.BlockSpec((bb, tc, W), lambda g, c: (g, c, 0)),
            pl.BlockSpec((bb, npair, RWKV_HEAD, 128), lambda g, c: (g, 0, 0, 0)),
        ],
        out_shape=[
            jax.ShapeDtypeStruct((B, T, W), out_dtype),
            jax.ShapeDtypeStruct((B, npair, RWKV_HEAD, 128), F32),
        ],
        scratch_shapes=[
            pltpu.VMEM((bb, 8, RWKV_PROJ), F32),
            pltpu.VMEM((bb * npair, RWKV_HEAD, 128), F32),
        ] + [pltpu.VMEM((bb, tc, W), F32)] * 12,
        compiler_params=_cp(("parallel", "arbitrary")),
        name="rwkv7",
    )(z3, z3, z3, z3, shift0.reshape(B, 1, RWKV_PROJ), s0p, *consts)
    return o.reshape(B * T, W), _unpack_pairs(sout)


def _gla_body(zqk_ref, zv_ref, zgr_ref, zgl_ref, s0_ref, wg_ref, bg_ref, ng_ref, ltri_ref, bones_ref,
              o_ref, sout_ref, st_s, q_s, k_s, cw_s, qe_s, ke_s, ab_s, v_s, y_s, *, bb, tc, sb):
    c = pl.program_id(1)
    npair = GLA_HEADS // 2
    HK = GLA_HEADS * GLA_DK

    @pl.when(c == 0)
    def _():
        st_s[...] = s0_ref[...].reshape(st_s.shape)

    for b in range(bb):
        zqk = zqk_ref[b]
        gate = jnp.dot(zgl_ref[b].astype(BF16), wg_ref[...], preferred_element_type=F32) + bg_ref[...]
        log_a = -_softplus(-gate) * (1.0 / GLA_TAU)
        cw = jnp.dot(ltri_ref[...], log_a, precision=HIGHEST, preferred_element_type=F32)
        cl = jnp.dot(bones_ref[...], log_a, precision=HIGHEST, preferred_element_type=F32)
        q = zqk[:, :HK] * (GLA_DK ** -0.5)
        k = zqk[:, HK:]
        q_s[b], k_s[b], cw_s[b], v_s[b] = q, k, cw, zv_ref[b]
        qe_s[b] = q * jnp.exp(cw)
        ke_s[b] = k * jnp.exp(cl - cw)
        ab_s[b] = jnp.exp(cl)

    lo_half = lax.broadcasted_iota(jnp.int32, (sb, 128), 1) < GLA_DK
    t_idx = lax.broadcasted_iota(jnp.int32, (sb, 128), 0)
    lo_of = lambda x: jnp.where(lo_half, x, 0.0)
    hi_of = lambda x: jnp.where(lo_half, 0.0, x)

    def block(bi, carry):
        t0 = pl.multiple_of(bi * sb, sb)
        for b in range(bb):
            for p in range(npair):
                cs = slice(128 * p, 128 * (p + 1))
                v0 = slice(2 * GLA_DV * p, 2 * GLA_DV * p + GLA_DV)
                v1 = slice(2 * GLA_DV * p + GLA_DV, 2 * GLA_DV * (p + 1))
                qb, kb, cwb, qeb, keb, abb = (x[b, pl.ds(t0, sb), cs] for x in (q_s, k_s, cw_s, qe_s, ke_s, ab_s))
                v0b = v_s[b, pl.ds(t0, sb), v0]
                v1b = v_s[b, pl.ds(t0, sb), v1]
                S = st_s[b * npair + p]
                inter = lax.dot_general(jnp.concatenate([lo_of(qeb), hi_of(qeb)], axis=0).astype(BF16),
                                        S.astype(BF16), _NT, preferred_element_type=F32)
                o0 = inter[0:sb]
                o1 = inter[sb:]
                for u in range(sb):
                    decay = jnp.exp(jnp.minimum(cwb - cwb[u:u + 1, :], 0.0))
                    w_tu = jnp.where(t_idx >= u, qb * decay * kb[u:u + 1, :], 0.0)
                    o0 = o0 + jnp.sum(lo_of(w_tu), -1, keepdims=True) * v0b[u:u + 1, :]
                    o1 = o1 + jnp.sum(hi_of(w_tu), -1, keepdims=True) * v1b[u:u + 1, :]
                y_s[b, pl.ds(t0, sb), v0] = o0
                y_s[b, pl.ds(t0, sb), v1] = o1
                upd = lax.dot_general(jnp.concatenate([v0b, v1b], axis=0).astype(BF16),
                                      jnp.concatenate([lo_of(keb), hi_of(keb)], axis=0).astype(BF16),
                                      (((0,), (0,)), ((), ())), preferred_element_type=F32)
                st_s[b * npair + p] = S * abb[0:1, :] + upd
        return carry

    lax.fori_loop(0, tc // sb, block, 0)

    for b in range(bb):
        gr = zgr_ref[b]
        for h in range(GLA_HEADS):
            hs = slice(h * GLA_DV, (h + 1) * GLA_DV)
            o = y_s[b, :, hs]
            on = o * lax.rsqrt(jnp.mean(o * o, -1, keepdims=True) + RMS_EPS) * ng_ref[...]
            o_ref[b, :, hs] = (on * _silu(gr[:, hs])).astype(o_ref.dtype)

    @pl.when(c == pl.num_programs(1) - 1)
    def _():
        sout_ref[...] = st_s[...].reshape(sout_ref.shape)


def _gla_call(st, z, s0, wg_p, bg, ng, out_dtype):
    B, T = st.B, st.T
    z3 = z.reshape(B, T, Z_WIDTH)
    bb, tc, sb = _scan_tiling(B, T, GLA_BLOCK)
    npair = GLA_HEADS // 2
    HK = GLA_HEADS * GLA_DK
    s0p = _pack_pairs(s0.transpose(0, 1, 3, 2))
    zspec = lambda w, idx: pl.BlockSpec((bb, tc, w), lambda g, c: (g, c, idx))
    const = lambda a: pl.BlockSpec(a.shape, lambda g, c: (0,) * a.ndim)
    ltri, bones = _block_cumsum_matrices(tc, sb)
    o, sout = pl.pallas_call(
        functools.partial(_gla_body, bb=bb, tc=tc, sb=sb),
        grid=(B // bb, T // tc),
        in_specs=[
            zspec(512, Z_GLA_QK), zspec(512, Z_GLA_V), zspec(512, Z_GLA_GR), zspec(128, Z_GLA_GL),
            pl.BlockSpec((bb, npair, GLA_DV, 128), lambda g, c: (g, 0, 0, 0)),
            const(wg_p), const(bg), const(ng), const(ltri), const(bones),
        ],
        out_specs=[
            pl.BlockSpec((bb, tc, GLA_WIDTH), lambda g, c: (g, c, 0)),
            pl.BlockSpec((bb, npair, GLA_DV, 128), lambda g, c: (g, 0, 0, 0)),
        ],
        out_shape=[
            jax.ShapeDtypeStruct((B, T, GLA_WIDTH), out_dtype),
            jax.ShapeDtypeStruct((B, npair, GLA_DV, 128), F32),
        ],
        scratch_shapes=[
            pltpu.VMEM((bb * npair, GLA_DV, 128), F32),
        ] + [pltpu.VMEM((bb, tc, HK), F32)] * 6 + [pltpu.VMEM((bb, tc, GLA_WIDTH), F32)] * 2,
        compiler_params=_cp(("parallel", "arbitrary")),
        name="gla",
    )(z3, z3, z3, z3, s0p, wg_p, bg, ng, ltri, bones)
    return o.reshape(B * T, GLA_WIDTH), _unpack_pairs(sout).transpose(0, 1, 3, 2)


def _outproj_body(om_ref, or_ref, og_ref, x_ref, g1_ref, sh2_ref, sc2_ref, w_ref, lng_ref, lnb_ref,
                  x1_ref, h2_ref, *, alpha):
    wm = MLA_HEADS * MLA_V
    mix = (jnp.dot(om_ref[...].astype(BF16), w_ref[0:wm, :], preferred_element_type=F32)
           + jnp.dot(or_ref[...].astype(BF16), w_ref[wm:wm + RWKV_WIDTH, :], preferred_element_type=F32)
           + jnp.dot(og_ref[...].astype(BF16), w_ref[wm + RWKV_WIDTH:, :], preferred_element_type=F32))
    x1 = _layernorm(alpha * x_ref[...] + (1.0 + g1_ref[...]) * mix, lng_ref[...], lnb_ref[...])
    x1_ref[...] = x1
    h2_ref[...] = x1 * (1.0 + sc2_ref[...]) + sh2_ref[...]


def _outproj_call(st, o_mla, o_rwkv, o_gla, x2d, l, w_out_b, ln_g, ln_b, alpha):
    N, D = x2d.shape
    bm = st.row_block(256)
    row = lambda w: pl.BlockSpec((bm, w), lambda i: (i, 0))
    return pl.pallas_call(
        functools.partial(_outproj_body, alpha=alpha),
        grid=(N // bm,),
        in_specs=[
            row(o_mla.shape[1]), row(o_rwkv.shape[1]), row(o_gla.shape[1]), row(D),
            st.mod_spec(l, 2, bm, D), st.mod_spec(l, 3, bm, D), st.mod_spec(l, 4, bm, D),
            pl.BlockSpec((None,) + w_out_b.shape[1:], lambda i: (l, 0, 0)),
            pl.BlockSpec((None, 1, D), lambda i: (l, 0, 0)),
            pl.BlockSpec((None, 1, D), lambda i: (l, 0, 0)),
        ],
        out_specs=[row(D), row(D)],
        out_shape=[jax.ShapeDtypeStruct((N, D), F32), jax.ShapeDtypeStruct((N, D), F32)],
        compiler_params=_cp(("parallel",)),
        name="out_proj",
    )(o_mla, o_rwkv, o_gla, x2d, st.mod, st.mod, st.mod, w_out_b, ln_g, ln_b)


def _router_body(x1_ref, sh2_ref, sc2_ref, wr_ref, o_ref):
    h = x1_ref[...] * (1.0 + sc2_ref[...]) + sh2_ref[...]
    logits = jnp.dot(h, wr_ref[...], precision=HIGHEST, preferred_element_type=F32)
    lane = lax.broadcasted_iota(jnp.int32, logits.shape, 1)
    lane_f = lane.astype(F32)
    lg = jnp.where(lane < N_EXPERTS, logits, -jnp.inf)
    v1 = jnp.max(lg, -1, keepdims=True)
    i1 = jnp.min(jnp.where(lg == v1, lane_f, 128.0), -1, keepdims=True)
    lg2 = jnp.where(lane_f == i1, -jnp.inf, lg)
    v2 = jnp.max(lg2, -1, keepdims=True)
    i2 = jnp.min(jnp.where(lg2 == v2, lane_f, 128.0), -1, keepdims=True)
    e = jnp.exp(v2 - v1)
    g1 = 1.0 / (1.0 + e)
    g2 = e * g1
    o_ref[...] = jnp.where(lane == 0, g1, jnp.where(lane == 1, g2, jnp.where(
        lane == 2, i1, jnp.where(lane == 3, i2, 0.0))))


def _router_call(st, x1, l, wr_p):
    N, D = x1.shape
    bm = st.row_block(512)
    return pl.pallas_call(
        _router_body,
        grid=(N // bm,),
        in_specs=[
            pl.BlockSpec((bm, D), lambda i: (i, 0)),
            st.mod_spec(l, 3, bm, D), st.mod_spec(l, 4, bm, D),
            pl.BlockSpec(wr_p.shape, lambda i: (0, 0)),
        ],
        out_specs=pl.BlockSpec((bm, 128), lambda i: (i, 0)),
        out_shape=jax.ShapeDtypeStruct((N, 128), F32),
        compiler_params=_cp(("parallel",)),
        name="moe_router",
    )(x1, st.mod, st.mod, wr_p)


def _ffn_up_body(te_ref, tv_ref, h_ref, w1_ref, w3_ref, o_ref):
    t = pl.program_id(0)

    @pl.when(tv_ref[t] != 0)
    def _():
        h = h_ref[...].astype(BF16)
        a = jnp.dot(h, w1_ref[...].astype(BF16), preferred_element_type=F32)
        b = jnp.dot(h, w3_ref[...].astype(BF16), preferred_element_type=F32)
        o_ref[...] = (_silu(a) * b).astype(BF16)

    @pl.when(tv_ref[t] == 0)
    def _():
        o_ref[...] = jnp.zeros(o_ref.shape, BF16)


def _ffn_down_body(te_ref, tv_ref, g_ref, w2_ref, o_ref):
    t = pl.program_id(0)

    @pl.when(tv_ref[t] != 0)
    def _():
        o_ref[...] = jnp.dot(g_ref[...], w2_ref[...].astype(BF16), preferred_element_type=F32)

    @pl.when(tv_ref[t] == 0)
    def _():
        o_ref[...] = jnp.zeros(o_ref.shape, F32)


def _ffn_up_gather_body(te_ref, tv_ref, src_ref, h_hbm, w1_ref, w3_ref, o_ref, rows_s, hb_s, sem):
    t = pl.program_id(0)
    bm = rows_s.shape[0]

    @pl.when(jnp.logical_and(pl.program_id(1) == 0, tv_ref[t] != 0))
    def _():
        def issue(r, carry):
            pltpu.make_async_copy(h_hbm.at[pl.ds(src_ref[t * bm + r], 1)], rows_s.at[pl.ds(r, 1)], sem).start()
            return carry

        lax.fori_loop(0, bm, issue, 0, unroll=8)
        pltpu.make_async_copy(h_hbm.at[pl.ds(0, bm)], rows_s, sem).wait()
        hb_s[...] = rows_s[...].astype(BF16)

    @pl.when(tv_ref[t] != 0)
    def _():
        h = hb_s[...]
        a = jnp.dot(h, w1_ref[...].astype(BF16), preferred_element_type=F32)
        b = jnp.dot(h, w3_ref[...].astype(BF16), preferred_element_type=F32)
        o_ref[...] = (_silu(a) * b).astype(BF16)

    @pl.when(tv_ref[t] == 0)
    def _():
        o_ref[...] = jnp.zeros(o_ref.shape, BF16)


def _ffn_call(h, tile_expert, tile_valid, w1, w3, w2, bm, src=None):
    D = h.shape[1]
    F = w1.shape[-1]
    R = h.shape[0] if src is None else src.shape[0]
    nt = R // bm
    bf = 512
    if src is None:
        up = pl.pallas_call(
            _ffn_up_body,
            grid_spec=pltpu.PrefetchScalarGridSpec(
                num_scalar_prefetch=2,
                grid=(nt, F // bf),
                in_specs=[
                    pl.BlockSpec((bm, D), lambda t, j, te, tv: (t, 0)),
                    pl.BlockSpec((None, D, bf), lambda t, j, te, tv: (te[t], 0, j)),
                    pl.BlockSpec((None, D, bf), lambda t, j, te, tv: (te[t], 0, j)),
                ],
                out_specs=pl.BlockSpec((bm, bf), lambda t, j, te, tv: (t, j)),
            ),
            out_shape=jax.ShapeDtypeStruct((R, F), BF16),
            compiler_params=_cp(("parallel", "arbitrary")),
            name="ffn_up",
        )(tile_expert, tile_valid, h, w1, w3)
    else:
        nj = F // bf
        up = pl.pallas_call(
            _ffn_up_gather_body,
            grid_spec=pltpu.PrefetchScalarGridSpec(
                num_scalar_prefetch=3,
                grid=(nt, F // bf),
                in_specs=[
                    pl.BlockSpec(memory_space=pl.ANY),
                    pl.BlockSpec((None, D, bf), lambda t, j, te, tv, sr: (te[t], 0, jnp.where(tv[t] != 0, j, nj - 1))),
                    pl.BlockSpec((None, D, bf), lambda t, j, te, tv, sr: (te[t], 0, jnp.where(tv[t] != 0, j, nj - 1))),
                ],
                out_specs=pl.BlockSpec((bm, bf), lambda t, j, te, tv, sr: (t, j)),
                scratch_shapes=[pltpu.VMEM((bm, D), h.dtype), pltpu.VMEM((bm, D), BF16),
                                pltpu.SemaphoreType.DMA(())],
            ),
            out_shape=jax.ShapeDtypeStruct((R, F), BF16),
            compiler_params=_cp(("arbitrary", "arbitrary")),
            name="ffn_up_gather",
        )(tile_expert, tile_valid, src, h, w1, w3)
    bn = min(512 if bm <= 768 else 256, D)
    assert D % bn == 0 and F % bf == 0
    return pl.pallas_call(
        _ffn_down_body,
        grid_spec=pltpu.PrefetchScalarGridSpec(
            num_scalar_prefetch=2,
            grid=(nt, D // bn),
            in_specs=[
                pl.BlockSpec((bm, F), lambda t, n, te, tv: (t, 0)),
                pl.BlockSpec((None, F, bn), lambda t, n, te, tv: (te[t], 0, jnp.where(tv[t] != 0, n, D // bn - 1))),
            ],
            out_specs=pl.BlockSpec((bm, bn), lambda t, n, te, tv: (t, n)),
        ),
        out_shape=jax.ShapeDtypeStruct((R, D), F32),
        compiler_params=_cp(("parallel", "arbitrary")),
        name="ffn_down",
    )(tile_expert, tile_valid, up, w2)


def _combine_body(*refs, alpha, nterm):
    x1_ref, g2_ref, lng_ref, lnb_ref = refs[:4]
    f_refs = refs[4:4 + nterm]
    o_ref = refs[-1]
    if nterm == 1:
        f = f_refs[0][...]
    else:
        gates = refs[4 + nterm][...]
        f = f_refs[0][...] * gates[:, 0:1] + f_refs[1][...] * gates[:, 1:2]
    o_ref[...] = _layernorm(alpha * x1_ref[...] + (1.0 + g2_ref[...]) * f, lng_ref[...], lnb_ref[...])


def _combine_call(st, x1, l, ln_g, ln_b, alpha, terms, gates=None, row0=0):
    N, D = x1.shape
    bm = st.row_block(512)
    assert row0 % bm == 0
    row = lambda w: pl.BlockSpec((bm, w), lambda i: (i, 0))
    shared = lambda w: pl.BlockSpec((bm, w), lambda i: (i + row0 // bm, 0))
    args = [x1, st.mod, ln_g, ln_b] + list(terms) + ([gates] if gates is not None else [])
    return pl.pallas_call(
        functools.partial(_combine_body, alpha=alpha, nterm=len(terms)),
        grid=(N // bm,),
        in_specs=[
            row(D), st.mod_spec(l, 5, bm, D),
            pl.BlockSpec((None, 1, D), lambda i: (l, 0, 0)),
            pl.BlockSpec((None, 1, D), lambda i: (l, 0, 0)),
        ] + [shared(D)] * len(terms) + ([shared(128)] if gates is not None else []),
        out_specs=row(D),
        out_shape=jax.ShapeDtypeStruct((N, D), F32),
        compiler_params=_cp(("parallel",)),
        name="ffn_residual",
    )(*args)


def _repack_w_in(w_in):
    L, D, _ = w_in.shape
    w = w_in.astype(BF16)
    zeros = lambda n: jnp.zeros((L, D, n), BF16)
    r0 = MLA_IN
    g0 = MLA_IN + RWKV_PROJ
    HK = GLA_HEADS * GLA_DK
    gq, gk, gv = g0, g0 + HK, g0 + 2 * HK
    ggl = gv + GLA_WIDTH
    ggr = ggl + GLA_GATE_LORA
    parts = [
        w[:, :, :MLA_IN], zeros(Z_MLA_W - MLA_IN),
        w[:, :, r0:r0 + RWKV_PROJ],
        w[:, :, ggl:ggr], zeros(256 - GLA_GATE_LORA),
        w[:, :, gq:gv], w[:, :, gv:ggl], w[:, :, ggr:ggr + GLA_WIDTH],
    ]
    out = jnp.concatenate(parts, axis=-1)
    assert out.shape[-1] == Z_WIDTH
    return out


def _moe_plan(gi, bm):
    n = gi.shape[0]
    E = N_EXPERTS
    experts = gi[:, 2:4].astype(jnp.int32).reshape(-1)
    onehot = (experts[:, None] == jnp.arange(E, dtype=jnp.int32)[None, :]).astype(jnp.int32)
    counts = jnp.sum(onehot, axis=0)
    tiles = (counts + bm - 1) // bm
    tile_end = jnp.cumsum(tiles)
    tile_start = tile_end - tiles
    rank = jnp.take_along_axis(jnp.cumsum(onehot, axis=0) - 1, experts[:, None], axis=1)[:, 0]
    pos = tile_start[experts] * bm + rank
    nt = (2 * n + bm - 1) // bm + E
    src = jnp.zeros((nt * bm,), jnp.int32).at[pos].set(jnp.arange(2 * n, dtype=jnp.int32) // 2)
    t_ids = jnp.arange(nt, dtype=jnp.int32)
    tile_expert = jnp.minimum(jnp.sum((t_ids[:, None] >= tile_end[None, :]).astype(jnp.int32), axis=1), E - 1)
    tile_valid = (t_ids < tile_end[-1]).astype(jnp.int32)
    tile_expert = jnp.where(tile_valid != 0, tile_expert, tile_expert[jnp.maximum(tile_end[-1] - 1, 0)])
    return src, pos.reshape(n, 2), tile_expert, tile_valid


def kernel(x_prompt, x_sample, cache_kv_latent, cache_k_rope, state_rwkv, state_rwkv_shift, state_gla, page_table, c_prompt, c_sample, w_in, w_out, mla_q_norm, mla_kv_norm, mla_w_q_up, mla_w_uk, mla_w_uv, rwkv_mu, rwkv_w0, rwkv_w2, rwkv_a0, rwkv_a2, rwkv_g2, rwkv_k_k, rwkv_k_a, rwkv_r_k, rwkv_lnx_g, rwkv_lnx_b, gla_w_g2, gla_b_g, gla_norm_g, ada_w, ada_b, ln1_g, ln1_b, ln2_g, ln2_b, ffn_w1, ffn_w3, ffn_w2, moe_router, moe_w1, moe_w3, moe_w2):
    Bp, Tp, D = x_prompt.shape
    Bs, Ts, _ = x_sample.shape
    L = w_in.shape[0]
    past_len = page_table.shape[1] * PAGE_SIZE
    alpha = (2 * L) ** 0.25
    dt = x_prompt.dtype

    n_c = Bp + Bs
    c_all = jnp.concatenate([c_prompt, c_sample, jnp.zeros((-n_c % 8, D), F32)], axis=0)
    mod = _ada_call(c_all, ada_w, ada_b)
    st_p = _Stream(Bp, Tp, mod[:, :Bp].reshape(L, Bp, 1, 6 * D), False, 0)
    st_s = _Stream(Bs, Ts, jnp.repeat(mod[:, Bp:Bp + Bs], Ts, axis=1), True, past_len)

    w_in_p = _repack_w_in(w_in)
    w_out_b = w_out.astype(BF16)
    wq = mla_w_q_up.astype(BF16)
    wq_p = jnp.concatenate([wq[..., :MLA_NOPE].reshape(L, MLA_Q_RANK, -1),
                            wq[..., MLA_NOPE:].reshape(L, MLA_Q_RANK, -1)], axis=-1)
    wuk_p = mla_w_uk.astype(BF16).transpose(0, 2, 3, 1)
    wuv_p = mla_w_uv.astype(BF16).transpose(0, 2, 1, 3)
    inv = 1.0 / (ROPE_BASE ** (jnp.arange(0, MLA_ROPE, 2, dtype=F32) / MLA_ROPE))
    inv128 = jnp.tile(inv, 4).reshape(1, 128)
    zpad = lambda a, n: jnp.concatenate([a, jnp.zeros((L, n) + a.shape[2:], a.dtype)], axis=1)
    zpre = lambda a, n: jnp.concatenate([jnp.zeros((L, n) + a.shape[2:], a.dtype), a], axis=1)
    rw_w2 = zpad(rwkv_w2.astype(BF16), RWKV_A_LORA)
    rw_a2 = zpre(rwkv_a2.astype(BF16), RWKV_W_LORA)
    gla_wg = zpad(gla_w_g2.astype(BF16), 128 - GLA_GATE_LORA)
    row1 = lambda a: a.reshape(1, -1)

    xp = x_prompt.reshape(Bp * Tp, D)
    xs = x_sample.reshape(Bs * Ts, D)
    zeros_shift = jnp.zeros((Bp, RWKV_PROJ), dt)
    zeros_rwkv = jnp.zeros((Bp, RWKV_HEADS, RWKV_HEAD, RWKV_HEAD), F32)
    zeros_gla = jnp.zeros((Bp, GLA_HEADS, GLA_DK, GLA_DV), F32)
    outs_p = [[] for _ in range(5)]
    outs_s = [[] for _ in range(5)]

    for l in range(L):
        rw = dict(mu=row1(rwkv_mu[l]), w0=row1(rwkv_w0[l]), w2=rw_w2[l], a0=row1(rwkv_a0[l]), a2=rw_a2[l],
                  g2=rwkv_g2[l].astype(BF16), k_k=row1(rwkv_k_k[l]), k_a=row1(rwkv_k_a[l]),
                  r_k=row1(rwkv_r_k[l]), lnx_g=row1(rwkv_lnx_g[l]), lnx_b=row1(rwkv_lnx_b[l]))
        streams = []
        for st, x2d, sample in ((st_p, xp, False), (st_s, xs, True)):
            z = _inproj_call(st, x2d, l, w_in_p)
            qlat, qpe, lat, kpe = _mla_prep_call(st, z, row1(mla_q_norm[l]), row1(mla_kv_norm[l]),
                                                 wq_p[l], wuk_p[l], inv128, F32 if sample else BF16)
            if sample:
                o_mla = _attn_sample_call(st, l, qlat, qpe, lat, kpe, cache_kv_latent, cache_k_rope,
                                          page_table, wuv_p[l])
                shift0, s_r0, s_g0 = state_rwkv_shift[l], state_rwkv[l], state_gla[l]
            else:
                o_mla = _attn_prompt_call(st, qlat, qpe, lat, kpe, wuv_p[l])
                shift0, s_r0, s_g0 = zeros_shift, zeros_rwkv, zeros_gla
            o_dt = F32 if sample else BF16
            o_rwkv, s_r = _rwkv_call(st, z, shift0, s_r0, rw, o_dt)
            o_gla, s_g = _gla_call(st, z, s_g0, gla_wg[l], row1(gla_b_g[l]), row1(gla_norm_g[l]), o_dt)
            x1, h2 = _outproj_call(st, o_mla, o_rwkv, o_gla, x2d, l, w_out_b, ln1_g.reshape(L, 1, D),
                                   ln1_b.reshape(L, 1, D), alpha)
            shift = z.reshape(st.B, st.T, Z_WIDTH)[:, -1, Z_MLA_W:Z_MLA_W + RWKV_PROJ]
            acc = outs_s if sample else outs_p
            for lst, val in zip(acc, (lat.reshape(st.B, st.T, -1), kpe.reshape(st.B, st.T, -1),
                                      s_r.astype(dt), shift, s_g.astype(dt))):
                lst.append(val)
            streams.append((st, x1, h2))

        (_, x1p, h2p), (_, x1s, h2s) = streams
        h_all = jnp.concatenate([h2p, h2s], axis=0)
        n_all = h_all.shape[0]
        n_p = h2p.shape[0]
        bm = _pick(n_all, 1024)
        lng, lnb = ln2_g.reshape(L, 1, D), ln2_b.reshape(L, 1, D)
        if l % 2 == 0:
            e = l // 2
            nt = n_all // bm
            f = _ffn_call(h_all, jnp.zeros((nt,), jnp.int32), jnp.ones((nt,), jnp.int32),
                          ffn_w1[e:e + 1], ffn_w3[e:e + 1], ffn_w2[e:e + 1], bm)
            xp = _combine_call(st_p, x1p, l, lng, lnb, alpha, [f])
            xs = _combine_call(st_s, x1s, l, lng, lnb, alpha, [f], row0=n_p)
        else:
            e = l // 2
            wr_p = jnp.concatenate([moe_router[e], jnp.zeros((D, 128 - N_EXPERTS), F32)], axis=1)
            gi = jnp.concatenate([_router_call(st_p, x1p, l, wr_p), _router_call(st_s, x1s, l, wr_p)], axis=0)
            src, pos, tile_expert, tile_valid = _moe_plan(gi, MOE_TILE_ROWS)
            f_sorted = _ffn_call(h_all, tile_expert, tile_valid, moe_w1[e], moe_w3[e], moe_w2[e], MOE_TILE_ROWS,
                                 src=src)
            f0 = f_sorted[pos[:, 0]]
            f1 = f_sorted[pos[:, 1]]
            xp = _combine_call(st_p, x1p, l, lng, lnb, alpha, [f0, f1], gi)
            xs = _combine_call(st_s, x1s, l, lng, lnb, alpha, [f0, f1], gi, row0=n_p)

    stack = lambda lst: jnp.stack(lst)
    return (xp.reshape(Bp, Tp, D), xs.reshape(Bs, Ts, D),
            *[stack(v) for v in outs_p], *[stack(v) for v in outs_s])
```
